```python
import math
import jax, jax.numpy as jnp
from jax import lax
import numpy as np

D_MODEL = 1024
BATCH = 4
SEQ = 4096
DEPTH = 4

N_MIXERS = 2
N_POOL_LAYERS = (DEPTH + 1) // 2
N_MLSTM_LAYERS = DEPTH // 2

POOL_WINDOWS = (2, 4, 8, 16)
N_POOL_GROUPS = len(POOL_WINDOWS)
POOL_GROUP_DIM = D_MODEL // N_POOL_GROUPS

N_HEADS = 4
DV = D_MODEL // N_HEADS
DQK = DV // 2
CHUNK = 64
QK_W = N_HEADS * DQK
V_W = N_HEADS * DV
MLSTM_IN_W = 2 * QK_W + 2 * V_W + 2 * N_HEADS

D_FF = int(math.ceil(8 * D_MODEL / 3 / 256) * 256)

EPS = 1e-6

kernel_name = "hybrid_pool_mlstm_swiglu_trunk"


def rms_norm(x, g):
    xf = x.astype(jnp.float32)
    y = xf * lax.rsqrt(jnp.mean(xf * xf, axis=-1, keepdims=True) + EPS)
    return y.astype(x.dtype) * g


def pool_mixer(x, w, scale):
    B, S, D = x.shape
    xg = x.reshape(B, S, N_POOL_GROUPS, POOL_GROUP_DIM).astype(jnp.float32)
    cs = jnp.cumsum(xg, axis=1)
    pos = jnp.arange(1, S + 1, dtype=jnp.float32)
    outs = []
    for g, win in enumerate(POOL_WINDOWS):
        c = cs[:, :, g]
        prev = jnp.pad(c[:, :S - win], ((0, 0), (win, 0), (0, 0)))
        cnt = jnp.minimum(pos, float(win))[None, :, None]
        outs.append((c - prev) / cnt)
    pooled = jnp.stack(outs, axis=2) - xg
    y = jnp.einsum('bsgc,gcd->bsgd', pooled.astype(x.dtype), w).reshape(B, S, D)
    return y * scale


def mlstm_chunkwise(q, k, v, i_pre, logf):
    B, H, S, _ = q.shape
    nc = S // CHUNK

    def to_chunks(a):
        a = a.reshape(a.shape[:2] + (nc, CHUNK) + a.shape[3:])
        return jnp.moveaxis(a, 2, 0)

    xs = (to_chunks(q), to_chunks(k), to_chunks(v), to_chunks(i_pre), to_chunks(logf))
    causal = jnp.tril(jnp.ones((CHUNK, CHUNK), dtype=bool))

    def step(carry, inp):
        C, n, m = carry
        qc, kc, vc, ic, fc = inp
        b = jnp.cumsum(fc, axis=-1)
        log_d = b[..., :, None] - b[..., None, :] + ic[..., None, :]
        log_d = jnp.where(causal, log_d, -jnp.inf)
        log_inter = b + m[..., None]
        m_t = jnp.maximum(log_inter, jnp.max(log_d, axis=-1))
        d = jnp.exp(log_d - m_t[..., None])
        inter = jnp.exp(log_inter - m_t)
        s = jnp.einsum('bhtk,bhsk->bhts', qc, kc) * d
        num = inter[..., None] * jnp.einsum('bhtk,bhkv->bhtv', qc, C) \
            + jnp.einsum('bhts,bhsv->bhtv', s, vc)
        den = inter * jnp.einsum('bhtk,bhk->bht', qc, n) + jnp.sum(s, axis=-1)
        h = num / jnp.maximum(jnp.abs(den), jnp.exp(-m_t))[..., None]
        m_new = m_t[..., -1]
        decay = jnp.exp(b[..., -1] + m - m_new)
        w = jnp.exp(b[..., -1:] - b + ic - m_new[..., None])
        kw = kc * w[..., None]
        C_new = decay[..., None, None] * C + jnp.einsum('bhsk,bhsv->bhkv', kw, vc)
        n_new = decay[..., None] * n + jnp.sum(kw, axis=2)
        return (C_new, n_new, m_new), h

    init = (jnp.zeros((B, H, DQK, DV), jnp.float32),
            jnp.zeros((B, H, DQK), jnp.float32),
            jnp.zeros((B, H), jnp.float32))
    _, hs = lax.scan(step, init, xs)
    return jnp.moveaxis(hs, 0, 2).reshape(B, H, S, DV)


def mlstm_mixer(x, w_in, gate_bias, head_norm, w_out):
    B, S, _ = x.shape
    proj = x @ w_in
    q, k, v, o, gates = jnp.split(proj, [QK_W, 2 * QK_W, 2 * QK_W + V_W, 2 * QK_W + 2 * V_W], axis=-1)
    gates = gates.astype(jnp.float32) + gate_bias.astype(jnp.float32)
    i_pre = jnp.transpose(gates[..., :N_HEADS], (0, 2, 1))
    logf = jax.nn.log_sigmoid(jnp.transpose(gates[..., N_HEADS:], (0, 2, 1)))

    def heads(a, d):
        return jnp.transpose(a.reshape(B, S, N_HEADS, d), (0, 2, 1, 3)).astype(jnp.float32)

    qh = heads(q, DQK) * (DQK ** -0.5)
    kh = heads(k, DQK)
    vh = heads(v, DV)
    h = mlstm_chunkwise(qh, kh, vh, i_pre, logf)
    h = jnp.transpose(h, (0, 2, 1, 3))
    h = h * lax.rsqrt(jnp.mean(h * h, axis=-1, keepdims=True) + EPS)
    h = h.reshape(B, S, V_W).astype(x.dtype) * head_norm
    h = h * jax.nn.sigmoid(o)
    return h @ w_out


def swiglu(x, w_in, w_out):
    gu = x @ w_in
    g, u = gu[..., :D_FF], gu[..., D_FF:]
    return (jax.nn.silu(g) * u) @ w_out


def setup_inputs(seed: int = 0) -> dict:
    key = jax.random.key(seed)
    ks = jax.random.split(key, 16)
    f32 = jnp.float32
    nrm = lambda k, shape, s: jax.random.normal(k, shape, f32) * s
    fgate_bias = jnp.linspace(3.0, 6.0, N_HEADS, dtype=f32)
    gate_bias = jnp.concatenate([
        nrm(ks[6], (N_MLSTM_LAYERS, N_HEADS), 0.1),
        fgate_bias[None, :] + nrm(ks[7], (N_MLSTM_LAYERS, N_HEADS), 0.1)], axis=-1)
    return {
        "x": nrm(ks[0], (BATCH, SEQ, D_MODEL), 1.0),
        "pool_norm": 1.0 + nrm(ks[1], (N_POOL_LAYERS, D_MODEL), 0.02),
        "pool_w": nrm(ks[2], (N_POOL_LAYERS, N_POOL_GROUPS, POOL_GROUP_DIM, POOL_GROUP_DIM), POOL_GROUP_DIM ** -0.5),
        "pool_scale": 0.5 + nrm(ks[3], (N_POOL_LAYERS, D_MODEL), 0.05),
        "mlstm_norm": 1.0 + nrm(ks[4], (N_MLSTM_LAYERS, D_MODEL), 0.02),
        "mlstm_w_in": nrm(ks[5], (N_MLSTM_LAYERS, D_MODEL, MLSTM_IN_W), D_MODEL ** -0.5),
        "mlstm_gate_bias": gate_bias,
        "mlstm_head_norm": 1.0 + nrm(ks[8], (N_MLSTM_LAYERS, V_W), 0.02),
        "mlstm_w_out": nrm(ks[9], (N_MLSTM_LAYERS, V_W, D_MODEL), V_W ** -0.5),
        "ffn_norm": 1.0 + nrm(ks[10], (DEPTH, D_MODEL), 0.02),
        "ffn_w_in": nrm(ks[11], (DEPTH, D_MODEL, 2 * D_FF), D_MODEL ** -0.5),
        "ffn_w_out": nrm(ks[12], (DEPTH, D_FF, D_MODEL), D_FF ** -0.5),
        "final_norm": 1.0 + nrm(ks[13], (D_MODEL,), 0.02),
    }


def reference(x, pool_norm, pool_w, pool_scale, mlstm_norm, mlstm_w_in, mlstm_gate_bias,
              mlstm_head_norm, mlstm_w_out, ffn_norm, ffn_w_in, ffn_w_out, final_norm):
    h = x
    for i in range(DEPTH):
        j = i // N_MIXERS
        if i % N_MIXERS == 0:
            h = h + pool_mixer(rms_norm(h, pool_norm[j]), pool_w[j], pool_scale[j])
        else:
            h = h + mlstm_mixer(rms_norm(h, mlstm_norm[j]), mlstm_w_in[j], mlstm_gate_bias[j],
                                mlstm_head_norm[j], mlstm_w_out[j])
        h = h + swiglu(rms_norm(h, ffn_norm[i]), ffn_w_in[i], ffn_w_out[i])
    return rms_norm(h, final_norm)
```

```python
import functools
import math

import jax
import jax.numpy as jnp
from jax import lax
from jax.experimental import pallas as pl
from jax.experimental.pallas import tpu as pltpu

D_MODEL = 1024
POOL_WINDOWS = (2, 4, 8, 16)
POOL_GROUP_DIM = D_MODEL // len(POOL_WINDOWS)
POOL_HALO = 16

N_HEADS = 4
DV = D_MODEL // N_HEADS
DQK = DV // 2
QK_W = N_HEADS * DQK
V_W = N_HEADS * DV
MAIN_W = 2 * QK_W + 2 * V_W
N_GATES = 2 * N_HEADS

D_FF = int(math.ceil(8 * D_MODEL / 3 / 256) * 256)
FF_CHUNK = 256

EPS = 1e-6

POOL_TILE = 512
FFN_TILE = 512
PROJ_TILE = 512
MLSTM_CHUNK = 256
N_GATE_ROWS = 24

VMEM_LIMIT = 56 * 1024 * 1024

_NT = (((1,), (1,)), ((), ()))
_TN = (((0,), (0,)), ((), ()))


def _params(*semantics):
    return pltpu.CompilerParams(dimension_semantics=semantics,
                                vmem_limit_bytes=VMEM_LIMIT)


def _resident(shape):
    zeros = (0,) * len(shape)
    return pl.BlockSpec(shape, lambda *_: zeros, pipeline_mode=pl.Buffered(1))


def _rms_norm(x, gain):
    return x * lax.rsqrt(jnp.mean(x * x, axis=-1, keepdims=True) + EPS) * gain


def _dot(a, b, dims=None):
    if dims is None:
        return jnp.dot(a, b, preferred_element_type=jnp.float32)
    return lax.dot_general(a, b, dims, preferred_element_type=jnp.float32)


def _pool_kernel(x_ref, gain_ref, w_ref, scale_ref, o_ref, ext_ref):
    s = pl.program_id(1)
    ts = x_ref.shape[1]
    x = x_ref[0]
    xn = _rms_norm(x, gain_ref[...])

    @pl.when(s == 0)
    def _():
        ext_ref[0:POOL_HALO, :] = jnp.zeros((POOL_HALO, D_MODEL), jnp.float32)

    ext_ref[POOL_HALO:POOL_HALO + ts, :] = xn

    pos = (s * ts + 1 + lax.broadcasted_iota(jnp.int32, (ts, 1), 0)).astype(jnp.float32)
    for g, win in enumerate(POOL_WINDOWS):
        cols = slice(g * POOL_GROUP_DIM, (g + 1) * POOL_GROUP_DIM)
        xg = xn[:, cols]
        tot = xg
        for k in range(1, win):
            tot = tot + ext_ref[POOL_HALO - k:POOL_HALO - k + ts, cols]
        inv_cnt = 1.0 / jnp.minimum(pos, float(win))
        pooled = tot * inv_cnt - xg
        y = _dot(pooled.astype(jnp.bfloat16), w_ref[g])
        o_ref[0, :, cols] = x[:, cols] + y * scale_ref[:, cols]

    ext_ref[0:POOL_HALO, :] = ext_ref[ts:ts + POOL_HALO, :]


def _pool_layer(h, gain, w, scale):
    B, S, D = h.shape
    ts = POOL_TILE
    return pl.pallas_call(
        _pool_kernel,
        grid=(B, S // ts),
        in_specs=[
            pl.BlockSpec((1, ts, D), lambda b, s: (b, s, 0)),
            _resident((1, D)),
            _resident(w.shape),
            _resident((1, D)),
        ],
        out_specs=pl.BlockSpec((1, ts, D), lambda b, s: (b, s, 0)),
        out_shape=jax.ShapeDtypeStruct(h.shape, h.dtype),
        scratch_shapes=[pltpu.VMEM((POOL_HALO + ts, D), jnp.float32)],
        compiler_params=_params("arbitrary", "arbitrary"),
        name="pool_mixer",
    )(h, gain.reshape(1, D), w.astype(jnp.bfloat16), scale.reshape(1, D))


def _ffn_kernel(x_ref, gain_ref, w_in_ref, w_out_ref, fgain_ref, o_ref, act_ref,
                *, final_norm):
    x = x_ref[...]
    xn = _rms_norm(x, gain_ref[...]).astype(jnp.bfloat16)
    for j in range(D_FF // FF_CHUNK):
        lo = j * FF_CHUNK
        gate = _dot(xn, w_in_ref[:, lo:lo + FF_CHUNK])
        up = _dot(xn, w_in_ref[:, D_FF + lo:D_FF + lo + FF_CHUNK])
        act_ref[:, lo:lo + FF_CHUNK] = (gate * jax.nn.sigmoid(gate) * up).astype(jnp.bfloat16)
    out = x + _dot(act_ref[...], w_out_ref[...])
    if final_norm:
        out = _rms_norm(out, fgain_ref[...])
    o_ref[...] = out


def _ffn_layer(h2, gain, w_in, w_out, final_gain, final_norm):
    T, D = h2.shape
    tm = FFN_TILE
    return pl.pallas_call(
        functools.partial(_ffn_kernel, final_norm=final_norm),
        grid=(T // tm,),
        in_specs=[
            pl.BlockSpec((tm, D), lambda i: (i, 0)),
            _resident((1, D)),
            _resident(w_in.shape),
            _resident(w_out.shape),
            _resident((1, D)),
        ],
        out_specs=pl.BlockSpec((tm, D), lambda i: (i, 0)),
        out_shape=jax.ShapeDtypeStruct(h2.shape, h2.dtype),
        scratch_shapes=[pltpu.VMEM((tm, D_FF), jnp.bfloat16)],
        compiler_params=_params("arbitrary"),
        name="swiglu_final" if final_norm else "swiglu",
    )(h2, gain.reshape(1, D), w_in.astype(jnp.bfloat16), w_out.astype(jnp.bfloat16),
      final_gain.reshape(1, D))


def _proj_kernel(x_ref, gain_ref, w_ref, wg_ref, q_ref, k_ref, v_ref, o_ref, gt_ref):
    xn = _rms_norm(x_ref[...], gain_ref[...]).astype(jnp.bfloat16)
    q_ref[...] = (_dot(xn, w_ref[:, 0:QK_W]) * (DQK ** -0.5)).astype(q_ref.dtype)
    k_ref[...] = _dot(xn, w_ref[:, QK_W:2 * QK_W]).astype(k_ref.dtype)
    v_ref[...] = _dot(xn, w_ref[:, 2 * QK_W:2 * QK_W + V_W]).astype(v_ref.dtype)
    o_ref[...] = _dot(xn, w_ref[:, 2 * QK_W + V_W:MAIN_W])
    gt_ref[...] = _dot(wg_ref[...], xn, _NT)


def _proj_layer(h2, gain, w_in):
    T, D = h2.shape
    tm = PROJ_TILE
    w_main = w_in[:, :MAIN_W].astype(jnp.bfloat16)
    w_gate_t = w_in[:, MAIN_W:].T.astype(jnp.bfloat16)
    tok = lambda i: (i, 0)
    return pl.pallas_call(
        _proj_kernel,
        grid=(T // tm,),
        in_specs=[
            pl.BlockSpec((tm, D), tok),
            _resident((1, D)),
            _resident(w_main.shape),
            _resident(w_gate_t.shape),
        ],
        out_specs=[
            pl.BlockSpec((tm, QK_W), tok),
            pl.BlockSpec((tm, QK_W), tok),
            pl.BlockSpec((tm, V_W), tok),
            pl.BlockSpec((tm, V_W), tok),
            pl.BlockSpec((N_GATES, tm), lambda i: (0, i)),
        ],
        out_shape=[
            jax.ShapeDtypeStruct((T, QK_W), jnp.bfloat16),
            jax.ShapeDtypeStruct((T, QK_W), jnp.bfloat16),
            jax.ShapeDtypeStruct((T, V_W), jnp.bfloat16),
            jax.ShapeDtypeStruct((T, V_W), jnp.float32),
            jax.ShapeDtypeStruct((N_GATES, T), jnp.float32),
        ],
        compiler_params=_params("arbitrary"),
        name="mlstm_proj",
    )(h2, gain.reshape(1, D), w_main, w_gate_t)


def _segmented_scan(x, op, lane, seg):
    shift = 1
    while shift < seg:
        moved = pltpu.roll(x, shift, 1)
        x = jnp.where((lane % seg) >= shift, op(x, moved), x)
        shift *= 2
    return x


def _gate_kernel(gt_ref, bias_ref, r_ref):
    L = MLSTM_CHUNK
    S = gt_ref.shape[1]
    g = gt_ref[...] + bias_ref[...]
    i_pre = g[0:N_HEADS]
    f_pre = g[N_HEADS:N_GATES]
    logf = jnp.minimum(f_pre, 0.0) - jnp.log1p(jnp.exp(-jnp.abs(f_pre)))
    lane = lax.broadcasted_iota(jnp.int32, (N_HEADS, S), 1)
    b = _segmented_scan(logf, jnp.add, lane, L)
    a = i_pre - b
    amax = _segmented_scan(a, jnp.maximum, lane, L)

    r_ref[0, 0:N_HEADS, :] = a
    r_ref[0, 5 * N_HEADS:N_GATE_ROWS, :] = jnp.zeros((N_GATE_ROWS - 5 * N_HEADS, S), jnp.float32)
    m_prev = jnp.zeros((N_HEADS, 1), jnp.float32)
    for c in range(S // L):
        seg = slice(c * L, (c + 1) * L)
        big_m = jnp.maximum(m_prev, amax[:, seg])
        m_last = big_m[:, L - 1:L]
        r_ref[0, N_HEADS:2 * N_HEADS, seg] = big_m
        r_ref[0, 2 * N_HEADS:3 * N_HEADS, seg] = -(b[:, seg] + big_m)
        r_ref[0, 3 * N_HEADS:4 * N_HEADS, seg] = m_prev - big_m
        r_ref[0, 4 * N_HEADS:5 * N_HEADS, seg] = a[:, seg] - m_last
        m_prev = b[:, (c + 1) * L - 1:(c + 1) * L] + m_last


def _gate_layer(gates_t, bias, B, S):
    return pl.pallas_call(
        _gate_kernel,
        grid=(B,),
        in_specs=[
            pl.BlockSpec((N_GATES, S), lambda b: (0, b)),
            _resident((N_GATES, 1)),
        ],
        out_specs=pl.BlockSpec((1, N_GATE_ROWS, S), lambda b: (b, 0, 0)),
        out_shape=jax.ShapeDtypeStruct((B, N_GATE_ROWS, S), jnp.float32),
        compiler_params=_params("arbitrary"),
        name="mlstm_gates",
    )(gates_t, bias.reshape(N_GATES, 1).astype(jnp.float32))


def _chunk_kernel(x_ref, q_ref, k_ref, v_ref, o_ref, r_ref, hn_ref, w_out_ref,
                  out_ref, c_ref, n_ref, hg_ref):
    L = MLSTM_CHUNK

    @pl.when(pl.program_id(1) == 0)
    def _():
        c_ref[...] = jnp.zeros(c_ref.shape, jnp.float32)
        n_ref[...] = jnp.zeros(n_ref.shape, jnp.float32)

    rows = r_ref[0]
    cols = rows.T
    t_idx = lax.broadcasted_iota(jnp.int32, (L, L), 0)
    s_idx = lax.broadcasted_iota(jnp.int32, (L, L), 1)
    causal = s_idx <= t_idx

    for h in range(N_HEADS):
        qk = slice(h * DQK, (h + 1) * DQK)
        vv = slice(h * DV, (h + 1) * DV)
        q = q_ref[:, qk]
        k = k_ref[:, qk]
        v = v_ref[:, vv]
        a_row = rows[h:h + 1, :]
        big_m = cols[:, N_HEADS + h:N_HEADS + h + 1]
        neg_m = cols[:, 2 * N_HEADS + h:2 * N_HEADS + h + 1]
        log_inter = cols[:, 3 * N_HEADS + h:3 * N_HEADS + h + 1]
        log_w = cols[:, 4 * N_HEADS + h:4 * N_HEADS + h + 1]

        d = jnp.exp(jnp.where(causal, a_row - big_m, -jnp.inf))
        inter = jnp.exp(log_inter)
        s = _dot(q, k, _NT) * d
        c_state = c_ref[h]
        n_state = n_ref[h]
        num = inter * _dot(q, c_state.astype(jnp.bfloat16)) + _dot(s.astype(jnp.bfloat16), v)
        qn = jnp.sum(q.astype(jnp.float32) * n_state, axis=1, keepdims=True)
        den = inter * qn + jnp.sum(s, axis=1, keepdims=True)
        hh = num / jnp.maximum(jnp.abs(den), jnp.exp(neg_m))

        kw = k.astype(jnp.float32) * jnp.exp(log_w)
        decay = inter[L - 1:L, :]
        c_ref[h] = decay * c_state + _dot(kw.astype(jnp.bfloat16), v, _TN)
        n_ref[h] = decay * n_state + jnp.sum(kw, axis=0, keepdims=True)

        hnorm = hh * lax.rsqrt(jnp.mean(hh * hh, axis=-1, keepdims=True) + EPS)
        gated = hnorm * hn_ref[:, vv] * jax.nn.sigmoid(o_ref[:, vv])
        hg_ref[:, vv] = gated.astype(jnp.bfloat16)

    out_ref[...] = x_ref[...] + _dot(hg_ref[...], w_out_ref[...])


def _chunk_layer(h2, q, k, v, o, rows, head_norm, w_out, B, S):
    T, D = h2.shape
    L = MLSTM_CHUNK
    nc = S // L
    tok = lambda b, c: (b * nc + c, 0)
    return pl.pallas_call(
        _chunk_kernel,
        grid=(B, nc),
        in_specs=[
            pl.BlockSpec((L, D), tok),
            pl.BlockSpec((L, QK_W), tok),
            pl.BlockSpec((L, QK_W), tok),
            pl.BlockSpec((L, V_W), tok),
            pl.BlockSpec((L, V_W), tok),
            pl.BlockSpec((1, N_GATE_ROWS, L), lambda b, c: (b, 0, c)),
            _resident((1, V_W)),
            _resident(w_out.shape),
        ],
        out_specs=pl.BlockSpec((L, D), tok),
        out_shape=jax.ShapeDtypeStruct(h2.shape, h2.dtype),
        scratch_shapes=[
            pltpu.VMEM((N_HEADS, DQK, DV), jnp.float32),
            pltpu.VMEM((N_HEADS, 1, DQK), jnp.float32),
            pltpu.VMEM((L, V_W), jnp.bfloat16),
        ],
        compiler_params=_params("arbitrary", "arbitrary"),
        name="mlstm_chunk",
    )(h2, q, k, v, o, rows, head_norm.reshape(1, V_W), w_out.astype(jnp.bfloat16))


def _mlstm_layer(h, gain, w_in, gate_bias, head_norm, w_out):
    B, S, D = h.shape
    h2 = h.reshape(B * S, D)
    q, k, v, o, gates_t = _proj_layer(h2, gain, w_in)
    rows = _gate_layer(gates_t, gate_bias, B, S)
    out = _chunk_layer(h2, q, k, v, o, rows, head_norm, w_out, B, S)
    return out.reshape(B, S, D)


def kernel(x, pool_norm, pool_w, pool_scale, mlstm_norm, mlstm_w_in, mlstm_gate_bias,
           mlstm_head_norm, mlstm_w_out, ffn_norm, ffn_w_in, ffn_w_out, final_norm):
    B, S, D = x.shape
    depth = ffn_norm.shape[0]
    h = x
    for i in range(depth):
        j = i // 2
        if i % 2 == 0:
            h = _pool_layer(h, pool_norm[j], pool_w[j], pool_scale[j])
        else:
            h = _mlstm_layer(h, mlstm_norm[j], mlstm_w_in[j], mlstm_gate_bias[j],
                             mlstm_head_norm[j], mlstm_w_out[j])
        h = _ffn_layer(h.reshape(B * S, D), ffn_norm[i], ffn_w_in[i], ffn_w_out[i],
                       final_norm, i == depth - 1).reshape(B, S, D)
    return h
```

```python
import functools
import math

import jax
import jax.numpy as jnp
from jax import lax
from jax.experimental import pallas as pl
from jax.experimental.pallas import tpu as pltpu

D_MODEL = 1024
POOL_WINDOWS = (2, 4, 8, 16)
POOL_GROUP_DIM = D_MODEL // len(POOL_WINDOWS)
POOL_HALO = 16

N_HEADS = 4
DV = D_MODEL // N_HEADS
DQK = DV // 2
QK_W = N_HEADS * DQK
V_W = N_HEADS * DV
MAIN_W = 2 * QK_W + 2 * V_W
N_GATES = 2 * N_HEADS

D_FF = int(math.ceil(8 * D_MODEL / 3 / 256) * 256)
FF_CHUNK = 256

EPS = 1e-6

POOL_TILE = 512
FFN_TILE = 512
PROJ_TILE = 512
MLSTM_CHUNK = 256
CHUNKS_PER_STEP = 2
N_GATE_ROWS = 24

VMEM_LIMIT = 56 * 1024 * 1024

_NT = (((1,), (1,)), ((), ()))
_TN = (((0,), (0,)), ((), ()))


def _params(*semantics):
    return pltpu.CompilerParams(dimension_semantics=semantics,
                                vmem_limit_bytes=VMEM_LIMIT)


def _resident(shape, layer=None):
    zeros = (0,) * len(shape)
    if layer is None:
        return pl.BlockSpec(shape, lambda *_: zeros, pipeline_mode=pl.Buffered(1))
    return pl.BlockSpec((None,) + shape, lambda *_: (layer,) + zeros,
                        pipeline_mode=pl.Buffered(1))


def _rms_norm(x, gain):
    return x * lax.rsqrt(jnp.mean(x * x, axis=-1, keepdims=True) + EPS) * gain


def _dot(a, b, dims=None):
    if dims is None:
        return jnp.dot(a, b, preferred_element_type=jnp.float32)
    return lax.dot_general(a, b, dims, preferred_element_type=jnp.float32)


def _pool_kernel(x_ref, gain_ref, w_ref, scale_ref, o_ref, ext_ref):
    s = pl.program_id(1)
    ts = x_ref.shape[1]
    x = x_ref[0]
    xn = _rms_norm(x, gain_ref[...])

    @pl.when(s == 0)
    def _():
        ext_ref[0:POOL_HALO, :] = jnp.zeros((POOL_HALO, D_MODEL), jnp.float32)

    ext_ref[POOL_HALO:POOL_HALO + ts, :] = xn

    pos = (s * ts + 1 + lax.broadcasted_iota(jnp.int32, (ts, 1), 0)).astype(jnp.float32)
    for g, win in enumerate(POOL_WINDOWS):
        cols = slice(g * POOL_GROUP_DIM, (g + 1) * POOL_GROUP_DIM)
        xg = xn[:, cols]
        tot = xg
        for k in range(1, win):
            tot = tot + ext_ref[POOL_HALO - k:POOL_HALO - k + ts, cols]
        inv_cnt = 1.0 / jnp.minimum(pos, float(win))
        pooled = tot * inv_cnt - xg
        y = _dot(pooled.astype(jnp.bfloat16), w_ref[g])
        o_ref[0, :, cols] = x[:, cols] + y * scale_ref[:, cols]

    ext_ref[0:POOL_HALO, :] = ext_ref[ts:ts + POOL_HALO, :]


def _pool_layer(h, gain, w, scale, layer):
    B, S, D = h.shape
    ts = POOL_TILE
    return pl.pallas_call(
        _pool_kernel,
        grid=(B, S // ts),
        in_specs=[
            pl.BlockSpec((1, ts, D), lambda b, s: (b, s, 0)),
            _resident((1, D), layer),
            _resident(w.shape[1:], layer),
            _resident((1, D), layer),
        ],
        out_specs=pl.BlockSpec((1, ts, D), lambda b, s: (b, s, 0)),
        out_shape=jax.ShapeDtypeStruct(h.shape, h.dtype),
        scratch_shapes=[pltpu.VMEM((POOL_HALO + ts, D), jnp.float32)],
        compiler_params=_params("arbitrary", "arbitrary"),
        name="pool_mixer",
    )(h, gain, w, scale)


def _ffn_kernel(*refs, mixer_proj, final_norm):
    if mixer_proj:
        x_ref, heads_ref, w_mix_ref, gain_ref, w_in_ref, w_out_ref, fgain_ref, o_ref, act_ref = refs
        x = x_ref[...] + _dot(heads_ref[...], w_mix_ref[...])
    else:
        x_ref, gain_ref, w_in_ref, w_out_ref, fgain_ref, o_ref, act_ref = refs
        x = x_ref[...]
    xn = _rms_norm(x, gain_ref[...]).astype(jnp.bfloat16)
    for j in range(D_FF // FF_CHUNK):
        lo = j * FF_CHUNK
        gate = _dot(xn, w_in_ref[:, lo:lo + FF_CHUNK])
        up = _dot(xn, w_in_ref[:, D_FF + lo:D_FF + lo + FF_CHUNK])
        act_ref[:, lo:lo + FF_CHUNK] = (gate * jax.nn.sigmoid(gate) * up).astype(jnp.bfloat16)
    out = x + _dot(act_ref[...], w_out_ref[...])
    if final_norm:
        out = _rms_norm(out, fgain_ref[...])
    o_ref[...] = out


def _ffn_layer(h2, gain, w_in, w_out, final_gain, layer, final_norm, heads=None, w_mix=None,
               mix_layer=None):
    T, D = h2.shape
    tm = FFN_TILE
    tok = pl.BlockSpec((tm, D), lambda i: (i, 0))
    mixer_proj = heads is not None
    mix_specs = [tok, _resident(w_mix.shape[1:], mix_layer)] if mixer_proj else []
    mix_args = (heads, w_mix) if mixer_proj else ()
    return pl.pallas_call(
        functools.partial(_ffn_kernel, mixer_proj=mixer_proj, final_norm=final_norm),
        grid=(T // tm,),
        in_specs=[tok] + mix_specs + [
            _resident((1, D), layer),
            _resident(w_in.shape[1:], layer),
            _resident(w_out.shape[1:], layer),
            _resident((1, D)),
        ],
        out_specs=tok,
        out_shape=jax.ShapeDtypeStruct(h2.shape, h2.dtype),
        scratch_shapes=[pltpu.VMEM((tm, D_FF), jnp.bfloat16)],
        compiler_params=_params("arbitrary"),
        name="swiglu" + ("_mix" if mixer_proj else "") + ("_final" if final_norm else ""),
    )(h2, *mix_args, gain, w_in, w_out, final_gain)


def _proj_kernel(x_ref, gain_ref, w_ref, wg_ref, q_ref, k_ref, v_ref, og_ref, gt_ref):
    xn = _rms_norm(x_ref[...], gain_ref[...]).astype(jnp.bfloat16)
    q_ref[...] = (_dot(xn, w_ref[:, 0:QK_W]) * (DQK ** -0.5)).astype(q_ref.dtype)
    k_ref[...] = _dot(xn, w_ref[:, QK_W:2 * QK_W]).astype(k_ref.dtype)
    v_ref[...] = _dot(xn, w_ref[:, 2 * QK_W:2 * QK_W + V_W]).astype(v_ref.dtype)
    og_ref[...] = jax.nn.sigmoid(_dot(xn, w_ref[:, 2 * QK_W + V_W:MAIN_W])).astype(og_ref.dtype)
    gt_ref[...] = _dot(wg_ref[...], xn, _NT)


def _proj_layer(h2, gain, w_main, w_gate_t, layer):
    T, D = h2.shape
    tm = PROJ_TILE
    tok = lambda i: (i, 0)
    return pl.pallas_call(
        _proj_kernel,
        grid=(T // tm,),
        in_specs=[
            pl.BlockSpec((tm, D), tok),
            _resident((1, D), layer),
            _resident(w_main.shape[1:], layer),
            _resident(w_gate_t.shape[1:], layer),
        ],
        out_specs=[
            pl.BlockSpec((tm, QK_W), tok),
            pl.BlockSpec((tm, QK_W), tok),
            pl.BlockSpec((tm, V_W), tok),
            pl.BlockSpec((tm, V_W), tok),
            pl.BlockSpec((N_GATES, tm), lambda i: (0, i)),
        ],
        out_shape=[
            jax.ShapeDtypeStruct((T, QK_W), jnp.bfloat16),
            jax.ShapeDtypeStruct((T, QK_W), jnp.bfloat16),
            jax.ShapeDtypeStruct((T, V_W), jnp.bfloat16),
            jax.ShapeDtypeStruct((T, V_W), jnp.bfloat16),
            jax.ShapeDtypeStruct((N_GATES, T), jnp.float32),
        ],
        compiler_params=_params("arbitrary"),
        name="mlstm_proj",
    )(h2, gain, w_main, w_gate_t)


def _segmented_scan(x, op, lane, seg):
    shift = 1
    while shift < seg:
        moved = pltpu.roll(x, shift, 1)
        x = jnp.where((lane % seg) >= shift, op(x, moved), x)
        shift *= 2
    return x


def _gate_kernel(gt_ref, bias_ref, r_ref):
    L = MLSTM_CHUNK
    S = gt_ref.shape[1]
    g = gt_ref[...] + bias_ref[...]
    i_pre = g[0:N_HEADS]
    f_pre = g[N_HEADS:N_GATES]
    logf = jnp.minimum(f_pre, 0.0) - jnp.log1p(jnp.exp(-jnp.abs(f_pre)))
    lane = lax.broadcasted_iota(jnp.int32, (N_HEADS, S), 1)
    b = _segmented_scan(logf, jnp.add, lane, L)
    a = i_pre - b
    amax = _segmented_scan(a, jnp.maximum, lane, L)

    r_ref[0, 0:N_HEADS, :] = a
    r_ref[0, 5 * N_HEADS:N_GATE_ROWS, :] = jnp.zeros((N_GATE_ROWS - 5 * N_HEADS, S), jnp.float32)
    m_prev = jnp.zeros((N_HEADS, 1), jnp.float32)
    for c in range(S // L):
        seg = slice(c * L, (c + 1) * L)
        big_m = jnp.maximum(m_prev, amax[:, seg])
        m_last = big_m[:, L - 1:L]
        r_ref[0, N_HEADS:2 * N_HEADS, seg] = big_m
        r_ref[0, 2 * N_HEADS:3 * N_HEADS, seg] = -(b[:, seg] + big_m)
        r_ref[0, 3 * N_HEADS:4 * N_HEADS, seg] = m_prev - big_m
        r_ref[0, 4 * N_HEADS:5 * N_HEADS, seg] = a[:, seg] - m_last
        m_prev = b[:, (c + 1) * L - 1:(c + 1) * L] + m_last


def _gate_layer(gates_t, bias, B, S, layer):
    return pl.pallas_call(
        _gate_kernel,
        grid=(B,),
        in_specs=[
            pl.BlockSpec((N_GATES, S), lambda b: (0, b)),
            _resident((N_GATES, 1), layer),
        ],
        out_specs=pl.BlockSpec((1, N_GATE_ROWS, S), lambda b: (b, 0, 0)),
        out_shape=jax.ShapeDtypeStruct((B, N_GATE_ROWS, S), jnp.float32),
        compiler_params=_params("arbitrary"),
        name="mlstm_gates",
    )(gates_t, bias)


def _chunk_kernel(q_ref, k_ref, v_ref, og_ref, r_ref, hn_ref, out_ref, c_ref, n_ref):
    L = MLSTM_CHUNK

    @pl.when(pl.program_id(1) == 0)
    def _():
        c_ref[...] = jnp.zeros(c_ref.shape, jnp.float32)
        n_ref[...] = jnp.zeros(n_ref.shape, jnp.float32)

    t_idx = lax.broadcasted_iota(jnp.int32, (L, L), 0)
    s_idx = lax.broadcasted_iota(jnp.int32, (L, L), 1)
    causal = s_idx <= t_idx
    rows = [r_ref[0, :, c * L:(c + 1) * L] for c in range(CHUNKS_PER_STEP)]
    cols = [r.T for r in rows]

    for h in range(N_HEADS):
        qk = slice(h * DQK, (h + 1) * DQK)
        vv = slice(h * DV, (h + 1) * DV)
        c_state = c_ref[h]
        n_state = n_ref[h]
        for c in range(CHUNKS_PER_STEP):
            tok = slice(c * L, (c + 1) * L)
            q = q_ref[tok, qk]
            k = k_ref[tok, qk]
            v = v_ref[tok, vv]
            a_row = rows[c][h:h + 1, :]
            big_m = cols[c][:, N_HEADS + h:N_HEADS + h + 1]
            neg_m = cols[c][:, 2 * N_HEADS + h:2 * N_HEADS + h + 1]
            log_inter = cols[c][:, 3 * N_HEADS + h:3 * N_HEADS + h + 1]
            log_w = cols[c][:, 4 * N_HEADS + h:4 * N_HEADS + h + 1]

            d = jnp.exp(jnp.where(causal, a_row - big_m, -jnp.inf))
            inter = jnp.exp(log_inter)
            s = _dot(q, k, _NT) * d
            num = inter * _dot(q, c_state.astype(jnp.bfloat16)) + _dot(s.astype(jnp.bfloat16), v)
            qn = jnp.sum(q.astype(jnp.float32) * n_state, axis=1, keepdims=True)
            den = inter * qn + jnp.sum(s, axis=1, keepdims=True)
            hh = num * (1.0 / jnp.maximum(jnp.abs(den), jnp.exp(neg_m)))

            kw = k.astype(jnp.float32) * jnp.exp(log_w)
            decay = inter[L - 1:L, :]
            c_state = decay * c_state + _dot(kw.astype(jnp.bfloat16), v, _TN)
            n_state = decay * n_state + jnp.sum(kw, axis=0, keepdims=True)

            hnorm = hh * lax.rsqrt(jnp.mean(hh * hh, axis=-1, keepdims=True) + EPS)
            gated = hnorm * hn_ref[:, vv] * og_ref[tok, vv].astype(jnp.float32)
            out_ref[tok, vv] = gated.astype(out_ref.dtype)
        c_ref[h] = c_state
        n_ref[h] = n_state


def _chunk_layer(q, k, v, og, rows, head_norm, B, S, layer):
    T = q.shape[0]
    step = MLSTM_CHUNK * CHUNKS_PER_STEP
    ns = S // step
    tok = lambda b, c: (b * ns + c, 0)
    return pl.pallas_call(
        _chunk_kernel,
        grid=(B, ns),
        in_specs=[
            pl.BlockSpec((step, QK_W), tok),
            pl.BlockSpec((step, QK_W), tok),
            pl.BlockSpec((step, V_W), tok),
            pl.BlockSpec((step, V_W), tok),
            pl.BlockSpec((1, N_GATE_ROWS, step), lambda b, c: (b, 0, c)),
            _resident((1, V_W), layer),
        ],
        out_specs=pl.BlockSpec((step, V_W), tok),
        out_shape=jax.ShapeDtypeStruct((T, V_W), jnp.bfloat16),
        scratch_shapes=[
            pltpu.VMEM((N_HEADS, DQK, DV), jnp.float32),
            pltpu.VMEM((N_HEADS, 1, DQK), jnp.float32),
        ],
        compiler_params=_params("arbitrary", "arbitrary"),
        name="mlstm_chunk",
    )(q, k, v, og, rows, head_norm)


def kernel(x, pool_norm, pool_w, pool_scale, mlstm_norm, mlstm_w_in, mlstm_gate_bias,
           mlstm_head_norm, mlstm_w_out, ffn_norm, ffn_w_in, ffn_w_out, final_norm):
    B, S, D = x.shape
    depth = ffn_norm.shape[0]
    bf16 = jnp.bfloat16
    row = lambda a: a.reshape(a.shape[0], 1, a.shape[1])
    pool_w_b = pool_w.astype(bf16)
    w_main_b = mlstm_w_in[:, :, :MAIN_W].astype(bf16)
    w_gate_t_b = jnp.swapaxes(mlstm_w_in[:, :, MAIN_W:], 1, 2).astype(bf16)
    gate_bias = mlstm_gate_bias.reshape(-1, N_GATES, 1)
    w_mix_b = mlstm_w_out.astype(bf16)
    ffn_w_in_b = ffn_w_in.astype(bf16)
    ffn_w_out_b = ffn_w_out.astype(bf16)
    fgain = final_norm.reshape(1, D)

    h2 = x.reshape(B * S, D)
    for i in range(depth):
        j = i // 2
        last = i == depth - 1
        if i % 2 == 0:
            h2 = _pool_layer(h2.reshape(B, S, D), row(pool_norm), pool_w_b, row(pool_scale), j)
            h2 = _ffn_layer(h2.reshape(B * S, D), row(ffn_norm), ffn_w_in_b, ffn_w_out_b, fgain, i, last)
        else:
            q, k, v, og, gates_t = _proj_layer(h2, row(mlstm_norm), w_main_b, w_gate_t_b, j)
            rows = _gate_layer(gates_t, gate_bias, B, S, j)
            heads = _chunk_layer(q, k, v, og, rows, row(mlstm_head_norm), B, S, j)
            h2 = _ffn_layer(h2, row(ffn_norm), ffn_w_in_b, ffn_w_out_b, fgain, i, last,
                            heads=heads, w_mix=w_mix_b, mix_layer=j)
    return h2.reshape(B, S, D)
```

```python
import functools
import math

import jax
import jax.numpy as jnp
from jax import lax
from jax.experimental import pallas as pl
from jax.experimental.pallas import tpu as pltpu

D_MODEL = 1024
POOL_WINDOWS = (2, 4, 8, 16)
POOL_GROUP_DIM = D_MODEL // len(POOL_WINDOWS)
POOL_HALO = 32

N_HEADS = 4
DV = D_MODEL // N_HEADS
DQK = DV // 2
QK_W = N_HEADS * DQK
V_W = N_HEADS * DV
MAIN_W = 2 * QK_W + 2 * V_W
N_GATES = 2 * N_HEADS

D_FF = int(math.ceil(8 * D_MODEL / 3 / 256) * 256)
FF_CHUNK = 256

EPS = 1e-6

POOL_TILE = 512
FFN_TILE = 1024
PROJ_TILE = 1024
MLSTM_CHUNK = 256
CHUNKS_PER_STEP = 2
N_GATE_ROWS = 24

VMEM_LIMIT = 56 * 1024 * 1024

_NT = (((1,), (1,)), ((), ()))


def _params(*semantics):
    return pltpu.CompilerParams(dimension_semantics=semantics,
                                vmem_limit_bytes=VMEM_LIMIT)


def _resident(shape, layer=None):
    zeros = (0,) * len(shape)
    if layer is None:
        return pl.BlockSpec(shape, lambda *_: zeros, pipeline_mode=pl.Buffered(1))
    return pl.BlockSpec((None,) + shape, lambda *_: (layer,) + zeros,
                        pipeline_mode=pl.Buffered(1))


def _rms_norm(x, gain):
    return x * lax.rsqrt(jnp.mean(x * x, axis=-1, keepdims=True) + EPS) * gain


def _dot(a, b, dims=None):
    if dims is None:
        return jnp.dot(a, b, preferred_element_type=jnp.float32)
    return lax.dot_general(a, b, dims, preferred_element_type=jnp.float32)


def _pool_kernel(x_ref, gain_ref, w_ref, scale_ref, o_ref, ext_ref, *stage_refs):
    s = pl.program_id(1)
    ts = x_ref.shape[1]
    rows = POOL_HALO + ts
    gd = POOL_GROUP_DIM
    x = x_ref[0]
    xn = _rms_norm(x, gain_ref[...])

    @pl.when(s == 0)
    def _():
        ext_ref[0:POOL_HALO, :] = jnp.zeros((POOL_HALO, D_MODEL), jnp.float32)

    ext_ref[POOL_HALO:rows, :] = xn

    prev = ext_ref
    for j, cur in enumerate(stage_refs, start=1):
        shift = 2 ** (j - 1)
        lo = 8 * j
        skip = 0 if j == 1 else gd
        cur[lo:rows, :] = prev[lo:rows, skip:] + prev[lo - shift:rows - shift, skip:]
        prev = cur

    pos = (s * ts + 1 + lax.broadcasted_iota(jnp.int32, (ts, 1), 0)).astype(jnp.float32)
    for g, win in enumerate(POOL_WINDOWS):
        cols = slice(g * gd, (g + 1) * gd)
        tot = stage_refs[g][POOL_HALO:rows, 0:gd]
        inv_cnt = 1.0 / jnp.minimum(pos, float(win))
        pooled = tot * inv_cnt - xn[:, cols]
        y = _dot(pooled.astype(jnp.bfloat16), w_ref[g])
        o_ref[0, :, cols] = x[:, cols] + y * scale_ref[:, cols]

    ext_ref[0:POOL_HALO, :] = ext_ref[ts:rows, :]


def _pool_layer(h, gain, w, scale, layer):
    B, S, D = h.shape
    ts = POOL_TILE
    return pl.pallas_call(
        _pool_kernel,
        grid=(B, S // ts),
        in_specs=[
            pl.BlockSpec((1, ts, D), lambda b, s: (b, s, 0)),
            _resident((1, D), layer),
            _resident(w.shape[1:], layer),
            _resident((1, D), layer),
        ],
        out_specs=pl.BlockSpec((1, ts, D), lambda b, s: (b, s, 0)),
        out_shape=jax.ShapeDtypeStruct(h.shape, h.dtype),
        scratch_shapes=[pltpu.VMEM((POOL_HALO + ts, D), jnp.float32)] + [
            pltpu.VMEM((POOL_HALO + ts, D - g * POOL_GROUP_DIM), jnp.float32)
            for g in range(len(POOL_WINDOWS))],
        compiler_params=_params("arbitrary", "arbitrary"),
        name="pool_mixer",
    )(h, gain, w, scale)


def _ffn_kernel(*refs, mixer_proj, final_norm):
    if mixer_proj:
        x_ref, heads_ref, w_mix_ref, gain_ref, w_in_ref, w_out_ref, fgain_ref, o_ref, act_ref = refs
        x = x_ref[...] + _dot(heads_ref[...], w_mix_ref[...])
    else:
        x_ref, gain_ref, w_in_ref, w_out_ref, fgain_ref, o_ref, act_ref = refs
        x = x_ref[...]
    xn = _rms_norm(x, gain_ref[...]).astype(jnp.bfloat16)
    for j in range(D_FF // FF_CHUNK):
        lo = j * FF_CHUNK
        gate = _dot(xn, w_in_ref[:, lo:lo + FF_CHUNK])
        up = _dot(xn, w_in_ref[:, D_FF + lo:D_FF + lo + FF_CHUNK])
        act_ref[:, lo:lo + FF_CHUNK] = (gate * jax.nn.sigmoid(gate) * up).astype(jnp.bfloat16)
    out = x + _dot(act_ref[...], w_out_ref[...])
    if final_norm:
        out = _rms_norm(out, fgain_ref[...])
    o_ref[...] = out


def _ffn_layer(h2, gain, w_in, w_out, final_gain, layer, final_norm, heads=None, w_mix=None,
               mix_layer=None):
    T, D = h2.shape
    tm = FFN_TILE
    tok = pl.BlockSpec((tm, D), lambda i: (i, 0))
    mixer_proj = heads is not None
    mix_specs = [tok, _resident(w_mix.shape[1:], mix_layer)] if mixer_proj else []
    mix_args = (heads, w_mix) if mixer_proj else ()
    return pl.pallas_call(
        functools.partial(_ffn_kernel, mixer_proj=mixer_proj, final_norm=final_norm),
        grid=(T // tm,),
        in_specs=[tok] + mix_specs + [
            _resident((1, D), layer),
            _resident(w_in.shape[1:], layer),
            _resident(w_out.shape[1:], layer),
            _resident((1, D)),
        ],
        out_specs=tok,
        out_shape=jax.ShapeDtypeStruct(h2.shape, h2.dtype),
        scratch_shapes=[pltpu.VMEM((tm, D_FF), jnp.bfloat16)],
        compiler_params=_params("arbitrary"),
        name="swiglu" + ("_mix" if mixer_proj else "") + ("_final" if final_norm else ""),
    )(h2, *mix_args, gain, w_in, w_out, final_gain)


def _proj_kernel(x_ref, gain_ref, wk_ref, wt_ref, wg_ref, qt_ref, k_ref, vt_ref, ogt_ref, gt_ref):
    xn = _rms_norm(x_ref[...], gain_ref[...]).astype(jnp.bfloat16)
    k_ref[...] = _dot(xn, wk_ref[...]).astype(k_ref.dtype)
    qt_ref[...] = (_dot(wt_ref[0:QK_W, :], xn, _NT) * (DQK ** -0.5)).astype(qt_ref.dtype)
    vt_ref[...] = _dot(wt_ref[QK_W:QK_W + V_W, :], xn, _NT).astype(vt_ref.dtype)
    ogt_ref[...] = jax.nn.sigmoid(_dot(wt_ref[QK_W + V_W:, :], xn, _NT)).astype(ogt_ref.dtype)
    gt_ref[...] = _dot(wg_ref[...], xn, _NT)


def _proj_layer(h2, gain, w_k, w_t, w_gate_t, layer):
    T, D = h2.shape
    tm = PROJ_TILE
    tok = lambda i: (i, 0)
    lanes = lambda i: (0, i)
    return pl.pallas_call(
        _proj_kernel,
        grid=(T // tm,),
        in_specs=[
            pl.BlockSpec((tm, D), tok),
            _resident((1, D), layer),
            _resident(w_k.shape[1:], layer),
            _resident(w_t.shape[1:], layer),
            _resident(w_gate_t.shape[1:], layer),
        ],
        out_specs=[
            pl.BlockSpec((QK_W, tm), lanes),
            pl.BlockSpec((tm, QK_W), tok),
            pl.BlockSpec((V_W, tm), lanes),
            pl.BlockSpec((V_W, tm), lanes),
            pl.BlockSpec((N_GATES, tm), lanes),
        ],
        out_shape=[
            jax.ShapeDtypeStruct((QK_W, T), jnp.bfloat16),
            jax.ShapeDtypeStruct((T, QK_W), jnp.bfloat16),
            jax.ShapeDtypeStruct((V_W, T), jnp.bfloat16),
            jax.ShapeDtypeStruct((V_W, T), jnp.bfloat16),
            jax.ShapeDtypeStruct((N_GATES, T), jnp.float32),
        ],
        compiler_params=_params("arbitrary"),
        name="mlstm_proj",
    )(h2, gain, w_k, w_t, w_gate_t)


def _segmented_scan(x, op, lane, seg):
    shift = 1
    while shift < seg:
        moved = pltpu.roll(x, shift, 1)
        x = jnp.where((lane % seg) >= shift, op(x, moved), x)
        shift *= 2
    return x


def _gate_kernel(gt_ref, bias_ref, r_ref):
    L = MLSTM_CHUNK
    S = gt_ref.shape[1]
    g = gt_ref[...] + bias_ref[...]
    i_pre = g[0:N_HEADS]
    f_pre = g[N_HEADS:N_GATES]
    logf = jnp.minimum(f_pre, 0.0) - jnp.log1p(jnp.exp(-jnp.abs(f_pre)))
    lane = lax.broadcasted_iota(jnp.int32, (N_HEADS, S), 1)
    b = _segmented_scan(logf, jnp.add, lane, L)
    a = i_pre - b
    amax = _segmented_scan(a, jnp.maximum, lane, L)

    r_ref[0, 0:N_HEADS, :] = a
    r_ref[0, 5 * N_HEADS:N_GATE_ROWS, :] = jnp.zeros((N_GATE_ROWS - 5 * N_HEADS, S), jnp.float32)
    m_prev = jnp.zeros((N_HEADS, 1), jnp.float32)
    for c in range(S // L):
        seg = slice(c * L, (c + 1) * L)
        big_m = jnp.maximum(m_prev, amax[:, seg])
        m_last = big_m[:, L - 1:L]
        r_ref[0, N_HEADS:2 * N_HEADS, seg] = big_m
        r_ref[0, 2 * N_HEADS:3 * N_HEADS, seg] = -(b[:, seg] + big_m)
        r_ref[0, 3 * N_HEADS:4 * N_HEADS, seg] = m_prev - big_m
        r_ref[0, 4 * N_HEADS:5 * N_HEADS, seg] = a[:, seg] - m_last
        m_prev = b[:, (c + 1) * L - 1:(c + 1) * L] + m_last


def _gate_layer(gates_t, bias, B, S, layer):
    return pl.pallas_call(
        _gate_kernel,
        grid=(B,),
        in_specs=[
            pl.BlockSpec((N_GATES, S), lambda b: (0, b)),
            _resident((N_GATES, 1), layer),
        ],
        out_specs=pl.BlockSpec((1, N_GATE_ROWS, S), lambda b: (b, 0, 0)),
        out_shape=jax.ShapeDtypeStruct((B, N_GATE_ROWS, S), jnp.float32),
        compiler_params=_params("arbitrary"),
        name="mlstm_gates",
    )(gates_t, bias)


def _hi_lo_rows(row):
    hi = row.astype(jnp.bfloat16).astype(jnp.float32)
    idx = lax.broadcasted_iota(jnp.int32, (16, row.shape[1]), 0)
    slab = jnp.where(idx == 0, hi, jnp.where(idx == 1, row - hi, 0.0))
    return slab.astype(jnp.bfloat16)


def _chunk_kernel(qt_ref, k_ref, vt_ref, ogt_ref, r_ref, hn_ref, out_ref, c_ref, n_ref, gated_ref):
    L = MLSTM_CHUNK

    @pl.when(pl.program_id(1) == 0)
    def _():
        c_ref[...] = jnp.zeros(c_ref.shape, jnp.float32)
        n_ref[...] = jnp.zeros(n_ref.shape, jnp.float32)

    s_idx = lax.broadcasted_iota(jnp.int32, (L, L), 0)
    t_idx = lax.broadcasted_iota(jnp.int32, (L, L), 1)
    causal = s_idx <= t_idx
    eye = (s_idx == t_idx).astype(jnp.bfloat16)

    tok = lambda c: slice(c * L, (c + 1) * L)
    qk = lambda h: slice(h * DQK, (h + 1) * DQK)
    vv = lambda h: slice(h * DV, (h + 1) * DV)
    units = [(c, h) for c in range(CHUNKS_PER_STEP) for h in range(N_HEADS)]
    rows = [r_ref[0, :, tok(c)] for c in range(CHUNKS_PER_STEP)]
    a_cols = [r[0:2 * N_HEADS, :].T for r in rows]
    gate_row = lambda c, h, kind: rows[c][kind * N_HEADS + h:kind * N_HEADS + h + 1, :]

    st = {u: _dot(k_ref[tok(u[0]), qk(u[1])], qt_ref[qk(u[1]), tok(u[0])]) for u in units}
    for c, h in units:
        decay_w = jnp.exp(jnp.where(causal, a_cols[c][:, h:h + 1] - gate_row(c, h, 1), -jnp.inf))
        st[c, h] = st[c, h] * decay_w
    den_intra = {u: jnp.sum(st[u], axis=0, keepdims=True) for u in units}
    num_intra = {u: _dot(vt_ref[vv(u[1]), tok(u[0])], st[u].astype(jnp.bfloat16)) for u in units}

    ct = [c_ref[h] for h in range(N_HEADS)]
    n = [n_ref[h] for h in range(N_HEADS)]
    for c, h in units:
        qt = qt_ref[qk(h), tok(c)]
        k = k_ref[tok(c), qk(h)]
        vt = vt_ref[vv(h), tok(c)]
        inter = jnp.exp(gate_row(c, h, 3))
        w = jnp.exp(gate_row(c, h, 4))

        state = jnp.concatenate([ct[h].astype(jnp.bfloat16), _hi_lo_rows(n[h])], axis=0)
        from_state = _dot(state, qt)
        qn = from_state[DV:DV + 1, :] + from_state[DV + 1:DV + 2, :]
        num = inter * from_state[0:DV, :] + num_intra[c, h]
        den = inter * qn + den_intra[c, h]
        r = 1.0 / jnp.maximum(jnp.abs(den), jnp.exp(gate_row(c, h, 2)))
        ss = jnp.sum(num * num, axis=0, keepdims=True)
        scale = r * lax.rsqrt(r * r * ss * (1.0 / DV) + EPS)
        gated = num * scale * hn_ref[vv(h), :] * ogt_ref[vv(h), tok(c)].astype(jnp.float32)
        gated_ref[vv(h), tok(c)] = gated.astype(jnp.bfloat16)

        vw = jnp.concatenate([(vt.astype(jnp.float32) * w).astype(jnp.bfloat16), _hi_lo_rows(w)],
                             axis=0)
        upd = _dot(vw, k)
        decay = inter[:, L - 1:L]
        ct[h] = decay * ct[h] + upd[0:DV, :]
        n[h] = decay * n[h] + upd[DV:DV + 1, :] + upd[DV + 1:DV + 2, :]

    for h in range(N_HEADS):
        c_ref[h] = ct[h]
        n_ref[h] = n[h]
    for c in range(CHUNKS_PER_STEP):
        out_ref[tok(c), :] = _dot(eye, gated_ref[:, tok(c)], _NT).astype(out_ref.dtype)


def _chunk_layer(qt, k, vt, ogt, rows, head_norm_b, B, S, layer):
    T = k.shape[0]
    step = MLSTM_CHUNK * CHUNKS_PER_STEP
    ns = S // step
    tok = lambda b, c: (b * ns + c, 0)
    lanes = lambda b, c: (0, b * ns + c)
    return pl.pallas_call(
        _chunk_kernel,
        grid=(B, ns),
        in_specs=[
            pl.BlockSpec((QK_W, step), lanes),
            pl.BlockSpec((step, QK_W), tok),
            pl.BlockSpec((V_W, step), lanes),
            pl.BlockSpec((V_W, step), lanes),
            pl.BlockSpec((1, N_GATE_ROWS, step), lambda b, c: (b, 0, c)),
            _resident(head_norm_b.shape[1:], layer),
        ],
        out_specs=pl.BlockSpec((step, V_W), tok),
        out_shape=jax.ShapeDtypeStruct((T, V_W), jnp.bfloat16),
        scratch_shapes=[
            pltpu.VMEM((N_HEADS, DV, DQK), jnp.float32),
            pltpu.VMEM((N_HEADS, 1, DQK), jnp.float32),
            pltpu.VMEM((V_W, step), jnp.bfloat16),
        ],
        compiler_params=_params("arbitrary", "arbitrary"),
        name="mlstm_chunk",
    )(qt, k, vt, ogt, rows, head_norm_b)


def kernel(x, pool_norm, pool_w, pool_scale, mlstm_norm, mlstm_w_in, mlstm_gate_bias,
           mlstm_head_norm, mlstm_w_out, ffn_norm, ffn_w_in, ffn_w_out, final_norm):
    B, S, D = x.shape
    depth = ffn_norm.shape[0]
    bf16 = jnp.bfloat16
    row = lambda a: a.reshape(a.shape[0], 1, a.shape[1])
    pool_w_b = pool_w.astype(bf16)
    w_k_b = mlstm_w_in[:, :, QK_W:2 * QK_W].astype(bf16)
    w_t_b = jnp.swapaxes(
        jnp.concatenate([mlstm_w_in[:, :, :QK_W], mlstm_w_in[:, :, 2 * QK_W:MAIN_W]], axis=2),
        1, 2).astype(bf16)
    head_norm_b = jnp.broadcast_to(mlstm_head_norm[:, :, None],
                                   mlstm_head_norm.shape + (MLSTM_CHUNK,))
    w_gate_t_b = jnp.swapaxes(mlstm_w_in[:, :, MAIN_W:], 1, 2).astype(bf16)
    gate_bias = mlstm_gate_bias.reshape(-1, N_GATES, 1)
    w_mix_b = mlstm_w_out.astype(bf16)
    ffn_w_in_b = ffn_w_in.astype(bf16)
    ffn_w_out_b = ffn_w_out.astype(bf16)
    fgain = final_norm.reshape(1, D)

    h2 = x.reshape(B * S, D)
    for i in range(depth):
        j = i // 2
        last = i == depth - 1
        if i % 2 == 0:
            h2 = _pool_layer(h2.reshape(B, S, D), row(pool_norm), pool_w_b, row(pool_scale), j)
            h2 = _ffn_layer(h2.reshape(B * S, D), row(ffn_norm), ffn_w_in_b, ffn_w_out_b, fgain, i, last)
        else:
            qt, k, vt, ogt, gates_t = _proj_layer(h2, row(mlstm_norm), w_k_b, w_t_b, w_gate_t_b, j)
            rows = _gate_layer(gates_t, gate_bias, B, S, j)
            heads = _chunk_layer(qt, k, vt, ogt, rows, head_norm_b, B, S, j)
            h2 = _ffn_layer(h2, row(ffn_norm), ffn_w_in_b, ffn_w_out_b, fgain, i, last,
                            heads=heads, w_mix=w_mix_b, mix_layer=j)
    return h2.reshape(B, S, D)
```

```python
import functools
import math

import jax
import jax.numpy as jnp
from jax import lax
from jax.experimental import pallas as pl
from jax.experimental.pallas import tpu as pltpu

D_MODEL = 1024
POOL_WINDOWS = (2, 4, 8, 16)
POOL_GROUP_DIM = D_MODEL // len(POOL_WINDOWS)
POOL_HALO = 32

N_HEADS = 4
DV = D_MODEL // N_HEADS
DQK = DV // 2
QK_W = N_HEADS * DQK
V_W = N_HEADS * DV
MAIN_W = 2 * QK_W + 2 * V_W
N_GATES = 2 * N_HEADS
QG_ROWS = QK_W + 16

D_FF = int(math.ceil(8 * D_MODEL / 3 / 256) * 256)
FF_CHUNK = 256

EPS = 1e-6

POOL_TILE = 1024
FFN_TILE = 512
PROJ_TILE = 1024
MLSTM_CHUNK = 256
CHUNKS_PER_STEP = 2
N_GATE_ROWS = 24

VMEM_LIMIT = 56 * 1024 * 1024

_NT = (((1,), (1,)), ((), ()))


def _params(*semantics):
    return pltpu.CompilerParams(dimension_semantics=semantics,
                                vmem_limit_bytes=VMEM_LIMIT)


def _resident(shape, layer=None):
    zeros = (0,) * len(shape)
    if layer is None:
        return pl.BlockSpec(shape, lambda *_: zeros, pipeline_mode=pl.Buffered(1))
    return pl.BlockSpec((None,) + shape, lambda *_: (layer,) + zeros,
                        pipeline_mode=pl.Buffered(1))


def _rms_norm(x, gain):
    return x * lax.rsqrt(jnp.mean(x * x, axis=-1, keepdims=True) + EPS) * gain


def _dot(a, b, dims=None):
    if dims is None:
        return jnp.dot(a, b, preferred_element_type=jnp.float32)
    return lax.dot_general(a, b, dims, preferred_element_type=jnp.float32)


def _pool_kernel(x_ref, gain_ref, w_ref, scale_ref, o_ref, ext_ref, *stage_refs):
    s = pl.program_id(1)
    ts = x_ref.shape[1]
    rows = POOL_HALO + ts
    gd = POOL_GROUP_DIM
    x = x_ref[0]
    xn = _rms_norm(x, gain_ref[...])

    @pl.when(s == 0)
    def _():
        ext_ref[0:POOL_HALO, :] = jnp.zeros((POOL_HALO, D_MODEL), jnp.float32)

    ext_ref[POOL_HALO:rows, :] = xn

    prev = ext_ref
    for j, cur in enumerate(stage_refs, start=1):
        shift = 2 ** (j - 1)
        lo = 8 * j
        skip = 0 if j == 1 else gd
        cur[lo:rows, :] = prev[lo:rows, skip:] + prev[lo - shift:rows - shift, skip:]
        prev = cur

    pos = (s * ts + 1 + lax.broadcasted_iota(jnp.int32, (ts, 1), 0)).astype(jnp.float32)
    for g, win in enumerate(POOL_WINDOWS):
        cols = slice(g * gd, (g + 1) * gd)
        tot = stage_refs[g][POOL_HALO:rows, 0:gd]
        inv_cnt = 1.0 / jnp.minimum(pos, float(win))
        pooled = tot * inv_cnt - xn[:, cols]
        y = _dot(pooled.astype(jnp.bfloat16), w_ref[g])
        o_ref[0, :, cols] = x[:, cols] + y * scale_ref[:, cols]

    ext_ref[0:POOL_HALO, :] = ext_ref[ts:rows, :]


def _pool_layer(h, gain, w, scale, layer):
    B, S, D = h.shape
    ts = POOL_TILE
    return pl.pallas_call(
        _pool_kernel,
        grid=(B, S // ts),
        in_specs=[
            pl.BlockSpec((1, ts, D), lambda b, s: (b, s, 0)),
            _resident((1, D), layer),
            _resident(w.shape[1:], layer),
            _resident((1, D), layer),
        ],
        out_specs=pl.BlockSpec((1, ts, D), lambda b, s: (b, s, 0)),
        out_shape=jax.ShapeDtypeStruct(h.shape, h.dtype),
        scratch_shapes=[pltpu.VMEM((POOL_HALO + ts, D), jnp.float32)] + [
            pltpu.VMEM((POOL_HALO + ts, D - g * POOL_GROUP_DIM), jnp.float32)
            for g in range(len(POOL_WINDOWS))],
        compiler_params=_params("arbitrary", "arbitrary"),
        name="pool_mixer",
    )(h, gain, w, scale)


def _ffn_kernel(*refs, layer, mixer_proj, final_norm):
    if mixer_proj:
        x_ref, heads_ref, w_mix_ref, *refs = refs
    else:
        x_ref, *refs = refs
    (gain_ref, w_in_hbm, w_out_hbm, fgain_ref, o_ref,
     w_in_ref, w_out_ref, act_ref, stage_in, stage_out, sem) = refs
    n_chunks = D_FF // FF_CHUNK

    def chunk_copies(j, slot):
        lo = j * FF_CHUNK
        return (
            pltpu.make_async_copy(w_in_hbm.at[layer, :, lo:lo + FF_CHUNK],
                                  stage_in.at[slot, 0], sem.at[slot, 0]),
            pltpu.make_async_copy(w_in_hbm.at[layer, :, D_FF + lo:D_FF + lo + FF_CHUNK],
                                  stage_in.at[slot, 1], sem.at[slot, 1]),
            pltpu.make_async_copy(w_out_hbm.at[layer, lo:lo + FF_CHUNK, :],
                                  stage_out.at[slot], sem.at[slot, 2]),
        )

    def body(stream_weights):
        if stream_weights:
            for cp in chunk_copies(0, 0):
                cp.start()
        if mixer_proj:
            x = x_ref[...] + _dot(heads_ref[...], w_mix_ref[...])
        else:
            x = x_ref[...]
        xn = _rms_norm(x, gain_ref[...]).astype(jnp.bfloat16)
        for j in range(n_chunks):
            lo = j * FF_CHUNK
            if stream_weights:
                slot = j % 2
                if j + 1 < n_chunks:
                    for cp in chunk_copies(j + 1, 1 - slot):
                        cp.start()
                for cp in chunk_copies(j, slot):
                    cp.wait()
                w_in_ref[:, lo:lo + FF_CHUNK] = stage_in[slot, 0].astype(jnp.bfloat16)
                w_in_ref[:, D_FF + lo:D_FF + lo + FF_CHUNK] = stage_in[slot, 1].astype(jnp.bfloat16)
                w_out_ref[lo:lo + FF_CHUNK, :] = stage_out[slot].astype(jnp.bfloat16)
            gate = _dot(xn, w_in_ref[:, lo:lo + FF_CHUNK])
            up = _dot(xn, w_in_ref[:, D_FF + lo:D_FF + lo + FF_CHUNK])
            act_ref[:, lo:lo + FF_CHUNK] = (gate * jax.nn.sigmoid(gate) * up).astype(jnp.bfloat16)
        out = x + _dot(act_ref[...], w_out_ref[...])
        if final_norm:
            out = _rms_norm(out, fgain_ref[...])
        o_ref[...] = out

    first = pl.program_id(0) == 0
    pl.when(first)(functools.partial(body, True))
    pl.when(jnp.logical_not(first))(functools.partial(body, False))


def _ffn_layer(h2, gain, w_in, w_out, final_gain, layer, final_norm, heads=None, w_mix=None,
               mix_layer=None):
    T, D = h2.shape
    tm = FFN_TILE
    tok = pl.BlockSpec((tm, D), lambda i: (i, 0))
    hbm = pl.BlockSpec(memory_space=pl.ANY)
    mixer_proj = heads is not None
    mix_specs = [tok, _resident(w_mix.shape[1:], mix_layer)] if mixer_proj else []
    mix_args = (heads, w_mix) if mixer_proj else ()
    return pl.pallas_call(
        functools.partial(_ffn_kernel, layer=layer, mixer_proj=mixer_proj, final_norm=final_norm),
        grid=(T // tm,),
        in_specs=[tok] + mix_specs + [_resident((1, D), layer), hbm, hbm, _resident((1, D))],
        out_specs=tok,
        out_shape=jax.ShapeDtypeStruct(h2.shape, h2.dtype),
        scratch_shapes=[
            pltpu.VMEM((D, 2 * D_FF), jnp.bfloat16),
            pltpu.VMEM((D_FF, D), jnp.bfloat16),
            pltpu.VMEM((tm, D_FF), jnp.bfloat16),
            pltpu.VMEM((2, 2, D, FF_CHUNK), jnp.float32),
            pltpu.VMEM((2, FF_CHUNK, D), jnp.float32),
            pltpu.SemaphoreType.DMA((2, 3)),
        ],
        compiler_params=_params("arbitrary"),
        name="swiglu" + ("_mix" if mixer_proj else "") + ("_final" if final_norm else ""),
    )(h2, *mix_args, gain, w_in, w_out, final_gain)


def _proj_kernel(x_ref, gain_ref, wk_ref, wt_ref, qt_ref, k_ref, vt_ref, ogt_ref, gt_ref):
    xn = _rms_norm(x_ref[...], gain_ref[...]).astype(jnp.bfloat16)
    k_ref[...] = _dot(xn, wk_ref[...]).astype(k_ref.dtype)
    qg = _dot(wt_ref[0:QG_ROWS, :], xn, _NT)
    qt_ref[...] = (qg[0:QK_W, :] * (DQK ** -0.5)).astype(qt_ref.dtype)
    gt_ref[...] = qg[QK_W:QK_W + N_GATES, :]
    vt_ref[...] = _dot(wt_ref[QG_ROWS:QG_ROWS + V_W, :], xn, _NT).astype(vt_ref.dtype)
    ogt_ref[...] = jax.nn.sigmoid(_dot(wt_ref[QG_ROWS + V_W:, :], xn, _NT)).astype(ogt_ref.dtype)


def _proj_layer(h2, gain, w_k, w_t, layer):
    T, D = h2.shape
    tm = PROJ_TILE
    tok = lambda i: (i, 0)
    lanes = lambda i: (0, i)
    return pl.pallas_call(
        _proj_kernel,
        grid=(T // tm,),
        in_specs=[
            pl.BlockSpec((tm, D), tok),
            _resident((1, D), layer),
            _resident(w_k.shape[1:], layer),
            _resident(w_t.shape[1:], layer),
        ],
        out_specs=[
            pl.BlockSpec((QK_W, tm), lanes),
            pl.BlockSpec((tm, QK_W), tok),
            pl.BlockSpec((V_W, tm), lanes),
            pl.BlockSpec((V_W, tm), lanes),
            pl.BlockSpec((N_GATES, tm), lanes),
        ],
        out_shape=[
            jax.ShapeDtypeStruct((QK_W, T), jnp.bfloat16),
            jax.ShapeDtypeStruct((T, QK_W), jnp.bfloat16),
            jax.ShapeDtypeStruct((V_W, T), jnp.bfloat16),
            jax.ShapeDtypeStruct((V_W, T), jnp.bfloat16),
            jax.ShapeDtypeStruct((N_GATES, T), jnp.float32),
        ],
        compiler_params=_params("arbitrary"),
        name="mlstm_proj",
    )(h2, gain, w_k, w_t)


def _segmented_scan(x, op, lane, seg):
    shift = 1
    while shift < seg:
        moved = pltpu.roll(x, shift, 1)
        x = jnp.where((lane % seg) >= shift, op(x, moved), x)
        shift *= 2
    return x


def _gate_kernel(gt_ref, bias_ref, r_ref):
    L = MLSTM_CHUNK
    S = gt_ref.shape[1]
    g = gt_ref[...] + bias_ref[...]
    i_pre = g[0:N_HEADS]
    f_pre = g[N_HEADS:N_GATES]
    logf = jnp.minimum(f_pre, 0.0) - jnp.log1p(jnp.exp(-jnp.abs(f_pre)))
    lane = lax.broadcasted_iota(jnp.int32, (N_HEADS, S), 1)
    b = _segmented_scan(logf, jnp.add, lane, L)
    a = i_pre - b
    amax = _segmented_scan(a, jnp.maximum, lane, L)

    r_ref[0, 0:N_HEADS, :] = a
    r_ref[0, 5 * N_HEADS:N_GATE_ROWS, :] = jnp.zeros((N_GATE_ROWS - 5 * N_HEADS, S), jnp.float32)
    m_prev = jnp.zeros((N_HEADS, 1), jnp.float32)
    for c in range(S // L):
        seg = slice(c * L, (c + 1) * L)
        big_m = jnp.maximum(m_prev, amax[:, seg])
        m_last = big_m[:, L - 1:L]
        r_ref[0, N_HEADS:2 * N_HEADS, seg] = big_m
        r_ref[0, 2 * N_HEADS:3 * N_HEADS, seg] = -(b[:, seg] + big_m)
        r_ref[0, 3 * N_HEADS:4 * N_HEADS, seg] = m_prev - big_m
        r_ref[0, 4 * N_HEADS:5 * N_HEADS, seg] = a[:, seg] - m_last
        m_prev = b[:, (c + 1) * L - 1:(c + 1) * L] + m_last


def _gate_layer(gates_t, bias, B, S, layer):
    return pl.pallas_call(
        _gate_kernel,
        grid=(B,),
        in_specs=[
            pl.BlockSpec((N_GATES, S), lambda b: (0, b)),
            _resident((N_GATES, 1), layer),
        ],
        out_specs=pl.BlockSpec((1, N_GATE_ROWS, S), lambda b: (b, 0, 0)),
        out_shape=jax.ShapeDtypeStruct((B, N_GATE_ROWS, S), jnp.float32),
        compiler_params=_params("arbitrary"),
        name="mlstm_gates",
    )(gates_t, bias)


def _hi_lo_rows(row):
    hi = row.astype(jnp.bfloat16).astype(jnp.float32)
    idx = lax.broadcasted_iota(jnp.int32, (16, row.shape[1]), 0)
    slab = jnp.where(idx == 0, hi, jnp.where(idx == 1, row - hi, 0.0))
    return slab.astype(jnp.bfloat16)


def _chunk_kernel(qt_ref, k_ref, vt_ref, ogt_ref, r_ref, hn_ref, out_ref, c_ref, n_ref, gated_ref):
    L = MLSTM_CHUNK

    @pl.when(pl.program_id(1) == 0)
    def _():
        c_ref[...] = jnp.zeros(c_ref.shape, jnp.float32)
        n_ref[...] = jnp.zeros(n_ref.shape, jnp.float32)

    s_idx = lax.broadcasted_iota(jnp.int32, (L, L), 0)
    t_idx = lax.broadcasted_iota(jnp.int32, (L, L), 1)
    causal = s_idx <= t_idx
    eye = (s_idx == t_idx).astype(jnp.bfloat16)

    tok = lambda c: slice(c * L, (c + 1) * L)
    qk = lambda h: slice(h * DQK, (h + 1) * DQK)
    vv = lambda h: slice(h * DV, (h + 1) * DV)
    units = [(c, h) for c in range(CHUNKS_PER_STEP) for h in range(N_HEADS)]
    rows = [r_ref[0, :, tok(c)] for c in range(CHUNKS_PER_STEP)]
    a_cols = [r[0:2 * N_HEADS, :].T for r in rows]
    gate_row = lambda c, h, kind: rows[c][kind * N_HEADS + h:kind * N_HEADS + h + 1, :]

    st = {u: _dot(k_ref[tok(u[0]), qk(u[1])], qt_ref[qk(u[1]), tok(u[0])]) for u in units}
    for c, h in units:
        decay_w = jnp.exp(jnp.where(causal, a_cols[c][:, h:h + 1] - gate_row(c, h, 1), -jnp.inf))
        st[c, h] = st[c, h] * decay_w
    den_intra = {u: jnp.sum(st[u], axis=0, keepdims=True) for u in units}
    num_intra = {u: _dot(vt_ref[vv(u[1]), tok(u[0])], st[u].astype(jnp.bfloat16)) for u in units}

    ct = [c_ref[h] for h in range(N_HEADS)]
    n = [n_ref[h] for h in range(N_HEADS)]
    for c, h in units:
        qt = qt_ref[qk(h), tok(c)]
        k = k_ref[tok(c), qk(h)]
        vt = vt_ref[vv(h), tok(c)]
        inter = jnp.exp(gate_row(c, h, 3))
        w = jnp.exp(gate_row(c, h, 4))

        state = jnp.concatenate([ct[h].astype(jnp.bfloat16), _hi_lo_rows(n[h])], axis=0)
        from_state = _dot(state, qt)
        qn = from_state[DV:DV + 1, :] + from_state[DV + 1:DV + 2, :]
        num = inter * from_state[0:DV, :] + num_intra[c, h]
        den = inter * qn + den_intra[c, h]
        r = 1.0 / jnp.maximum(jnp.abs(den), jnp.exp(gate_row(c, h, 2)))
        ss = jnp.sum(num * num, axis=0, keepdims=True)
        scale = r * lax.rsqrt(r * r * ss * (1.0 / DV) + EPS)
        gated = num * scale * hn_ref[vv(h), :] * ogt_ref[vv(h), tok(c)].astype(jnp.float32)
        gated_ref[vv(h), tok(c)] = gated.astype(jnp.bfloat16)

        vw = jnp.concatenate([(vt.astype(jnp.float32) * w).astype(jnp.bfloat16), _hi_lo_rows(w)],
                             axis=0)
        upd = _dot(vw, k)
        decay = inter[:, L - 1:L]
        ct[h] = decay * ct[h] + upd[0:DV, :]
        n[h] = decay * n[h] + upd[DV:DV + 1, :] + upd[DV + 1:DV + 2, :]

    for h in range(N_HEADS):
        c_ref[h] = ct[h]
        n_ref[h] = n[h]
    for c in range(CHUNKS_PER_STEP):
        out_ref[tok(c), :] = _dot(eye, gated_ref[:, tok(c)], _NT).astype(out_ref.dtype)


def _chunk_layer(qt, k, vt, ogt, rows, head_norm_b, B, S, layer):
    T = k.shape[0]
    step = MLSTM_CHUNK * CHUNKS_PER_STEP
    ns = S // step
    tok = lambda b, c: (b * ns + c, 0)
    lanes = lambda b, c: (0, b * ns + c)
    return pl.pallas_call(
        _chunk_kernel,
        grid=(B, ns),
        in_specs=[
            pl.BlockSpec((QK_W, step), lanes),
            pl.BlockSpec((step, QK_W), tok),
            pl.BlockSpec((V_W, step), lanes),
            pl.BlockSpec((V_W, step), lanes),
            pl.BlockSpec((1, N_GATE_ROWS, step), lambda b, c: (b, 0, c)),
            _resident(head_norm_b.shape[1:], layer),
        ],
        out_specs=pl.BlockSpec((step, V_W), tok),
        out_shape=jax.ShapeDtypeStruct((T, V_W), jnp.bfloat16),
        scratch_shapes=[
            pltpu.VMEM((N_HEADS, DV, DQK), jnp.float32),
            pltpu.VMEM((N_HEADS, 1, DQK), jnp.float32),
            pltpu.VMEM((V_W, step), jnp.bfloat16),
        ],
        compiler_params=_params("arbitrary", "arbitrary"),
        name="mlstm_chunk",
    )(qt, k, vt, ogt, rows, head_norm_b)


def kernel(x, pool_norm, pool_w, pool_scale, mlstm_norm, mlstm_w_in, mlstm_gate_bias,
           mlstm_head_norm, mlstm_w_out, ffn_norm, ffn_w_in, ffn_w_out, final_norm):
    B, S, D = x.shape
    depth = ffn_norm.shape[0]
    bf16 = jnp.bfloat16
    row = lambda a: a.reshape(a.shape[0], 1, a.shape[1])
    pool_w_b = pool_w.astype(bf16)
    w_k_b = mlstm_w_in[:, :, QK_W:2 * QK_W].astype(bf16)
    w_t_b = jnp.swapaxes(
        jnp.concatenate([mlstm_w_in[:, :, :QK_W], mlstm_w_in[:, :, MAIN_W:],
                         jnp.zeros(mlstm_w_in.shape[:2] + (QG_ROWS - QK_W - N_GATES,), mlstm_w_in.dtype),
                         mlstm_w_in[:, :, 2 * QK_W:MAIN_W]], axis=2),
        1, 2).astype(bf16)
    head_norm_b = jnp.broadcast_to(mlstm_head_norm[:, :, None],
                                   mlstm_head_norm.shape + (MLSTM_CHUNK,))
    gate_bias = mlstm_gate_bias.reshape(-1, N_GATES, 1)
    w_mix_b = mlstm_w_out.astype(bf16)
    fgain = final_norm.reshape(1, D)

    h2 = x.reshape(B * S, D)
    for i in range(depth):
        j = i // 2
        last = i == depth - 1
        if i % 2 == 0:
            h2 = _pool_layer(h2.reshape(B, S, D), row(pool_norm), pool_w_b, row(pool_scale), j)
            h2 = _ffn_layer(h2.reshape(B * S, D), row(ffn_norm), ffn_w_in, ffn_w_out, fgain, i, last)
        else:
            qt, k, vt, ogt, gates_t = _proj_layer(h2, row(mlstm_norm), w_k_b, w_t_b, j)
            rows = _gate_layer(gates_t, gate_bias, B, S, j)
            heads = _chunk_layer(qt, k, vt, ogt, rows, head_norm_b, B, S, j)
            h2 = _ffn_layer(h2, row(ffn_norm), ffn_w_in, ffn_w_out, fgain, i, last,
                            heads=heads, w_mix=w_mix_b, mix_layer=j)
    return h2.reshape(B, S, D)
```

```python
import functools
import math

import jax
import jax.numpy as jnp
from jax import lax
from jax.experimental import pallas as pl
from jax.experimental.pallas import tpu as pltpu

D_MODEL = 1024
POOL_WINDOWS = (2, 4, 8, 16)
POOL_GROUP_DIM = D_MODEL // len(POOL_WINDOWS)
POOL_HALO = 32

N_HEADS = 4
DV = D_MODEL // N_HEADS
DQK = DV // 2
QK_W = N_HEADS * DQK
V_W = N_HEADS * DV
MAIN_W = 2 * QK_W + 2 * V_W
N_GATES = 2 * N_HEADS
QG_ROWS = QK_W + 16

D_FF = int(math.ceil(8 * D_MODEL / 3 / 256) * 256)
FF_CHUNK = 256
LANE = 128

EPS = 1e-6

POOL_TILE = 1024
FFN_TILE = 512
PROJ_TILE = 1024
MLSTM_CHUNK = 256
CHUNKS_PER_STEP = 4
N_GATE_ROWS = 24

VMEM_LIMIT = 56 * 1024 * 1024

_NT = (((1,), (1,)), ((), ()))


def _params(*semantics):
    return pltpu.CompilerParams(dimension_semantics=semantics,
                                vmem_limit_bytes=VMEM_LIMIT)


def _resident(shape, layer=None):
    zeros = (0,) * len(shape)
    if layer is None:
        return pl.BlockSpec(shape, lambda *_: zeros, pipeline_mode=pl.Buffered(1))
    return pl.BlockSpec((None,) + shape, lambda *_: (layer,) + zeros,
                        pipeline_mode=pl.Buffered(1))


def _rms_norm(x, gain):
    return x * lax.rsqrt(jnp.mean(x * x, axis=-1, keepdims=True) + EPS) * gain


def _dot(a, b, dims=None):
    if dims is None:
        return jnp.dot(a, b, preferred_element_type=jnp.float32)
    return lax.dot_general(a, b, dims, preferred_element_type=jnp.float32)


def _pool_kernel(x_ref, gain_ref, w_ref, scale_ref, o_ref, ext_ref, *stage_refs):
    s = pl.program_id(1)
    ts = x_ref.shape[1]
    rows = POOL_HALO + ts
    gd = POOL_GROUP_DIM
    x = x_ref[0]
    xn = _rms_norm(x, gain_ref[...])

    @pl.when(s == 0)
    def _():
        ext_ref[0:POOL_HALO, :] = jnp.zeros((POOL_HALO, D_MODEL), jnp.float32)

    ext_ref[POOL_HALO:rows, :] = xn

    prev = ext_ref
    for j, cur in enumerate(stage_refs, start=1):
        shift = 2 ** (j - 1)
        lo = 8 * j
        skip = 0 if j == 1 else gd
        cur[lo:rows, :] = prev[lo:rows, skip:] + prev[lo - shift:rows - shift, skip:]
        prev = cur

    pos = (s * ts + 1 + lax.broadcasted_iota(jnp.int32, (ts, 1), 0)).astype(jnp.float32)
    for g, win in enumerate(POOL_WINDOWS):
        cols = slice(g * gd, (g + 1) * gd)
        tot = stage_refs[g][POOL_HALO:rows, 0:gd]
        inv_cnt = 1.0 / jnp.minimum(pos, float(win))
        pooled = tot * inv_cnt - xn[:, cols]
        y = _dot(pooled.astype(jnp.bfloat16), w_ref[g])
        o_ref[0, :, cols] = x[:, cols] + y * scale_ref[:, cols]

    ext_ref[0:POOL_HALO, :] = ext_ref[ts:rows, :]


def _pool_layer(h, gain, w, scale, layer):
    B, S, D = h.shape
    ts = POOL_TILE
    return pl.pallas_call(
        _pool_kernel,
        grid=(B, S // ts),
        in_specs=[
            pl.BlockSpec((1, ts, D), lambda b, s: (b, s, 0)),
            _resident((1, D), layer),
            _resident(w.shape[1:], layer),
            _resident((1, D), layer),
        ],
        out_specs=pl.BlockSpec((1, ts, D), lambda b, s: (b, s, 0)),
        out_shape=jax.ShapeDtypeStruct(h.shape, h.dtype),
        scratch_shapes=[pltpu.VMEM((POOL_HALO + ts, D), jnp.float32)] + [
            pltpu.VMEM((POOL_HALO + ts, D - g * POOL_GROUP_DIM), jnp.float32)
            for g in range(len(POOL_WINDOWS))],
        compiler_params=_params("arbitrary", "arbitrary"),
        name="pool_mixer",
    )(h, gain, w, scale)


def _ffn_kernel(*refs, layer, mixer_proj, final_norm):
    if mixer_proj:
        x_ref, heads_ref, w_mix_ref, *refs = refs
    else:
        x_ref, *refs = refs
    (gain_ref, w_in_hbm, w_out_hbm, fgain_ref, o_ref,
     w_in_ref, w_out_ref, act_ref, stage_in, stage_out, sem) = refs
    n_chunks = D_FF // FF_CHUNK

    def chunk_copies(j, slot):
        lo = j * FF_CHUNK
        return (
            pltpu.make_async_copy(w_in_hbm.at[layer, :, lo:lo + FF_CHUNK],
                                  stage_in.at[slot, 0], sem.at[slot, 0]),
            pltpu.make_async_copy(w_in_hbm.at[layer, :, D_FF + lo:D_FF + lo + FF_CHUNK],
                                  stage_in.at[slot, 1], sem.at[slot, 1]),
            pltpu.make_async_copy(w_out_hbm.at[layer, lo:lo + FF_CHUNK, :],
                                  stage_out.at[slot], sem.at[slot, 2]),
        )

    def body(stream_weights):
        if stream_weights:
            for cp in chunk_copies(0, 0):
                cp.start()
        if mixer_proj:
            x = x_ref[...] + _dot(heads_ref[...], w_mix_ref[...])
        else:
            x = x_ref[...]
        xn = _rms_norm(x, gain_ref[...]).astype(jnp.bfloat16)
        for j in range(n_chunks):
            lo = j * FF_CHUNK
            if stream_weights:
                slot = j % 2
                if j + 1 < n_chunks:
                    for cp in chunk_copies(j + 1, 1 - slot):
                        cp.start()
                for cp in chunk_copies(j, slot):
                    cp.wait()
                w_in_ref[:, lo:lo + FF_CHUNK] = stage_in[slot, 0].astype(jnp.bfloat16)
                w_in_ref[:, D_FF + lo:D_FF + lo + FF_CHUNK] = stage_in[slot, 1].astype(jnp.bfloat16)
                w_out_ref[lo:lo + FF_CHUNK, :] = stage_out[slot].astype(jnp.bfloat16)
            gate = _dot(xn, w_in_ref[:, lo:lo + FF_CHUNK])
            up = _dot(xn, w_in_ref[:, D_FF + lo:D_FF + lo + FF_CHUNK])
            act_ref[:, lo:lo + FF_CHUNK] = (gate * jax.nn.sigmoid(gate) * up).astype(jnp.bfloat16)
        out = x + _dot(act_ref[...], w_out_ref[...])
        if final_norm:
            out = _rms_norm(out, fgain_ref[...])
        o_ref[...] = out

    first = pl.program_id(0) == 0
    pl.when(first)(functools.partial(body, True))
    pl.when(jnp.logical_not(first))(functools.partial(body, False))


def _ffn_layer(h2, gain, w_in, w_out, final_gain, layer, final_norm, heads=None, w_mix=None,
               mix_layer=None):
    T, D = h2.shape
    tm = FFN_TILE
    tok = pl.BlockSpec((tm, D), lambda i: (i, 0))
    hbm = pl.BlockSpec(memory_space=pl.ANY)
    mixer_proj = heads is not None
    mix_specs = [tok, _resident(w_mix.shape[1:], mix_layer)] if mixer_proj else []
    mix_args = (heads, w_mix) if mixer_proj else ()
    return pl.pallas_call(
        functools.partial(_ffn_kernel, layer=layer, mixer_proj=mixer_proj, final_norm=final_norm),
        grid=(T // tm,),
        in_specs=[tok] + mix_specs + [_resident((1, D), layer), hbm, hbm, _resident((1, D))],
        out_specs=tok,
        out_shape=jax.ShapeDtypeStruct(h2.shape, h2.dtype),
        scratch_shapes=[
            pltpu.VMEM((D, 2 * D_FF), jnp.bfloat16),
            pltpu.VMEM((D_FF, D), jnp.bfloat16),
            pltpu.VMEM((tm, D_FF), jnp.bfloat16),
            pltpu.VMEM((2, 2, D, FF_CHUNK), jnp.float32),
            pltpu.VMEM((2, FF_CHUNK, D), jnp.float32),
            pltpu.SemaphoreType.DMA((2, 3)),
        ],
        compiler_params=_params("arbitrary"),
        name="swiglu" + ("_mix" if mixer_proj else "") + ("_final" if final_norm else ""),
    )(h2, *mix_args, gain, w_in, w_out, final_gain)


def _proj_kernel(x_ref, gain_ref, wk_ref, wt_ref, hn_ref, qt_ref, k_ref, vt_ref, ogt_ref, gt_ref):
    xn = _rms_norm(x_ref[...], gain_ref[...]).astype(jnp.bfloat16)
    k_ref[...] = _dot(xn, wk_ref[...]).astype(k_ref.dtype)
    qg = _dot(wt_ref[0:QG_ROWS, :], xn, _NT)
    qt_ref[...] = (qg[0:QK_W, :] * (DQK ** -0.5)).astype(qt_ref.dtype)
    gt_ref[...] = qg[QK_W:QK_W + N_GATES, :]
    vt_ref[...] = _dot(wt_ref[QG_ROWS:QG_ROWS + V_W, :], xn, _NT).astype(vt_ref.dtype)
    og = jax.nn.sigmoid(_dot(wt_ref[QG_ROWS + V_W:, :], xn, _NT))
    for lo in range(0, og.shape[1], LANE):
        ogt_ref[:, lo:lo + LANE] = (og[:, lo:lo + LANE] * hn_ref[...]).astype(ogt_ref.dtype)


def _proj_layer(h2, gain, w_k, w_t, head_norm_b, layer):
    T, D = h2.shape
    tm = PROJ_TILE
    tok = lambda i: (i, 0)
    lanes = lambda i: (0, i)
    return pl.pallas_call(
        _proj_kernel,
        grid=(T // tm,),
        in_specs=[
            pl.BlockSpec((tm, D), tok),
            _resident((1, D), layer),
            _resident(w_k.shape[1:], layer),
            _resident(w_t.shape[1:], layer),
            _resident(head_norm_b.shape[1:], layer),
        ],
        out_specs=[
            pl.BlockSpec((QK_W, tm), lanes),
            pl.BlockSpec((tm, QK_W), tok),
            pl.BlockSpec((V_W, tm), lanes),
            pl.BlockSpec((V_W, tm), lanes),
            pl.BlockSpec((N_GATES, tm), lanes),
        ],
        out_shape=[
            jax.ShapeDtypeStruct((QK_W, T), jnp.bfloat16),
            jax.ShapeDtypeStruct((T, QK_W), jnp.bfloat16),
            jax.ShapeDtypeStruct((V_W, T), jnp.bfloat16),
            jax.ShapeDtypeStruct((V_W, T), jnp.bfloat16),
            jax.ShapeDtypeStruct((N_GATES, T), jnp.float32),
        ],
        compiler_params=_params("arbitrary"),
        name="mlstm_proj",
    )(h2, gain, w_k, w_t, head_norm_b)


def _segmented_scan(x, op, lane, seg):
    shift = 1
    while shift < seg:
        moved = pltpu.roll(x, shift, 1)
        x = jnp.where((lane % seg) >= shift, op(x, moved), x)
        shift *= 2
    return x


def _gate_kernel(gt_ref, bias_ref, r_ref):
    L = MLSTM_CHUNK
    S = gt_ref.shape[1]
    g = gt_ref[...] + bias_ref[...]
    i_pre = g[0:N_HEADS]
    f_pre = g[N_HEADS:N_GATES]
    logf = jnp.minimum(f_pre, 0.0) - jnp.log1p(jnp.exp(-jnp.abs(f_pre)))
    lane = lax.broadcasted_iota(jnp.int32, (N_HEADS, S), 1)
    b = _segmented_scan(logf, jnp.add, lane, L)
    a = i_pre - b
    amax = _segmented_scan(a, jnp.maximum, lane, L)

    r_ref[0, 0:N_HEADS, :] = a
    r_ref[0, 5 * N_HEADS:N_GATE_ROWS, :] = jnp.zeros((N_GATE_ROWS - 5 * N_HEADS, S), jnp.float32)
    m_prev = jnp.zeros((N_HEADS, 1), jnp.float32)
    for c in range(S // L):
        seg = slice(c * L, (c + 1) * L)
        big_m = jnp.maximum(m_prev, amax[:, seg])
        m_last = big_m[:, L - 1:L]
        r_ref[0, N_HEADS:2 * N_HEADS, seg] = big_m
        r_ref[0, 2 * N_HEADS:3 * N_HEADS, seg] = -(b[:, seg] + big_m)
        r_ref[0, 3 * N_HEADS:4 * N_HEADS, seg] = m_prev - big_m
        r_ref[0, 4 * N_HEADS:5 * N_HEADS, seg] = a[:, seg] - m_last
        m_prev = b[:, (c + 1) * L - 1:(c + 1) * L] + m_last


def _gate_layer(gates_t, bias, B, S, layer):
    return pl.pallas_call(
        _gate_kernel,
        grid=(B,),
        in_specs=[
            pl.BlockSpec((N_GATES, S), lambda b: (0, b)),
            _resident((N_GATES, 1), layer),
        ],
        out_specs=pl.BlockSpec((1, N_GATE_ROWS, S), lambda b: (b, 0, 0)),
        out_shape=jax.ShapeDtypeStruct((B, N_GATE_ROWS, S), jnp.float32),
        compiler_params=_params("arbitrary"),
        name="mlstm_gates",
    )(gates_t, bias)


def _hi_lo_rows(row):
    hi = row.astype(jnp.bfloat16).astype(jnp.float32)
    idx = lax.broadcasted_iota(jnp.int32, (16, row.shape[1]), 0)
    slab = jnp.where(idx == 0, hi, jnp.where(idx == 1, row - hi, 0.0))
    return slab.astype(jnp.bfloat16)


def _chunk_kernel(qt_ref, k_ref, vt_ref, ogt_ref, r_ref, out_ref, c_ref, n_ref, gated_ref):
    L = MLSTM_CHUNK

    @pl.when(pl.program_id(1) == 0)
    def _():
        c_ref[...] = jnp.zeros(c_ref.shape, jnp.float32)
        n_ref[...] = jnp.zeros(n_ref.shape, jnp.float32)

    s_idx = lax.broadcasted_iota(jnp.int32, (L, L), 0)
    t_idx = lax.broadcasted_iota(jnp.int32, (L, L), 1)
    eye = (s_idx == t_idx).astype(jnp.bfloat16)
    H = L // 2
    causal = (lax.broadcasted_iota(jnp.int32, (H, H), 0)
              <= lax.broadcasted_iota(jnp.int32, (H, H), 1))

    tok = lambda c: slice(c * L, (c + 1) * L)
    qk = lambda h: slice(h * DQK, (h + 1) * DQK)
    vv = lambda h: slice(h * DV, (h + 1) * DV)
    units = [(c, h) for c in range(CHUNKS_PER_STEP) for h in range(N_HEADS)]
    rows = [r_ref[0, :, tok(c)] for c in range(CHUNKS_PER_STEP)]
    a_cols = [r[0:2 * N_HEADS, :].T for r in rows]
    gate_row = lambda c, h, kind: rows[c][kind * N_HEADS + h:kind * N_HEADS + h + 1, :]

    st = {u: _dot(k_ref[tok(u[0]), qk(u[1])], qt_ref[qk(u[1]), tok(u[0])]) for u in units}
    for c, h in units:
        a_col = a_cols[c][:, h:h + 1]
        big_m = gate_row(c, h, 1)
        diag = lambda i: jnp.exp(jnp.where(
            causal, a_col[i * H:(i + 1) * H, :] - big_m[:, i * H:(i + 1) * H], -jnp.inf))
        above = jnp.exp(a_col[0:H, :] - big_m[:, H:L])
        top = st[c, h][0:H, :] * jnp.concatenate([diag(0), above], axis=1)
        bottom = jnp.concatenate([jnp.zeros((H, H), jnp.float32), st[c, h][H:L, H:L] * diag(1)], axis=1)
        st[c, h] = jnp.concatenate([top, bottom], axis=0)
    den_intra = {u: jnp.sum(st[u], axis=0, keepdims=True) for u in units}
    num_intra = {u: _dot(vt_ref[vv(u[1]), tok(u[0])], st[u].astype(jnp.bfloat16)) for u in units}

    ct = [c_ref[h] for h in range(N_HEADS)]
    n = [n_ref[h] for h in range(N_HEADS)]
    for c, h in units:
        qt = qt_ref[qk(h), tok(c)]
        k = k_ref[tok(c), qk(h)]
        vt = vt_ref[vv(h), tok(c)]
        inter = jnp.exp(gate_row(c, h, 3))
        w = jnp.exp(gate_row(c, h, 4))

        state = jnp.concatenate([ct[h].astype(jnp.bfloat16), _hi_lo_rows(n[h])], axis=0)
        from_state = _dot(state, qt)
        qn = from_state[DV:DV + 1, :] + from_state[DV + 1:DV + 2, :]
        num = inter * from_state[0:DV, :] + num_intra[c, h]
        den = inter * qn + den_intra[c, h]
        r = 1.0 / jnp.maximum(jnp.abs(den), jnp.exp(gate_row(c, h, 2)))
        ss = jnp.sum(num * num, axis=0, keepdims=True)
        scale = r * lax.rsqrt(r * r * ss * (1.0 / DV) + EPS)
        gated = num * scale * ogt_ref[vv(h), tok(c)].astype(jnp.float32)
        gated_ref[vv(h), tok(c)] = gated.astype(jnp.bfloat16)

        vw = jnp.concatenate([vt * w.astype(jnp.bfloat16), _hi_lo_rows(w)], axis=0)
        upd = _dot(vw, k)
        decay = inter[:, L - 1:L]
        ct[h] = decay * ct[h] + upd[0:DV, :]
        n[h] = decay * n[h] + upd[DV:DV + 1, :] + upd[DV + 1:DV + 2, :]

    for h in range(N_HEADS):
        c_ref[h] = ct[h]
        n_ref[h] = n[h]
    for c in range(CHUNKS_PER_STEP):
        out_ref[tok(c), :] = _dot(eye, gated_ref[:, tok(c)], _NT).astype(out_ref.dtype)


def _chunk_layer(qt, k, vt, ogt, rows, B, S):
    T = k.shape[0]
    step = MLSTM_CHUNK * CHUNKS_PER_STEP
    ns = S // step
    tok = lambda b, c: (b * ns + c, 0)
    lanes = lambda b, c: (0, b * ns + c)
    return pl.pallas_call(
        _chunk_kernel,
        grid=(B, ns),
        in_specs=[
            pl.BlockSpec((QK_W, step), lanes),
            pl.BlockSpec((step, QK_W), tok),
            pl.BlockSpec((V_W, step), lanes),
            pl.BlockSpec((V_W, step), lanes),
            pl.BlockSpec((1, N_GATE_ROWS, step), lambda b, c: (b, 0, c)),
        ],
        out_specs=pl.BlockSpec((step, V_W), tok),
        out_shape=jax.ShapeDtypeStruct((T, V_W), jnp.bfloat16),
        scratch_shapes=[
            pltpu.VMEM((N_HEADS, DV, DQK), jnp.float32),
            pltpu.VMEM((N_HEADS, 1, DQK), jnp.float32),
            pltpu.VMEM((V_W, step), jnp.bfloat16),
        ],
        compiler_params=_params("arbitrary", "arbitrary"),
        name="mlstm_chunk",
    )(qt, k, vt, ogt, rows)


def kernel(x, pool_norm, pool_w, pool_scale, mlstm_norm, mlstm_w_in, mlstm_gate_bias,
           mlstm_head_norm, mlstm_w_out, ffn_norm, ffn_w_in, ffn_w_out, final_norm):
    B, S, D = x.shape
    depth = ffn_norm.shape[0]
    bf16 = jnp.bfloat16
    row = lambda a: a.reshape(a.shape[0], 1, a.shape[1])
    pool_w_b = pool_w.astype(bf16)
    w_k_b = mlstm_w_in[:, :, QK_W:2 * QK_W].astype(bf16)
    w_t_b = jnp.swapaxes(
        jnp.concatenate([mlstm_w_in[:, :, :QK_W], mlstm_w_in[:, :, MAIN_W:],
                         jnp.zeros(mlstm_w_in.shape[:2] + (QG_ROWS - QK_W - N_GATES,), mlstm_w_in.dtype),
                         mlstm_w_in[:, :, 2 * QK_W:MAIN_W]], axis=2),
        1, 2).astype(bf16)
    head_norm_b = jnp.broadcast_to(mlstm_head_norm[:, :, None], mlstm_head_norm.shape + (LANE,))
    gate_bias = mlstm_gate_bias.reshape(-1, N_GATES, 1)
    w_mix_b = mlstm_w_out.astype(bf16)
    fgain = final_norm.reshape(1, D)

    h2 = x.reshape(B * S, D)
    for i in range(depth):
        j = i // 2
        last = i == depth - 1
        if i % 2 == 0:
            h2 = _pool_layer(h2.reshape(B, S, D), row(pool_norm), pool_w_b, row(pool_scale), j)
            h2 = _ffn_layer(h2.reshape(B * S, D), row(ffn_norm), ffn_w_in, ffn_w_out, fgain, i, last)
        else:
            qt, k, vt, ogt, gates_t = _proj_layer(h2, row(mlstm_norm), w_k_b, w_t_b, head_norm_b, j)
            rows = _gate_layer(gates_t, gate_bias, B, S, j)
            heads = _chunk_layer(qt, k, vt, ogt, rows, B, S)
            h2 = _ffn_layer(h2, row(ffn_norm), ffn_w_in, ffn_w_out, fgain, i, last,
                            heads=heads, w_mix=w_mix_b, mix_layer=j)
    return h2.reshape(B, S, D)
```

```python
import functools
import math

import jax
import jax.numpy as jnp
from jax import lax
from jax.experimental import pallas as pl
from jax.experimental.pallas import tpu as pltpu

D_MODEL = 1024
POOL_WINDOWS = (2, 4, 8, 16)
POOL_GROUP_DIM = D_MODEL // len(POOL_WINDOWS)
POOL_HALO = 32

N_HEADS = 4
DV = D_MODEL // N_HEADS
DQK = DV // 2
QK_W = N_HEADS * DQK
V_W = N_HEADS * DV
MAIN_W = 2 * QK_W + 2 * V_W
N_GATES = 2 * N_HEADS
QG_ROWS = QK_W + 16

D_FF = int(math.ceil(8 * D_MODEL / 3 / 256) * 256)
FF_CHUNK = 256
LANE = 128

EPS = 1e-6

POOL_TILE = 1024
FFN_TILE = 512
PROJ_TILE = 1024
MLSTM_CHUNK = 256
CHUNKS_PER_STEP = 4
N_GATE_ROWS = 24

VMEM_LIMIT = 56 * 1024 * 1024

_NT = (((1,), (1,)), ((), ()))


def _params(*semantics):
    return pltpu.CompilerParams(dimension_semantics=semantics,
                                vmem_limit_bytes=VMEM_LIMIT)


def _resident(shape, layer=None):
    zeros = (0,) * len(shape)
    if layer is None:
        return pl.BlockSpec(shape, lambda *_: zeros, pipeline_mode=pl.Buffered(1))
    return pl.BlockSpec((None,) + shape, lambda *_: (layer,) + zeros,
                        pipeline_mode=pl.Buffered(1))


def _rms_norm(x, gain):
    return x * lax.rsqrt(jnp.mean(x * x, axis=-1, keepdims=True) + EPS) * gain


def _dot(a, b, dims=None):
    if dims is None:
        return jnp.dot(a, b, preferred_element_type=jnp.float32)
    return lax.dot_general(a, b, dims, preferred_element_type=jnp.float32)


def _pool_kernel(x_ref, gain_ref, w_ref, scale_ref, o_ref, ext_ref, *stage_refs):
    s = pl.program_id(1)
    ts = x_ref.shape[1]
    rows = POOL_HALO + ts
    gd = POOL_GROUP_DIM
    x = x_ref[0]
    xn = _rms_norm(x, gain_ref[...])

    @pl.when(s == 0)
    def _():
        ext_ref[0:POOL_HALO, :] = jnp.zeros((POOL_HALO, D_MODEL), jnp.float32)

    ext_ref[POOL_HALO:rows, :] = xn

    prev = ext_ref
    for j, cur in enumerate(stage_refs, start=1):
        shift = 2 ** (j - 1)
        lo = 8 * j
        skip = 0 if j == 1 else gd
        cur[lo:rows, :] = prev[lo:rows, skip:] + prev[lo - shift:rows - shift, skip:]
        prev = cur

    pos = (s * ts + 1 + lax.broadcasted_iota(jnp.int32, (ts, 1), 0)).astype(jnp.float32)
    for g, win in enumerate(POOL_WINDOWS):
        cols = slice(g * gd, (g + 1) * gd)
        tot = stage_refs[g][POOL_HALO:rows, 0:gd]
        inv_cnt = 1.0 / jnp.minimum(pos, float(win))
        pooled = tot * inv_cnt - xn[:, cols]
        y = _dot(pooled.astype(jnp.bfloat16), w_ref[g])
        o_ref[0, :, cols] = x[:, cols] + y * scale_ref[:, cols]

    ext_ref[0:POOL_HALO, :] = ext_ref[ts:rows, :]


def _pool_layer(h, gain, w, scale, layer):
    B, S, D = h.shape
    ts = POOL_TILE
    return pl.pallas_call(
        _pool_kernel,
        grid=(B, S // ts),
        in_specs=[
            pl.BlockSpec((1, ts, D), lambda b, s: (b, s, 0)),
            _resident((1, D), layer),
            _resident(w.shape[1:], layer),
            _resident((1, D), layer),
        ],
        out_specs=pl.BlockSpec((1, ts, D), lambda b, s: (b, s, 0)),
        out_shape=jax.ShapeDtypeStruct(h.shape, h.dtype),
        scratch_shapes=[pltpu.VMEM((POOL_HALO + ts, D), jnp.float32)] + [
            pltpu.VMEM((POOL_HALO + ts, D - g * POOL_GROUP_DIM), jnp.float32)
            for g in range(len(POOL_WINDOWS))],
        compiler_params=_params("arbitrary", "arbitrary"),
        name="pool_mixer",
    )(h, gain, w, scale)


def _ffn_kernel(*refs, layer, mixer_proj, final_norm):
    if mixer_proj:
        x_ref, heads_ref, w_mix_ref, *refs = refs
    else:
        x_ref, *refs = refs
    (gain_ref, w_in_hbm, w_out_hbm, fgain_ref, o_ref,
     w_in_ref, w_out_ref, act_ref, stage_in, stage_out, sem) = refs
    n_chunks = D_FF // FF_CHUNK

    def chunk_copies(j, slot):
        lo = j * FF_CHUNK
        return (
            pltpu.make_async_copy(w_in_hbm.at[layer, :, lo:lo + FF_CHUNK],
                                  stage_in.at[slot, 0], sem.at[slot, 0]),
            pltpu.make_async_copy(w_in_hbm.at[layer, :, D_FF + lo:D_FF + lo + FF_CHUNK],
                                  stage_in.at[slot, 1], sem.at[slot, 1]),
            pltpu.make_async_copy(w_out_hbm.at[layer, lo:lo + FF_CHUNK, :],
                                  stage_out.at[slot], sem.at[slot, 2]),
        )

    def body(stream_weights):
        if stream_weights:
            for cp in chunk_copies(0, 0):
                cp.start()
        if mixer_proj:
            x = x_ref[...] + _dot(heads_ref[...], w_mix_ref[...], (((0,), (0,)), ((), ())))
        else:
            x = x_ref[...]
        xn = _rms_norm(x, gain_ref[...]).astype(jnp.bfloat16)
        for j in range(n_chunks):
            lo = j * FF_CHUNK
            if stream_weights:
                slot = j % 2
                if j + 1 < n_chunks:
                    for cp in chunk_copies(j + 1, 1 - slot):
                        cp.start()
                for cp in chunk_copies(j, slot):
                    cp.wait()
                w_in_ref[:, lo:lo + FF_CHUNK] = stage_in[slot, 0].astype(jnp.bfloat16)
                w_in_ref[:, D_FF + lo:D_FF + lo + FF_CHUNK] = stage_in[slot, 1].astype(jnp.bfloat16)
                w_out_ref[lo:lo + FF_CHUNK, :] = stage_out[slot].astype(jnp.bfloat16)
            gate = _dot(xn, w_in_ref[:, lo:lo + FF_CHUNK])
            up = _dot(xn, w_in_ref[:, D_FF + lo:D_FF + lo + FF_CHUNK])
            act_ref[:, lo:lo + FF_CHUNK] = (gate * jax.nn.sigmoid(gate) * up).astype(jnp.bfloat16)
        out = x + _dot(act_ref[...], w_out_ref[...])
        if final_norm:
            out = _rms_norm(out, fgain_ref[...])
        o_ref[...] = out

    first = pl.program_id(0) == 0
    pl.when(first)(functools.partial(body, True))
    pl.when(jnp.logical_not(first))(functools.partial(body, False))


def _ffn_layer(h2, gain, w_in, w_out, final_gain, layer, final_norm, heads=None, w_mix=None,
               mix_layer=None):
    T, D = h2.shape
    tm = FFN_TILE
    tok = pl.BlockSpec((tm, D), lambda i: (i, 0))
    hbm = pl.BlockSpec(memory_space=pl.ANY)
    mixer_proj = heads is not None
    mix_specs = ([pl.BlockSpec((D, tm), lambda i: (0, i)), _resident(w_mix.shape[1:], mix_layer)]
                 if mixer_proj else [])
    mix_args = (heads, w_mix) if mixer_proj else ()
    return pl.pallas_call(
        functools.partial(_ffn_kernel, layer=layer, mixer_proj=mixer_proj, final_norm=final_norm),
        grid=(T // tm,),
        in_specs=[tok] + mix_specs + [_resident((1, D), layer), hbm, hbm, _resident((1, D))],
        out_specs=tok,
        out_shape=jax.ShapeDtypeStruct(h2.shape, h2.dtype),
        scratch_shapes=[
            pltpu.VMEM((D, 2 * D_FF), jnp.bfloat16),
            pltpu.VMEM((D_FF, D), jnp.bfloat16),
            pltpu.VMEM((tm, D_FF), jnp.bfloat16),
            pltpu.VMEM((2, 2, D, FF_CHUNK), jnp.float32),
            pltpu.VMEM((2, FF_CHUNK, D), jnp.float32),
            pltpu.SemaphoreType.DMA((2, 3)),
        ],
        compiler_params=_params("arbitrary"),
        name="swiglu" + ("_mix" if mixer_proj else "") + ("_final" if final_norm else ""),
    )(h2, *mix_args, gain, w_in, w_out, final_gain)


def _proj_kernel(x_ref, gain_ref, wk_ref, wt_ref, hn_ref, qt_ref, k_ref, vt_ref, ogt_ref, gt_ref):
    xn = _rms_norm(x_ref[...], gain_ref[...]).astype(jnp.bfloat16)
    k_ref[...] = _dot(xn, wk_ref[...]).astype(k_ref.dtype)
    qg = _dot(wt_ref[0:QG_ROWS, :], xn, _NT)
    qt_ref[...] = (qg[0:QK_W, :] * (DQK ** -0.5)).astype(qt_ref.dtype)
    gt_ref[...] = qg[QK_W:QK_W + N_GATES, :]
    vt_ref[...] = _dot(wt_ref[QG_ROWS:QG_ROWS + V_W, :], xn, _NT).astype(vt_ref.dtype)
    og = jax.nn.sigmoid(_dot(wt_ref[QG_ROWS + V_W:, :], xn, _NT))
    for lo in range(0, og.shape[1], LANE):
        ogt_ref[:, lo:lo + LANE] = (og[:, lo:lo + LANE] * hn_ref[...]).astype(ogt_ref.dtype)


def _proj_layer(h2, gain, w_k, w_t, head_norm_b, layer):
    T, D = h2.shape
    tm = PROJ_TILE
    tok = lambda i: (i, 0)
    lanes = lambda i: (0, i)
    return pl.pallas_call(
        _proj_kernel,
        grid=(T // tm,),
        in_specs=[
            pl.BlockSpec((tm, D), tok),
            _resident((1, D), layer),
            _resident(w_k.shape[1:], layer),
            _resident(w_t.shape[1:], layer),
            _resident(head_norm_b.shape[1:], layer),
        ],
        out_specs=[
            pl.BlockSpec((QK_W, tm), lanes),
            pl.BlockSpec((tm, QK_W), tok),
            pl.BlockSpec((V_W, tm), lanes),
            pl.BlockSpec((V_W, tm), lanes),
            pl.BlockSpec((N_GATES, tm), lanes),
        ],
        out_shape=[
            jax.ShapeDtypeStruct((QK_W, T), jnp.bfloat16),
            jax.ShapeDtypeStruct((T, QK_W), jnp.bfloat16),
            jax.ShapeDtypeStruct((V_W, T), jnp.bfloat16),
            jax.ShapeDtypeStruct((V_W, T), jnp.bfloat16),
            jax.ShapeDtypeStruct((N_GATES, T), jnp.float32),
        ],
        compiler_params=_params("arbitrary"),
        name="mlstm_proj",
    )(h2, gain, w_k, w_t, head_norm_b)


def _segmented_scan(x, op, lane, seg):
    shift = 1
    while shift < seg:
        moved = pltpu.roll(x, shift, 1)
        x = jnp.where((lane % seg) >= shift, op(x, moved), x)
        shift *= 2
    return x


def _gate_kernel(gt_ref, bias_ref, r_ref):
    L = MLSTM_CHUNK
    B, _, S = r_ref.shape
    g = gt_ref[...] + bias_ref[...]
    per_seq = lambda lo: jnp.concatenate(
        [g[lo:lo + N_HEADS, s * S:(s + 1) * S] for s in range(B)], axis=0)
    i_pre = per_seq(0)
    f_pre = per_seq(N_HEADS)
    logf = jnp.minimum(f_pre, 0.0) - jnp.log1p(jnp.exp(-jnp.abs(f_pre)))
    lane = lax.broadcasted_iota(jnp.int32, (B * N_HEADS, S), 1)
    b = _segmented_scan(logf, jnp.add, lane, L)
    a = i_pre - b
    amax = _segmented_scan(a, jnp.maximum, lane, L)

    def put(kind, lanes, value):
        for s in range(B):
            r_ref[s, kind * N_HEADS:(kind + 1) * N_HEADS, lanes] = value[s * N_HEADS:(s + 1) * N_HEADS]

    put(0, slice(0, S), a)
    r_ref[:, 5 * N_HEADS:N_GATE_ROWS, :] = jnp.zeros((B, N_GATE_ROWS - 5 * N_HEADS, S), jnp.float32)
    m_prev = jnp.zeros((B * N_HEADS, 1), jnp.float32)
    for c in range(S // L):
        seg = slice(c * L, (c + 1) * L)
        big_m = jnp.maximum(m_prev, amax[:, seg])
        m_last = big_m[:, L - 1:L]
        put(1, seg, big_m)
        put(2, seg, -(b[:, seg] + big_m))
        put(3, seg, m_prev - big_m)
        put(4, seg, a[:, seg] - m_last)
        m_prev = b[:, (c + 1) * L - 1:(c + 1) * L] + m_last


def _gate_layer(gates_t, bias, B, S, layer):
    return pl.pallas_call(
        _gate_kernel,
        grid=(1,),
        in_specs=[
            pl.BlockSpec((N_GATES, B * S), lambda i: (0, 0)),
            _resident((N_GATES, 1), layer),
        ],
        out_specs=pl.BlockSpec((B, N_GATE_ROWS, S), lambda i: (0, 0, 0)),
        out_shape=jax.ShapeDtypeStruct((B, N_GATE_ROWS, S), jnp.float32),
        compiler_params=_params("arbitrary"),
        name="mlstm_gates",
    )(gates_t, bias)


def _hi_lo_rows(row):
    hi = row.astype(jnp.bfloat16).astype(jnp.float32)
    idx = lax.broadcasted_iota(jnp.int32, (16, row.shape[1]), 0)
    slab = jnp.where(idx == 0, hi, jnp.where(idx == 1, row - hi, 0.0))
    return slab.astype(jnp.bfloat16)


def _chunk_kernel(qt_ref, k_ref, vt_ref, ogt_ref, r_ref, out_ref, c_ref, n_ref):
    L = MLSTM_CHUNK
    H = L // 2

    @pl.when(pl.program_id(1) == 0)
    def _():
        c_ref[...] = jnp.zeros(c_ref.shape, jnp.float32)
        n_ref[...] = jnp.zeros(n_ref.shape, jnp.float32)

    causal = (lax.broadcasted_iota(jnp.int32, (H, H), 0)
              <= lax.broadcasted_iota(jnp.int32, (H, H), 1))

    tok = lambda c: slice(c * L, (c + 1) * L)
    qk = lambda h: slice(h * DQK, (h + 1) * DQK)
    vv = lambda h: slice(h * DV, (h + 1) * DV)
    units = [(c, h) for c in range(CHUNKS_PER_STEP) for h in range(N_HEADS)]
    rows = [r_ref[0, :, tok(c)] for c in range(CHUNKS_PER_STEP)]
    a_cols = [r[0:2 * N_HEADS, :].T for r in rows]
    gate_row = lambda c, h, kind: rows[c][kind * N_HEADS + h:kind * N_HEADS + h + 1, :]
    ct = [c_ref[h] for h in range(N_HEADS)]
    n = [n_ref[h] for h in range(N_HEADS)]

    def scores(c, h):
        return _dot(k_ref[tok(c), qk(h)], qt_ref[qk(h), tok(c)])

    def weigh(c, h, s):
        a_col = a_cols[c][:, h:h + 1]
        big_m = gate_row(c, h, 1)
        diag = lambda i: jnp.exp(jnp.where(
            causal, a_col[i * H:(i + 1) * H, :] - big_m[:, i * H:(i + 1) * H], -jnp.inf))
        above = jnp.exp(a_col[0:H, :] - big_m[:, H:L])
        top = s[0:H, :] * jnp.concatenate([diag(0), above], axis=1)
        bottom = jnp.concatenate([jnp.zeros((H, H), jnp.float32), s[H:L, H:L] * diag(1)], axis=1)
        st = jnp.concatenate([top, bottom], axis=0)
        den_intra = jnp.sum(st, axis=0, keepdims=True)
        num_intra = _dot(vt_ref[vv(h), tok(c)], st.astype(jnp.bfloat16))
        state = jnp.concatenate([ct[h].astype(jnp.bfloat16), _hi_lo_rows(n[h])], axis=0)
        from_state = _dot(state, qt_ref[qk(h), tok(c)])
        return den_intra, num_intra, from_state

    def finish(c, h, den_intra, num_intra, from_state):
        inter = jnp.exp(gate_row(c, h, 3))
        w = jnp.exp(gate_row(c, h, 4))
        qn = from_state[DV:DV + 1, :] + from_state[DV + 1:DV + 2, :]
        num = inter * from_state[0:DV, :] + num_intra
        den = inter * qn + den_intra
        r = 1.0 / jnp.maximum(jnp.abs(den), jnp.exp(gate_row(c, h, 2)))
        ss = jnp.sum(num * num, axis=0, keepdims=True)
        scale = r * lax.rsqrt(r * r * ss * (1.0 / DV) + EPS)
        gated = num * scale * ogt_ref[vv(h), tok(c)].astype(jnp.float32)
        out_ref[vv(h), tok(c)] = gated.astype(out_ref.dtype)

        vw = jnp.concatenate([vt_ref[vv(h), tok(c)] * w.astype(jnp.bfloat16), _hi_lo_rows(w)], axis=0)
        upd = _dot(vw, k_ref[tok(c), qk(h)])
        decay = inter[:, L - 1:L]
        ct[h] = decay * ct[h] + upd[0:DV, :]
        n[h] = decay * n[h] + upd[DV:DV + 1, :] + upd[DV + 1:DV + 2, :]

    scored = {0: scores(*units[0])}
    if len(units) > 1:
        scored[1] = scores(*units[1])
    weighed = {0: weigh(*units[0], scored.pop(0))}
    for i, u in enumerate(units):
        if i + 2 < len(units):
            scored[i + 2] = scores(*units[i + 2])
        if i + 1 < len(units):
            weighed[i + 1] = weigh(*units[i + 1], scored.pop(i + 1))
        finish(*u, *weighed.pop(i))

    for h in range(N_HEADS):
        c_ref[h] = ct[h]
        n_ref[h] = n[h]


def _chunk_layer(qt, k, vt, ogt, rows, B, S):
    T = k.shape[0]
    step = MLSTM_CHUNK * CHUNKS_PER_STEP
    ns = S // step
    tok = lambda b, c: (b * ns + c, 0)
    lanes = lambda b, c: (0, b * ns + c)
    return pl.pallas_call(
        _chunk_kernel,
        grid=(B, ns),
        in_specs=[
            pl.BlockSpec((QK_W, step), lanes),
            pl.BlockSpec((step, QK_W), tok),
            pl.BlockSpec((V_W, step), lanes),
            pl.BlockSpec((V_W, step), lanes),
            pl.BlockSpec((1, N_GATE_ROWS, step), lambda b, c: (b, 0, c)),
        ],
        out_specs=pl.BlockSpec((V_W, step), lanes),
        out_shape=jax.ShapeDtypeStruct((V_W, T), jnp.bfloat16),
        scratch_shapes=[
            pltpu.VMEM((N_HEADS, DV, DQK), jnp.float32),
            pltpu.VMEM((N_HEADS, 1, DQK), jnp.float32),
        ],
        compiler_params=_params("arbitrary", "arbitrary"),
        name="mlstm_chunk",
    )(qt, k, vt, ogt, rows)


def kernel(x, pool_norm, pool_w, pool_scale, mlstm_norm, mlstm_w_in, mlstm_gate_bias,
           mlstm_head_norm, mlstm_w_out, ffn_norm, ffn_w_in, ffn_w_out, final_norm):
    B, S, D = x.shape
    depth = ffn_norm.shape[0]
    bf16 = jnp.bfloat16
    row = lambda a: a.reshape(a.shape[0], 1, a.shape[1])
    pool_w_b = pool_w.astype(bf16)
    w_k_b = mlstm_w_in[:, :, QK_W:2 * QK_W].astype(bf16)
    w_t_b = jnp.swapaxes(
        jnp.concatenate([mlstm_w_in[:, :, :QK_W], mlstm_w_in[:, :, MAIN_W:],
                         jnp.zeros(mlstm_w_in.shape[:2] + (QG_ROWS - QK_W - N_GATES,), mlstm_w_in.dtype),
                         mlstm_w_in[:, :, 2 * QK_W:MAIN_W]], axis=2),
        1, 2).astype(bf16)
    head_norm_b = jnp.broadcast_to(mlstm_head_norm[:, :, None], mlstm_head_norm.shape + (LANE,))
    gate_bias = mlstm_gate_bias.reshape(-1, N_GATES, 1)
    w_mix_b = mlstm_w_out.astype(bf16)
    fgain = final_norm.reshape(1, D)

    h2 = x.reshape(B * S, D)
    for i in range(depth):
        j = i // 2
        last = i == depth - 1
        if i % 2 == 0:
            h2 = _pool_layer(h2.reshape(B, S, D), row(pool_norm), pool_w_b, row(pool_scale), j)
            h2 = _ffn_layer(h2.reshape(B * S, D), row(ffn_norm), ffn_w_in, ffn_w_out, fgain, i, last)
        else:
            qt, k, vt, ogt, gates_t = _proj_layer(h2, row(mlstm_norm), w_k_b, w_t_b, head_norm_b, j)
            rows = _gate_layer(gates_t, gate_bias, B, S, j)
            heads = _chunk_layer(qt, k, vt, ogt, rows, B, S)
            h2 = _ffn_layer(h2, row(ffn_norm), ffn_w_in, ffn_w_out, fgain, i, last,
                            heads=heads, w_mix=w_mix_b, mix_layer=j)
    return h2.reshape(B, S, D)
```

```python
import functools
import math

import jax
import jax.numpy as jnp
from jax import lax
from jax.experimental import pallas as pl
from jax.experimental.pallas import tpu as pltpu

D_MODEL = 1024
POOL_WINDOWS = (2, 4, 8, 16)
POOL_GROUP_DIM = D_MODEL // len(POOL_WINDOWS)
POOL_HALO = 32

N_HEADS = 4
DV = D_MODEL // N_HEADS
DQK = DV // 2
QK_W = N_HEADS * DQK
V_W = N_HEADS * DV
MAIN_W = 2 * QK_W + 2 * V_W
N_GATES = 2 * N_HEADS
QG_ROWS = QK_W + 16

D_FF = int(math.ceil(8 * D_MODEL / 3 / 256) * 256)
FF_CHUNK = 256
LANE = 128

EPS = 1e-6

POOL_TILE = 1024
FFN_TILE = 512
PROJ_TILE = 1024
MLSTM_CHUNK = 256
CHUNKS_PER_STEP = 4
N_GATE_ROWS = 24

VMEM_LIMIT = 56 * 1024 * 1024

_NT = (((1,), (1,)), ((), ()))


def _params(*semantics):
    return pltpu.CompilerParams(dimension_semantics=semantics,
                                vmem_limit_bytes=VMEM_LIMIT)


def _resident(shape, layer=None):
    zeros = (0,) * len(shape)
    if layer is None:
        return pl.BlockSpec(shape, lambda *_: zeros, pipeline_mode=pl.Buffered(1))
    return pl.BlockSpec((None,) + shape, lambda *_: (layer,) + zeros,
                        pipeline_mode=pl.Buffered(1))


def _rms_norm(x, gain):
    return x * lax.rsqrt(jnp.mean(x * x, axis=-1, keepdims=True) + EPS) * gain


def _dot(a, b, dims=None):
    if dims is None:
        return jnp.dot(a, b, preferred_element_type=jnp.float32)
    return lax.dot_general(a, b, dims, preferred_element_type=jnp.float32)


def _pool_kernel(x_ref, gain_ref, w_ref, scale_ref, o_ref, ext_ref, *stage_refs):
    s = pl.program_id(1)
    ts = x_ref.shape[1]
    rows = POOL_HALO + ts
    gd = POOL_GROUP_DIM
    x = x_ref[0]
    xn = _rms_norm(x, gain_ref[...])

    @pl.when(s == 0)
    def _():
        ext_ref[0:POOL_HALO, :] = jnp.zeros((POOL_HALO, D_MODEL), jnp.float32)

    ext_ref[POOL_HALO:rows, :] = xn

    prev = ext_ref
    for j, cur in enumerate(stage_refs, start=1):
        shift = 2 ** (j - 1)
        lo = 8 * j
        skip = 0 if j == 1 else gd
        cur[lo:rows, :] = prev[lo:rows, skip:] + prev[lo - shift:rows - shift, skip:]
        prev = cur

    pos = (s * ts + 1 + lax.broadcasted_iota(jnp.int32, (ts, 1), 0)).astype(jnp.float32)
    for g, win in enumerate(POOL_WINDOWS):
        cols = slice(g * gd, (g + 1) * gd)
        tot = stage_refs[g][POOL_HALO:rows, 0:gd]
        inv_cnt = 1.0 / jnp.minimum(pos, float(win))
        pooled = tot * inv_cnt - xn[:, cols]
        y = _dot(pooled.astype(jnp.bfloat16), w_ref[g])
        o_ref[0, :, cols] = x[:, cols] + y * scale_ref[:, cols]

    ext_ref[0:POOL_HALO, :] = ext_ref[ts:rows, :]


def _pool_layer(h, gain, w, scale, layer):
    B, S, D = h.shape
    ts = POOL_TILE
    return pl.pallas_call(
        _pool_kernel,
        grid=(B, S // ts),
        in_specs=[
            pl.BlockSpec((1, ts, D), lambda b, s: (b, s, 0)),
            _resident((1, D), layer),
            _resident(w.shape[1:], layer),
            _resident((1, D), layer),
        ],
        out_specs=pl.BlockSpec((1, ts, D), lambda b, s: (b, s, 0)),
        out_shape=jax.ShapeDtypeStruct(h.shape, h.dtype),
        scratch_shapes=[pltpu.VMEM((POOL_HALO + ts, D), jnp.float32)] + [
            pltpu.VMEM((POOL_HALO + ts, D - g * POOL_GROUP_DIM), jnp.float32)
            for g in range(len(POOL_WINDOWS))],
        compiler_params=_params("arbitrary", "arbitrary"),
        name="pool_mixer",
    )(h, gain, w, scale)


def _ffn_kernel(*refs, layer, mixer_proj, final_norm):
    if mixer_proj:
        x_ref, heads_ref, w_mix_ref, *refs = refs
    else:
        x_ref, *refs = refs
    (gain_ref, w_in_hbm, w_out_hbm, fgain_ref, o_ref,
     w_in_ref, w_out_ref, act_ref, stage_in, stage_out, sem) = refs
    n_chunks = D_FF // FF_CHUNK

    def chunk_copies(j, slot):
        lo = j * FF_CHUNK
        return (
            pltpu.make_async_copy(w_in_hbm.at[layer, :, lo:lo + FF_CHUNK],
                                  stage_in.at[slot, 0], sem.at[slot, 0]),
            pltpu.make_async_copy(w_in_hbm.at[layer, :, D_FF + lo:D_FF + lo + FF_CHUNK],
                                  stage_in.at[slot, 1], sem.at[slot, 1]),
            pltpu.make_async_copy(w_out_hbm.at[layer, lo:lo + FF_CHUNK, :],
                                  stage_out.at[slot], sem.at[slot, 2]),
        )

    def body(stream_weights):
        if stream_weights:
            for cp in chunk_copies(0, 0):
                cp.start()
        if mixer_proj:
            x = x_ref[...] + _dot(heads_ref[...], w_mix_ref[...], (((0,), (0,)), ((), ())))
        else:
            x = x_ref[...]
        xn = _rms_norm(x, gain_ref[...]).astype(jnp.bfloat16)
        for j in range(n_chunks):
            lo = j * FF_CHUNK
            if stream_weights:
                slot = j % 2
                if j + 1 < n_chunks:
                    for cp in chunk_copies(j + 1, 1 - slot):
                        cp.start()
                for cp in chunk_copies(j, slot):
                    cp.wait()
                w_in_ref[:, lo:lo + FF_CHUNK] = stage_in[slot, 0].astype(jnp.bfloat16)
                w_in_ref[:, D_FF + lo:D_FF + lo + FF_CHUNK] = stage_in[slot, 1].astype(jnp.bfloat16)
                w_out_ref[lo:lo + FF_CHUNK, :] = stage_out[slot].astype(jnp.bfloat16)
            gate = _dot(xn, w_in_ref[:, lo:lo + FF_CHUNK])
            up = _dot(xn, w_in_ref[:, D_FF + lo:D_FF + lo + FF_CHUNK])
            act_ref[:, lo:lo + FF_CHUNK] = (gate * jax.nn.sigmoid(gate) * up).astype(jnp.bfloat16)
        out = x + _dot(act_ref[...], w_out_ref[...])
        if final_norm:
            out = _rms_norm(out, fgain_ref[...])
        o_ref[...] = out

    first = pl.program_id(0) == 0
    pl.when(first)(functools.partial(body, True))
    pl.when(jnp.logical_not(first))(functools.partial(body, False))


def _ffn_layer(h2, gain, w_in, w_out, final_gain, layer, final_norm, heads=None, w_mix=None,
               mix_layer=None):
    T, D = h2.shape
    tm = FFN_TILE
    tok = pl.BlockSpec((tm, D), lambda i: (i, 0))
    hbm = pl.BlockSpec(memory_space=pl.ANY)
    mixer_proj = heads is not None
    mix_specs = ([pl.BlockSpec((None, D, tm), lambda i: (i, 0, 0)), _resident(w_mix.shape[1:], mix_layer)]
                 if mixer_proj else [])
    mix_args = (heads, w_mix) if mixer_proj else ()
    return pl.pallas_call(
        functools.partial(_ffn_kernel, layer=layer, mixer_proj=mixer_proj, final_norm=final_norm),
        grid=(T // tm,),
        in_specs=[tok] + mix_specs + [_resident((1, D), layer), hbm, hbm, _resident((1, D))],
        out_specs=tok,
        out_shape=jax.ShapeDtypeStruct(h2.shape, h2.dtype),
        scratch_shapes=[
            pltpu.VMEM((D, 2 * D_FF), jnp.bfloat16),
            pltpu.VMEM((D_FF, D), jnp.bfloat16),
            pltpu.VMEM((tm, D_FF), jnp.bfloat16),
            pltpu.VMEM((2, 2, D, FF_CHUNK), jnp.float32),
            pltpu.VMEM((2, FF_CHUNK, D), jnp.float32),
            pltpu.SemaphoreType.DMA((2, 3)),
        ],
        compiler_params=_params("arbitrary"),
        name="swiglu" + ("_mix" if mixer_proj else "") + ("_final" if final_norm else ""),
    )(h2, *mix_args, gain, w_in, w_out, final_gain)


def _proj_kernel(x_ref, gain_ref, wk_ref, wt_ref, hn_ref, qt_ref, k_ref, vt_ref, ogt_ref, gt_ref):
    xn = _rms_norm(x_ref[...], gain_ref[...]).astype(jnp.bfloat16)
    k_ref[...] = _dot(xn, wk_ref[...]).astype(k_ref.dtype)
    qg = _dot(wt_ref[0:QG_ROWS, :], xn, _NT)
    qt_ref[...] = (qg[0:QK_W, :] * (DQK ** -0.5)).astype(qt_ref.dtype)
    gt_ref[...] = qg[QK_W:QK_W + N_GATES, :]
    vt_ref[...] = _dot(wt_ref[QG_ROWS:QG_ROWS + V_W, :], xn, _NT).astype(vt_ref.dtype)
    og = jax.nn.sigmoid(_dot(wt_ref[QG_ROWS + V_W:, :], xn, _NT))
    for lo in range(0, og.shape[1], LANE):
        ogt_ref[:, lo:lo + LANE] = (og[:, lo:lo + LANE] * hn_ref[...]).astype(ogt_ref.dtype)


def _proj_layer(h2, gain, w_k, w_t, head_norm_b, layer):
    T, D = h2.shape
    tm = PROJ_TILE
    tok = lambda i: (i, 0)
    lanes = lambda i: (0, i)
    tile = lambda i: (i, 0, 0)
    return pl.pallas_call(
        _proj_kernel,
        grid=(T // tm,),
        in_specs=[
            pl.BlockSpec((tm, D), tok),
            _resident((1, D), layer),
            _resident(w_k.shape[1:], layer),
            _resident(w_t.shape[1:], layer),
            _resident(head_norm_b.shape[1:], layer),
        ],
        out_specs=[
            pl.BlockSpec((None, QK_W, tm), tile),
            pl.BlockSpec((tm, QK_W), tok),
            pl.BlockSpec((None, V_W, tm), tile),
            pl.BlockSpec((None, V_W, tm), tile),
            pl.BlockSpec((N_GATES, tm), lanes),
        ],
        out_shape=[
            jax.ShapeDtypeStruct((T // tm, QK_W, tm), jnp.bfloat16),
            jax.ShapeDtypeStruct((T, QK_W), jnp.bfloat16),
            jax.ShapeDtypeStruct((T // tm, V_W, tm), jnp.bfloat16),
            jax.ShapeDtypeStruct((T // tm, V_W, tm), jnp.bfloat16),
            jax.ShapeDtypeStruct((N_GATES, T), jnp.float32),
        ],
        compiler_params=_params("arbitrary"),
        name="mlstm_proj",
    )(h2, gain, w_k, w_t, head_norm_b)


def _segmented_scan(x, op, lane, seg):
    shift = 1
    while shift < seg:
        moved = pltpu.roll(x, shift, 1)
        x = jnp.where((lane % seg) >= shift, op(x, moved), x)
        shift *= 2
    return x


def _gate_kernel(gt_ref, bias_ref, r_ref):
    L = MLSTM_CHUNK
    B, _, S = r_ref.shape
    g = gt_ref[...] + bias_ref[...]
    per_seq = lambda lo: jnp.concatenate(
        [g[lo:lo + N_HEADS, s * S:(s + 1) * S] for s in range(B)], axis=0)
    i_pre = per_seq(0)
    f_pre = per_seq(N_HEADS)
    logf = jnp.minimum(f_pre, 0.0) - jnp.log1p(jnp.exp(-jnp.abs(f_pre)))
    lane = lax.broadcasted_iota(jnp.int32, (B * N_HEADS, S), 1)
    b = _segmented_scan(logf, jnp.add, lane, L)
    a = i_pre - b
    amax = _segmented_scan(a, jnp.maximum, lane, L)

    def put(kind, lanes, value):
        for s in range(B):
            r_ref[s, kind * N_HEADS:(kind + 1) * N_HEADS, lanes] = value[s * N_HEADS:(s + 1) * N_HEADS]

    put(0, slice(0, S), a)
    r_ref[:, 5 * N_HEADS:N_GATE_ROWS, :] = jnp.zeros((B, N_GATE_ROWS - 5 * N_HEADS, S), jnp.float32)
    m_prev = jnp.zeros((B * N_HEADS, 1), jnp.float32)
    for c in range(S // L):
        seg = slice(c * L, (c + 1) * L)
        big_m = jnp.maximum(m_prev, amax[:, seg])
        m_last = big_m[:, L - 1:L]
        put(1, seg, big_m)
        put(2, seg, -(b[:, seg] + big_m))
        put(3, seg, m_prev - big_m)
        put(4, seg, a[:, seg] - m_last)
        m_prev = b[:, (c + 1) * L - 1:(c + 1) * L] + m_last


def _gate_layer(gates_t, bias, B, S, layer):
    return pl.pallas_call(
        _gate_kernel,
        grid=(1,),
        in_specs=[
            pl.BlockSpec((N_GATES, B * S), lambda i: (0, 0)),
            _resident((N_GATES, 1), layer),
        ],
        out_specs=pl.BlockSpec((B, N_GATE_ROWS, S), lambda i: (0, 0, 0)),
        out_shape=jax.ShapeDtypeStruct((B, N_GATE_ROWS, S), jnp.float32),
        compiler_params=_params("arbitrary"),
        name="mlstm_gates",
    )(gates_t, bias)


def _hi_lo_rows(row):
    hi = row.astype(jnp.bfloat16).astype(jnp.float32)
    idx = lax.broadcasted_iota(jnp.int32, (16, row.shape[1]), 0)
    slab = jnp.where(idx == 0, hi, jnp.where(idx == 1, row - hi, 0.0))
    return slab.astype(jnp.bfloat16)


def _chunk_kernel(qt_ref, k_ref, vt_ref, ogt_ref, r_ref, out_ref, c_ref, n_ref):
    L = MLSTM_CHUNK
    H = L // 2

    @pl.when(pl.program_id(1) == 0)
    def _():
        c_ref[...] = jnp.zeros(c_ref.shape, jnp.float32)
        n_ref[...] = jnp.zeros(n_ref.shape, jnp.float32)

    causal = (lax.broadcasted_iota(jnp.int32, (H, H), 0)
              <= lax.broadcasted_iota(jnp.int32, (H, H), 1))

    tok = lambda c: slice(c * L, (c + 1) * L)
    qk = lambda h: slice(h * DQK, (h + 1) * DQK)
    vv = lambda h: slice(h * DV, (h + 1) * DV)
    units = [(c, h) for c in range(CHUNKS_PER_STEP) for h in range(N_HEADS)]
    rows = [r_ref[0, :, tok(c)] for c in range(CHUNKS_PER_STEP)]
    a_cols = [r[0:2 * N_HEADS, :].T for r in rows]
    gate_row = lambda c, h, kind: rows[c][kind * N_HEADS + h:kind * N_HEADS + h + 1, :]
    ct = [c_ref[h] for h in range(N_HEADS)]
    n = [n_ref[h] for h in range(N_HEADS)]

    def scores(c, h):
        return _dot(k_ref[tok(c), qk(h)], qt_ref[qk(h), tok(c)])

    def weigh(c, h, s):
        a_col = a_cols[c][:, h:h + 1]
        big_m = gate_row(c, h, 1)
        diag = lambda i: jnp.exp(jnp.where(
            causal, a_col[i * H:(i + 1) * H, :] - big_m[:, i * H:(i + 1) * H], -jnp.inf))
        above = jnp.exp(a_col[0:H, :] - big_m[:, H:L])
        top = s[0:H, :] * jnp.concatenate([diag(0), above], axis=1)
        bottom = jnp.concatenate([jnp.zeros((H, H), jnp.float32), s[H:L, H:L] * diag(1)], axis=1)
        st = jnp.concatenate([top, bottom], axis=0)
        den_intra = jnp.sum(st, axis=0, keepdims=True)
        num_intra = _dot(vt_ref[vv(h), tok(c)], st.astype(jnp.bfloat16))
        state = jnp.concatenate([ct[h].astype(jnp.bfloat16), _hi_lo_rows(n[h])], axis=0)
        from_state = _dot(state, qt_ref[qk(h), tok(c)])
        return den_intra, num_intra, from_state

    def finish(c, h, den_intra, num_intra, from_state):
        inter = jnp.exp(gate_row(c, h, 3))
        w = jnp.exp(gate_row(c, h, 4))
        qn = from_state[DV:DV + 1, :] + from_state[DV + 1:DV + 2, :]
        num = inter * from_state[0:DV, :] + num_intra
        den = inter * qn + den_intra
        r = 1.0 / jnp.maximum(jnp.abs(den), jnp.exp(gate_row(c, h, 2)))
        ss = jnp.sum(num * num, axis=0, keepdims=True)
        scale = r * lax.rsqrt(r * r * ss * (1.0 / DV) + EPS)
        gated = num * scale * ogt_ref[vv(h), tok(c)].astype(jnp.float32)
        lane0 = (c * L) % FFN_TILE
        out_ref[(c * L) // FFN_TILE, vv(h), lane0:lane0 + L] = gated.astype(out_ref.dtype)

        vw = jnp.concatenate([vt_ref[vv(h), tok(c)] * w.astype(jnp.bfloat16), _hi_lo_rows(w)], axis=0)
        upd = _dot(vw, k_ref[tok(c), qk(h)])
        decay = inter[:, L - 1:L]
        ct[h] = decay * ct[h] + upd[0:DV, :]
        n[h] = decay * n[h] + upd[DV:DV + 1, :] + upd[DV + 1:DV + 2, :]

    scored = {0: scores(*units[0])}
    if len(units) > 1:
        scored[1] = scores(*units[1])
    weighed = {0: weigh(*units[0], scored.pop(0))}
    for i, u in enumerate(units):
        if i + 2 < len(units):
            scored[i + 2] = scores(*units[i + 2])
        if i + 1 < len(units):
            weighed[i + 1] = weigh(*units[i + 1], scored.pop(i + 1))
        finish(*u, *weighed.pop(i))

    for h in range(N_HEADS):
        c_ref[h] = ct[h]
        n_ref[h] = n[h]


def _chunk_layer(qt, k, vt, ogt, rows, B, S):
    T = k.shape[0]
    step = MLSTM_CHUNK * CHUNKS_PER_STEP
    assert step == PROJ_TILE and step % FFN_TILE == 0
    ns = S // step
    tok = lambda b, c: (b * ns + c, 0)
    tile = lambda b, c: (b * ns + c, 0, 0)
    return pl.pallas_call(
        _chunk_kernel,
        grid=(B, ns),
        in_specs=[
            pl.BlockSpec((None, QK_W, step), tile),
            pl.BlockSpec((step, QK_W), tok),
            pl.BlockSpec((None, V_W, step), tile),
            pl.BlockSpec((None, V_W, step), tile),
            pl.BlockSpec((1, N_GATE_ROWS, step), lambda b, c: (b, 0, c)),
        ],
        out_specs=pl.BlockSpec((step // FFN_TILE, V_W, FFN_TILE), tile),
        out_shape=jax.ShapeDtypeStruct((T // FFN_TILE, V_W, FFN_TILE), jnp.bfloat16),
        scratch_shapes=[
            pltpu.VMEM((N_HEADS, DV, DQK), jnp.float32),
            pltpu.VMEM((N_HEADS, 1, DQK), jnp.float32),
        ],
        compiler_params=_params("arbitrary", "arbitrary"),
        name="mlstm_chunk",
    )(qt, k, vt, ogt, rows)


def kernel(x, pool_norm, pool_w, pool_scale, mlstm_norm, mlstm_w_in, mlstm_gate_bias,
           mlstm_head_norm, mlstm_w_out, ffn_norm, ffn_w_in, ffn_w_out, final_norm):
    B, S, D = x.shape
    depth = ffn_norm.shape[0]
    bf16 = jnp.bfloat16
    row = lambda a: a.reshape(a.shape[0], 1, a.shape[1])
    pool_w_b = pool_w.astype(bf16)
    w_k_b = mlstm_w_in[:, :, QK_W:2 * QK_W].astype(bf16)
    w_t_b = jnp.swapaxes(
        jnp.concatenate([mlstm_w_in[:, :, :QK_W], mlstm_w_in[:, :, MAIN_W:],
                         jnp.zeros(mlstm_w_in.shape[:2] + (QG_ROWS - QK_W - N_GATES,), mlstm_w_in.dtype),
                         mlstm_w_in[:, :, 2 * QK_W:MAIN_W]], axis=2),
        1, 2).astype(bf16)
    head_norm_b = jnp.broadcast_to(mlstm_head_norm[:, :, None], mlstm_head_norm.shape + (LANE,))
    gate_bias = mlstm_gate_bias.reshape(-1, N_GATES, 1)
    w_mix_b = mlstm_w_out.astype(bf16)
    fgain = final_norm.reshape(1, D)

    h2 = x.reshape(B * S, D)
    for i in range(depth):
        j = i // 2
        last = i == depth - 1
        if i % 2 == 0:
            h2 = _pool_layer(h2.reshape(B, S, D), row(pool_norm), pool_w_b, row(pool_scale), j)
            h2 = _ffn_layer(h2.reshape(B * S, D), row(ffn_norm), ffn_w_in, ffn_w_out, fgain, i, last)
        else:
            qt, k, vt, ogt, gates_t = _proj_layer(h2, row(mlstm_norm), w_k_b, w_t_b, head_norm_b, j)
            rows = _gate_layer(gates_t, gate_bias, B, S, j)
            heads = _chunk_layer(qt, k, vt, ogt, rows, B, S)
            h2 = _ffn_layer(h2, row(ffn_norm), ffn_w_in, ffn_w_out, fgain, i, last,
                            heads=heads, w_mix=w_mix_b, mix_layer=j)
    return h2.reshape(B, S, D)
```

```python
import functools
import math

import jax
import jax.numpy as jnp
from jax import lax
from jax.experimental import pallas as pl
from jax.experimental.pallas import tpu as pltpu

D_MODEL = 1024
POOL_WINDOWS = (2, 4, 8, 16)
POOL_GROUP_DIM = D_MODEL // len(POOL_WINDOWS)
POOL_HALO = 32

N_HEADS = 4
DV = D_MODEL // N_HEADS
DQK = DV // 2
QK_W = N_HEADS * DQK
V_W = N_HEADS * DV
MAIN_W = 2 * QK_W + 2 * V_W
N_GATES = 2 * N_HEADS
QG_ROWS = QK_W + 16

D_FF = int(math.ceil(8 * D_MODEL / 3 / 256) * 256)
FF_CHUNK = 256
LANE = 128

EPS = 1e-6

POOL_TILE = 1024
FFN_TILE = 512
PROJ_TILE = 1024
MLSTM_CHUNK = 256
CHUNKS_PER_STEP = 4
N_GATE_ROWS = 24

VMEM_LIMIT = 56 * 1024 * 1024

_NT = (((1,), (1,)), ((), ()))


def _params(*semantics):
    return pltpu.CompilerParams(dimension_semantics=semantics,
                                vmem_limit_bytes=VMEM_LIMIT)


def _resident(shape, layer=None):
    zeros = (0,) * len(shape)
    if layer is None:
        return pl.BlockSpec(shape, lambda *_: zeros, pipeline_mode=pl.Buffered(1))
    return pl.BlockSpec((None,) + shape, lambda *_: (layer,) + zeros,
                        pipeline_mode=pl.Buffered(1))


def _rms_norm(x, gain):
    return x * lax.rsqrt(jnp.mean(x * x, axis=-1, keepdims=True) + EPS) * gain


def _dot(a, b, dims=None):
    if dims is None:
        return jnp.dot(a, b, preferred_element_type=jnp.float32)
    return lax.dot_general(a, b, dims, preferred_element_type=jnp.float32)


def _pool_kernel(x_ref, gain_ref, w_ref, scale_ref, o_ref, ext_ref, *stage_refs):
    s = pl.program_id(1)
    ts = x_ref.shape[1]
    rows = POOL_HALO + ts
    gd = POOL_GROUP_DIM
    x = x_ref[0]
    xn = _rms_norm(x, gain_ref[...])

    @pl.when(s == 0)
    def _():
        ext_ref[0:POOL_HALO, :] = jnp.zeros((POOL_HALO, D_MODEL), jnp.float32)

    ext_ref[POOL_HALO:rows, :] = xn

    prev = ext_ref
    for j, cur in enumerate(stage_refs, start=1):
        shift = 2 ** (j - 1)
        lo = 8 * j
        skip = 0 if j == 1 else gd
        cur[lo:rows, :] = prev[lo:rows, skip:] + prev[lo - shift:rows - shift, skip:]
        prev = cur

    pos = (s * ts + 1 + lax.broadcasted_iota(jnp.int32, (ts, 1), 0)).astype(jnp.float32)
    for g, win in enumerate(POOL_WINDOWS):
        cols = slice(g * gd, (g + 1) * gd)
        tot = stage_refs[g][POOL_HALO:rows, 0:gd]
        inv_cnt = 1.0 / jnp.minimum(pos, float(win))
        pooled = tot * inv_cnt - xn[:, cols]
        y = _dot(pooled.astype(jnp.bfloat16), w_ref[g])
        o_ref[0, :, cols] = x[:, cols] + y * scale_ref[:, cols]

    ext_ref[0:POOL_HALO, :] = ext_ref[ts:rows, :]


def _pool_layer(h, gain, w, scale, layer):
    B, S, D = h.shape
    ts = POOL_TILE
    return pl.pallas_call(
        _pool_kernel,
        grid=(B, S // ts),
        in_specs=[
            pl.BlockSpec((1, ts, D), lambda b, s: (b, s, 0)),
            _resident((1, D), layer),
            _resident(w.shape[1:], layer),
            _resident((1, D), layer),
        ],
        out_specs=pl.BlockSpec((1, ts, D), lambda b, s: (b, s, 0)),
        out_shape=jax.ShapeDtypeStruct(h.shape, h.dtype),
        scratch_shapes=[pltpu.VMEM((POOL_HALO + ts, D), jnp.float32)] + [
            pltpu.VMEM((POOL_HALO + ts, D - g * POOL_GROUP_DIM), jnp.float32)
            for g in range(len(POOL_WINDOWS))],
        compiler_params=_params("arbitrary", "arbitrary"),
        name="pool_mixer",
    )(h, gain, w, scale)


def _ffn_kernel(*refs, layer, mixer_proj, final_norm):
    if mixer_proj:
        x_ref, heads_ref, w_mix_ref, *refs = refs
    else:
        x_ref, *refs = refs
    (gain_ref, w_in_hbm, w_out_hbm, fgain_ref, o_ref,
     w_in_ref, w_out_ref, act_ref, stage_in, stage_out, sem) = refs
    n_chunks = D_FF // FF_CHUNK

    def chunk_copies(j, slot):
        lo = j * FF_CHUNK
        return (
            pltpu.make_async_copy(w_in_hbm.at[layer, :, lo:lo + FF_CHUNK],
                                  stage_in.at[slot, 0], sem.at[slot, 0]),
            pltpu.make_async_copy(w_in_hbm.at[layer, :, D_FF + lo:D_FF + lo + FF_CHUNK],
                                  stage_in.at[slot, 1], sem.at[slot, 1]),
            pltpu.make_async_copy(w_out_hbm.at[layer, lo:lo + FF_CHUNK, :],
                                  stage_out.at[slot], sem.at[slot, 2]),
        )

    def body(stream_weights):
        if stream_weights:
            for cp in chunk_copies(0, 0):
                cp.start()
        if mixer_proj:
            x = x_ref[...] + _dot(heads_ref[...], w_mix_ref[...], (((0,), (0,)), ((), ())))
        else:
            x = x_ref[...]
        xb = x.astype(jnp.bfloat16)
        rstd = lax.rsqrt(jnp.mean(x * x, axis=-1, keepdims=True) + EPS)
        for j in range(n_chunks):
            lo = j * FF_CHUNK
            if stream_weights:
                slot = j % 2
                if j + 1 < n_chunks:
                    for cp in chunk_copies(j + 1, 1 - slot):
                        cp.start()
                for cp in chunk_copies(j, slot):
                    cp.wait()
                for part, col in ((0, lo), (1, D_FF + lo)):
                    for sub in range(0, FF_CHUNK, LANE):
                        w_in_ref[:, col + sub:col + sub + LANE] = (
                            stage_in[slot, part, :, sub:sub + LANE] * gain_ref[...]).astype(jnp.bfloat16)
                w_out_ref[lo:lo + FF_CHUNK, :] = stage_out[slot].astype(jnp.bfloat16)
            gate = _dot(xb, w_in_ref[:, lo:lo + FF_CHUNK]) * rstd
            up = _dot(xb, w_in_ref[:, D_FF + lo:D_FF + lo + FF_CHUNK]) * rstd
            act_ref[:, lo:lo + FF_CHUNK] = (gate * jax.nn.sigmoid(gate) * up).astype(jnp.bfloat16)
        out = x + _dot(act_ref[...], w_out_ref[...])
        if final_norm:
            out = _rms_norm(out, fgain_ref[...])
        o_ref[...] = out

    first = pl.program_id(0) == 0
    pl.when(first)(functools.partial(body, True))
    pl.when(jnp.logical_not(first))(functools.partial(body, False))


def _ffn_layer(h2, gain, w_in, w_out, final_gain, layer, final_norm, heads=None, w_mix=None,
               mix_layer=None):
    T, D = h2.shape
    tm = FFN_TILE
    tok = pl.BlockSpec((tm, D), lambda i: (i, 0))
    hbm = pl.BlockSpec(memory_space=pl.ANY)
    mixer_proj = heads is not None
    mix_specs = ([pl.BlockSpec((None, D, tm), lambda i: (i, 0, 0)), _resident(w_mix.shape[1:], mix_layer)]
                 if mixer_proj else [])
    mix_args = (heads, w_mix) if mixer_proj else ()
    return pl.pallas_call(
        functools.partial(_ffn_kernel, layer=layer, mixer_proj=mixer_proj, final_norm=final_norm),
        grid=(T // tm,),
        in_specs=[tok] + mix_specs + [_resident((D, LANE), layer), hbm, hbm, _resident((1, D))],
        out_specs=tok,
        out_shape=jax.ShapeDtypeStruct(h2.shape, h2.dtype),
        scratch_shapes=[
            pltpu.VMEM((D, 2 * D_FF), jnp.bfloat16),
            pltpu.VMEM((D_FF, D), jnp.bfloat16),
            pltpu.VMEM((tm, D_FF), jnp.bfloat16),
            pltpu.VMEM((2, 2, D, FF_CHUNK), jnp.float32),
            pltpu.VMEM((2, FF_CHUNK, D), jnp.float32),
            pltpu.SemaphoreType.DMA((2, 3)),
        ],
        compiler_params=_params("arbitrary"),
        name="swiglu" + ("_mix" if mixer_proj else "") + ("_final" if final_norm else ""),
    )(h2, *mix_args, gain, w_in, w_out, final_gain)


def _proj_kernel(x_ref, gain_ref, wk_ref, wt_ref, hn_ref, qt_ref, k_ref, vt_ref, ogt_ref, gt_ref):
    xn = _rms_norm(x_ref[...], gain_ref[...]).astype(jnp.bfloat16)
    k_ref[...] = _dot(xn, wk_ref[...]).astype(k_ref.dtype)
    qg = _dot(wt_ref[0:QG_ROWS, :], xn, _NT)
    qt_ref[...] = (qg[0:QK_W, :] * (DQK ** -0.5)).astype(qt_ref.dtype)
    gt_ref[...] = qg[QK_W:QK_W + N_GATES, :]
    vt_ref[...] = _dot(wt_ref[QG_ROWS:QG_ROWS + V_W, :], xn, _NT).astype(vt_ref.dtype)
    og = jax.nn.sigmoid(_dot(wt_ref[QG_ROWS + V_W:, :], xn, _NT))
    for lo in range(0, og.shape[1], LANE):
        ogt_ref[:, lo:lo + LANE] = (og[:, lo:lo + LANE] * hn_ref[...]).astype(ogt_ref.dtype)


def _proj_layer(h2, gain, w_k, w_t, head_norm_b, layer):
    T, D = h2.shape
    tm = PROJ_TILE
    tok = lambda i: (i, 0)
    lanes = lambda i: (0, i)
    tile = lambda i: (i, 0, 0)
    return pl.pallas_call(
        _proj_kernel,
        grid=(T // tm,),
        in_specs=[
            pl.BlockSpec((tm, D), tok),
            _resident((1, D), layer),
            _resident(w_k.shape[1:], layer),
            _resident(w_t.shape[1:], layer),
            _resident(head_norm_b.shape[1:], layer),
        ],
        out_specs=[
            pl.BlockSpec((None, QK_W, tm), tile),
            pl.BlockSpec((tm, QK_W), tok),
            pl.BlockSpec((None, V_W, tm), tile),
            pl.BlockSpec((None, V_W, tm), tile),
            pl.BlockSpec((N_GATES, tm), lanes),
        ],
        out_shape=[
            jax.ShapeDtypeStruct((T // tm, QK_W, tm), jnp.bfloat16),
            jax.ShapeDtypeStruct((T, QK_W), jnp.bfloat16),
            jax.ShapeDtypeStruct((T // tm, V_W, tm), jnp.bfloat16),
            jax.ShapeDtypeStruct((T // tm, V_W, tm), jnp.bfloat16),
            jax.ShapeDtypeStruct((N_GATES, T), jnp.float32),
        ],
        compiler_params=_params("arbitrary"),
        name="mlstm_proj",
    )(h2, gain, w_k, w_t, head_norm_b)


def _segmented_scan(x, op, lane, seg):
    shift = 1
    while shift < seg:
        moved = pltpu.roll(x, shift, 1)
        x = jnp.where((lane % seg) >= shift, op(x, moved), x)
        shift *= 2
    return x


def _gate_kernel(gt_ref, bias_ref, r_ref):
    L = MLSTM_CHUNK
    B, _, S = r_ref.shape
    g = gt_ref[...] + bias_ref[...]
    per_seq = lambda lo: jnp.concatenate(
        [g[lo:lo + N_HEADS, s * S:(s + 1) * S] for s in range(B)], axis=0)
    i_pre = per_seq(0)
    f_pre = per_seq(N_HEADS)
    logf = jnp.minimum(f_pre, 0.0) - jnp.log1p(jnp.exp(-jnp.abs(f_pre)))
    lane = lax.broadcasted_iota(jnp.int32, (B * N_HEADS, S), 1)
    b = _segmented_scan(logf, jnp.add, lane, L)
    a = i_pre - b
    amax = _segmented_scan(a, jnp.maximum, lane, L)

    def put(kind, lanes, value):
        for s in range(B):
            r_ref[s, kind * N_HEADS:(kind + 1) * N_HEADS, lanes] = value[s * N_HEADS:(s + 1) * N_HEADS]

    put(0, slice(0, S), a)
    r_ref[:, 5 * N_HEADS:N_GATE_ROWS, :] = jnp.zeros((B, N_GATE_ROWS - 5 * N_HEADS, S), jnp.float32)
    m_prev = jnp.zeros((B * N_HEADS, 1), jnp.float32)
    for c in range(S // L):
        seg = slice(c * L, (c + 1) * L)
        big_m = jnp.maximum(m_prev, amax[:, seg])
        m_last = big_m[:, L - 1:L]
        put(1, seg, big_m)
        put(2, seg, -(b[:, seg] + big_m))
        put(3, seg, m_prev - big_m)
        put(4, seg, a[:, seg] - m_last)
        m_prev = b[:, (c + 1) * L - 1:(c + 1) * L] + m_last


def _gate_layer(gates_t, bias, B, S, layer):
    return pl.pallas_call(
        _gate_kernel,
        grid=(1,),
        in_specs=[
            pl.BlockSpec((N_GATES, B * S), lambda i: (0, 0)),
            _resident((N_GATES, 1), layer),
        ],
        out_specs=pl.BlockSpec((B, N_GATE_ROWS, S), lambda i: (0, 0, 0)),
        out_shape=jax.ShapeDtypeStruct((B, N_GATE_ROWS, S), jnp.float32),
        compiler_params=_params("arbitrary"),
        name="mlstm_gates",
    )(gates_t, bias)


def _hi_lo_rows(row):
    hi = row.astype(jnp.bfloat16).astype(jnp.float32)
    idx = lax.broadcasted_iota(jnp.int32, (16, row.shape[1]), 0)
    slab = jnp.where(idx == 0, hi, jnp.where(idx == 1, row - hi, 0.0))
    return slab.astype(jnp.bfloat16)


def _chunk_kernel(qt_ref, k_ref, vt_ref, ogt_ref, r_ref, out_ref, c_ref, n_ref):
    L = MLSTM_CHUNK
    H = L // 2

    @pl.when(pl.program_id(1) == 0)
    def _():
        c_ref[...] = jnp.zeros(c_ref.shape, jnp.float32)
        n_ref[...] = jnp.zeros(n_ref.shape, jnp.float32)

    causal = (lax.broadcasted_iota(jnp.int32, (H, H), 0)
              <= lax.broadcasted_iota(jnp.int32, (H, H), 1))

    tok = lambda c: slice(c * L, (c + 1) * L)
    qk = lambda h: slice(h * DQK, (h + 1) * DQK)
    vv = lambda h: slice(h * DV, (h + 1) * DV)
    units = [(c, h) for c in range(CHUNKS_PER_STEP) for h in range(N_HEADS)]
    rows = [r_ref[0, :, tok(c)] for c in range(CHUNKS_PER_STEP)]
    a_cols = [r[0:2 * N_HEADS, :].T for r in rows]
    gate_row = lambda c, h, kind: rows[c][kind * N_HEADS + h:kind * N_HEADS + h + 1, :]
    ct = [c_ref[h] for h in range(N_HEADS)]
    n = [n_ref[h] for h in range(N_HEADS)]

    def scores(c, h):
        return _dot(k_ref[tok(c), qk(h)], qt_ref[qk(h), tok(c)])

    def weigh(c, h, s):
        a_col = a_cols[c][:, h:h + 1]
        big_m = gate_row(c, h, 1)
        diag = lambda i: jnp.exp(jnp.where(
            causal, a_col[i * H:(i + 1) * H, :] - big_m[:, i * H:(i + 1) * H], -jnp.inf))
        above = jnp.exp(a_col[0:H, :] - big_m[:, H:L])
        top = s[0:H, :] * jnp.concatenate([diag(0), above], axis=1)
        bottom = jnp.concatenate([jnp.zeros((H, H), jnp.float32), s[H:L, H:L] * diag(1)], axis=1)
        st = jnp.concatenate([top, bottom], axis=0)
        den_intra = jnp.sum(st, axis=0, keepdims=True)
        num_intra = _dot(vt_ref[vv(h), tok(c)], st.astype(jnp.bfloat16))
        state = jnp.concatenate([ct[h].astype(jnp.bfloat16), _hi_lo_rows(n[h])], axis=0)
        from_state = _dot(state, qt_ref[qk(h), tok(c)])
        return den_intra, num_intra, from_state

    def finish(c, h, den_intra, num_intra, from_state):
        inter = jnp.exp(gate_row(c, h, 3))
        w = jnp.exp(gate_row(c, h, 4))
        qn = from_state[DV:DV + 1, :] + from_state[DV + 1:DV + 2, :]
        num = inter * from_state[0:DV, :] + num_intra
        den = inter * qn + den_intra
        r = 1.0 / jnp.maximum(jnp.abs(den), jnp.exp(gate_row(c, h, 2)))
        ss = jnp.sum(num * num, axis=0, keepdims=True)
        scale = r * lax.rsqrt(r * r * ss * (1.0 / DV) + EPS)
        gated = num * scale * ogt_ref[vv(h), tok(c)].astype(jnp.float32)
        lane0 = (c * L) % FFN_TILE
        out_ref[(c * L) // FFN_TILE, vv(h), lane0:lane0 + L] = gated.astype(out_ref.dtype)

        vw = jnp.concatenate([vt_ref[vv(h), tok(c)] * w.astype(jnp.bfloat16), _hi_lo_rows(w)], axis=0)
        upd = _dot(vw, k_ref[tok(c), qk(h)])
        decay = inter[:, L - 1:L]
        ct[h] = decay * ct[h] + upd[0:DV, :]
        n[h] = decay * n[h] + upd[DV:DV + 1, :] + upd[DV + 1:DV + 2, :]

    scored = {0: scores(*units[0])}
    if len(units) > 1:
        scored[1] = scores(*units[1])
    weighed = {0: weigh(*units[0], scored.pop(0))}
    for i, u in enumerate(units):
        if i + 2 < len(units):
            scored[i + 2] = scores(*units[i + 2])
        if i + 1 < len(units):
            weighed[i + 1] = weigh(*units[i + 1], scored.pop(i + 1))
        finish(*u, *weighed.pop(i))

    for h in range(N_HEADS):
        c_ref[h] = ct[h]
        n_ref[h] = n[h]


def _chunk_layer(qt, k, vt, ogt, rows, B, S):
    T = k.shape[0]
    step = MLSTM_CHUNK * CHUNKS_PER_STEP
    assert step == PROJ_TILE and step % FFN_TILE == 0
    ns = S // step
    tok = lambda b, c: (b * ns + c, 0)
    tile = lambda b, c: (b * ns + c, 0, 0)
    return pl.pallas_call(
        _chunk_kernel,
        grid=(B, ns),
        in_specs=[
            pl.BlockSpec((None, QK_W, step), tile),
            pl.BlockSpec((step, QK_W), tok),
            pl.BlockSpec((None, V_W, step), tile),
            pl.BlockSpec((None, V_W, step), tile),
            pl.BlockSpec((1, N_GATE_ROWS, step), lambda b, c: (b, 0, c)),
        ],
        out_specs=pl.BlockSpec((step // FFN_TILE, V_W, FFN_TILE), tile),
        out_shape=jax.ShapeDtypeStruct((T // FFN_TILE, V_W, FFN_TILE), jnp.bfloat16),
        scratch_shapes=[
            pltpu.VMEM((N_HEADS, DV, DQK), jnp.float32),
            pltpu.VMEM((N_HEADS, 1, DQK), jnp.float32),
        ],
        compiler_params=_params("arbitrary", "arbitrary"),
        name="mlstm_chunk",
    )(qt, k, vt, ogt, rows)


def kernel(x, pool_norm, pool_w, pool_scale, mlstm_norm, mlstm_w_in, mlstm_gate_bias,
           mlstm_head_norm, mlstm_w_out, ffn_norm, ffn_w_in, ffn_w_out, final_norm):
    B, S, D = x.shape
    depth = ffn_norm.shape[0]
    bf16 = jnp.bfloat16
    row = lambda a: a.reshape(a.shape[0], 1, a.shape[1])
    pool_w_b = pool_w.astype(bf16)
    w_k_b = mlstm_w_in[:, :, QK_W:2 * QK_W].astype(bf16)
    w_t_b = jnp.swapaxes(
        jnp.concatenate([mlstm_w_in[:, :, :QK_W], mlstm_w_in[:, :, MAIN_W:],
                         jnp.zeros(mlstm_w_in.shape[:2] + (QG_ROWS - QK_W - N_GATES,), mlstm_w_in.dtype),
                         mlstm_w_in[:, :, 2 * QK_W:MAIN_W]], axis=2),
        1, 2).astype(bf16)
    head_norm_b = jnp.broadcast_to(mlstm_head_norm[:, :, None], mlstm_head_norm.shape + (LANE,))
    gate_bias = mlstm_gate_bias.reshape(-1, N_GATES, 1)
    w_mix_b = mlstm_w_out.astype(bf16)
    fgain = final_norm.reshape(1, D)
    ffn_gain_b = jnp.broadcast_to(ffn_norm[:, :, None], ffn_norm.shape + (LANE,))

    h2 = x.reshape(B * S, D)
    for i in range(depth):
        j = i // 2
        last = i == depth - 1
        if i % 2 == 0:
            h2 = _pool_layer(h2.reshape(B, S, D), row(pool_norm), pool_w_b, row(pool_scale), j)
            h2 = _ffn_layer(h2.reshape(B * S, D), ffn_gain_b, ffn_w_in, ffn_w_out, fgain, i, last)
        else:
            qt, k, vt, ogt, gates_t = _proj_layer(h2, row(mlstm_norm), w_k_b, w_t_b, head_norm_b, j)
            rows = _gate_layer(gates_t, gate_bias, B, S, j)
            heads = _chunk_layer(qt, k, vt, ogt, rows, B, S)
            h2 = _ffn_layer(h2, ffn_gain_b, ffn_w_in, ffn_w_out, fgain, i, last,
                            heads=heads, w_mix=w_mix_b, mix_layer=j)
    return h2.reshape(B, S, D)
```

```python
import functools
import math

import jax
import jax.numpy as jnp
from jax import lax
from jax.experimental import pallas as pl
from jax.experimental.pallas import tpu as pltpu

D_MODEL = 1024
POOL_WINDOWS = (2, 4, 8, 16)
POOL_GROUP_DIM = D_MODEL // len(POOL_WINDOWS)
POOL_HALO = 32

N_HEADS = 4
DV = D_MODEL // N_HEADS
DQK = DV // 2
QK_W = N_HEADS * DQK
V_W = N_HEADS * DV
MAIN_W = 2 * QK_W + 2 * V_W
N_GATES = 2 * N_HEADS
W_T_ROWS = MAIN_W + 16

D_FF = int(math.ceil(8 * D_MODEL / 3 / 256) * 256)
FF_CHUNK = 256
LANE = 128

EPS = 1e-6

POOL_TILE = 1024
FFN_TILE = 512
PROJ_TILE = 1024
MLSTM_CHUNK = 256
CHUNKS_PER_STEP = 4
N_GATE_ROWS = 24

VMEM_LIMIT = 56 * 1024 * 1024

_NT = (((1,), (1,)), ((), ()))


def _params(*semantics):
    return pltpu.CompilerParams(dimension_semantics=semantics,
                                vmem_limit_bytes=VMEM_LIMIT)


def _resident(shape, layer=None):
    zeros = (0,) * len(shape)
    if layer is None:
        return pl.BlockSpec(shape, lambda *_: zeros, pipeline_mode=pl.Buffered(1))
    return pl.BlockSpec((None,) + shape, lambda *_: (layer,) + zeros,
                        pipeline_mode=pl.Buffered(1))


def _rms_norm(x, gain):
    return x * lax.rsqrt(jnp.mean(x * x, axis=-1, keepdims=True) + EPS) * gain


def _dot(a, b, dims=None):
    if dims is None:
        return jnp.dot(a, b, preferred_element_type=jnp.float32)
    return lax.dot_general(a, b, dims, preferred_element_type=jnp.float32)


def _pool_kernel(x_ref, w_ref, o_ref, ext_ref, *stage_refs):
    s = pl.program_id(1)
    ts = x_ref.shape[1]
    rows = POOL_HALO + ts
    gd = POOL_GROUP_DIM
    x = x_ref[0]
    xn = x * lax.rsqrt(jnp.mean(x * x, axis=-1, keepdims=True) + EPS)

    @pl.when(s == 0)
    def _():
        ext_ref[0:POOL_HALO, :] = jnp.zeros((POOL_HALO, D_MODEL), jnp.float32)

    ext_ref[POOL_HALO:rows, :] = xn

    prev = ext_ref
    for j, cur in enumerate(stage_refs, start=1):
        shift = 2 ** (j - 1)
        lo = 8 * j
        skip = 0 if j == 1 else gd
        cur[lo:rows, :] = prev[lo:rows, skip:] + prev[lo - shift:rows - shift, skip:]
        prev = cur

    pos = (s * ts + 1 + lax.broadcasted_iota(jnp.int32, (ts, 1), 0)).astype(jnp.float32)
    for g, win in enumerate(POOL_WINDOWS):
        cols = slice(g * gd, (g + 1) * gd)
        tot = stage_refs[g][POOL_HALO:rows, 0:gd]
        inv_cnt = 1.0 / jnp.minimum(pos, float(win))
        pooled = tot * inv_cnt - xn[:, cols]
        y = _dot(pooled.astype(jnp.bfloat16), w_ref[g])
        o_ref[0, :, cols] = x[:, cols] + y

    ext_ref[0:POOL_HALO, :] = ext_ref[ts:rows, :]


def _pool_layer(h, w, layer):
    B, S, D = h.shape
    ts = POOL_TILE
    return pl.pallas_call(
        _pool_kernel,
        grid=(B, S // ts),
        in_specs=[
            pl.BlockSpec((1, ts, D), lambda b, s: (b, s, 0)),
            _resident(w.shape[1:], layer),
        ],
        out_specs=pl.BlockSpec((1, ts, D), lambda b, s: (b, s, 0)),
        out_shape=jax.ShapeDtypeStruct(h.shape, h.dtype),
        scratch_shapes=[pltpu.VMEM((POOL_HALO + ts, D), jnp.float32)] + [
            pltpu.VMEM((POOL_HALO + ts, D - g * POOL_GROUP_DIM), jnp.float32)
            for g in range(len(POOL_WINDOWS))],
        compiler_params=_params("arbitrary", "arbitrary"),
        name="pool_mixer",
    )(h, w)


def _ffn_kernel(*refs, layer, mixer_proj, final_norm):
    if mixer_proj:
        x_ref, heads_ref, w_mix_ref, *refs = refs
    else:
        x_ref, *refs = refs
    (gain_ref, w_in_hbm, w_out_hbm, fgain_ref, o_ref,
     w_in_ref, w_out_ref, act_ref, stage_in, stage_out, sem) = refs
    n_chunks = D_FF // FF_CHUNK

    def chunk_copies(j, slot):
        lo = j * FF_CHUNK
        return (
            pltpu.make_async_copy(w_in_hbm.at[layer, :, lo:lo + FF_CHUNK],
                                  stage_in.at[slot, 0], sem.at[slot, 0]),
            pltpu.make_async_copy(w_in_hbm.at[layer, :, D_FF + lo:D_FF + lo + FF_CHUNK],
                                  stage_in.at[slot, 1], sem.at[slot, 1]),
            pltpu.make_async_copy(w_out_hbm.at[layer, lo:lo + FF_CHUNK, :],
                                  stage_out.at[slot], sem.at[slot, 2]),
        )

    def body(stream_weights):
        if stream_weights:
            for cp in chunk_copies(0, 0):
                cp.start()
        if mixer_proj:
            x = x_ref[...] + _dot(heads_ref[...], w_mix_ref[...], (((0,), (0,)), ((), ())))
        else:
            x = x_ref[...]
        xb = x.astype(jnp.bfloat16)
        rstd = lax.rsqrt(jnp.mean(x * x, axis=-1, keepdims=True) + EPS)
        for j in range(n_chunks):
            lo = j * FF_CHUNK
            if stream_weights:
                slot = j % 2
                if j + 1 < n_chunks:
                    for cp in chunk_copies(j + 1, 1 - slot):
                        cp.start()
                for cp in chunk_copies(j, slot):
                    cp.wait()
                for part, col in ((0, lo), (1, D_FF + lo)):
                    for sub in range(0, FF_CHUNK, LANE):
                        w_in_ref[:, col + sub:col + sub + LANE] = (
                            stage_in[slot, part, :, sub:sub + LANE] * gain_ref[...]).astype(jnp.bfloat16)
                w_out_ref[lo:lo + FF_CHUNK, :] = stage_out[slot].astype(jnp.bfloat16)
            gate = _dot(xb, w_in_ref[:, lo:lo + FF_CHUNK]) * rstd
            up = _dot(xb, w_in_ref[:, D_FF + lo:D_FF + lo + FF_CHUNK]) * rstd
            act_ref[:, lo:lo + FF_CHUNK] = (gate * jax.nn.sigmoid(gate) * up).astype(jnp.bfloat16)
        out = x + _dot(act_ref[...], w_out_ref[...])
        if final_norm:
            out = _rms_norm(out, fgain_ref[...])
        o_ref[...] = out

    first = pl.program_id(0) == 0
    pl.when(first)(functools.partial(body, True))
    pl.when(jnp.logical_not(first))(functools.partial(body, False))


def _ffn_layer(h2, gain, w_in, w_out, final_gain, layer, final_norm, heads=None, w_mix=None,
               mix_layer=None):
    T, D = h2.shape
    tm = FFN_TILE
    tok = pl.BlockSpec((tm, D), lambda i: (i, 0))
    hbm = pl.BlockSpec(memory_space=pl.ANY)
    mixer_proj = heads is not None
    mix_specs = ([pl.BlockSpec((None, D, tm), lambda i: (i, 0, 0)), _resident(w_mix.shape[1:], mix_layer)]
                 if mixer_proj else [])
    mix_args = (heads, w_mix) if mixer_proj else ()
    return pl.pallas_call(
        functools.partial(_ffn_kernel, layer=layer, mixer_proj=mixer_proj, final_norm=final_norm),
        grid=(T // tm,),
        in_specs=[tok] + mix_specs + [_resident((D, LANE), layer), hbm, hbm, _resident((1, D))],
        out_specs=tok,
        out_shape=jax.ShapeDtypeStruct(h2.shape, h2.dtype),
        scratch_shapes=[
            pltpu.VMEM((D, 2 * D_FF), jnp.bfloat16),
            pltpu.VMEM((D_FF, D), jnp.bfloat16),
            pltpu.VMEM((tm, D_FF), jnp.bfloat16),
            pltpu.VMEM((2, 2, D, FF_CHUNK), jnp.float32),
            pltpu.VMEM((2, FF_CHUNK, D), jnp.float32),
            pltpu.SemaphoreType.DMA((2, 3)),
        ],
        compiler_params=_params("arbitrary"),
        name="swiglu" + ("_mix" if mixer_proj else "") + ("_final" if final_norm else ""),
    )(h2, *mix_args, gain, w_in, w_out, final_gain)


def _proj_kernel(x_ref, wt_ref, hn_ref, qt_ref, k_ref, vt_ref, ogt_ref, gt_ref):
    x = x_ref[...]
    xb = x.astype(jnp.bfloat16)
    rstd = lax.rsqrt(jnp.mean(x * x, axis=-1, keepdims=True) + EPS)
    rstd_row = jnp.broadcast_to(rstd, (x.shape[0], LANE)).T[0:1, :]
    og = _dot(wt_ref[2 * QK_W + V_W:, :], xb, _NT) * rstd_row
    gt_ref[...] = og[V_W:V_W + N_GATES, :]
    for lo in range(0, og.shape[1], LANE):
        ogt_ref[:, lo:lo + LANE] = (jax.nn.sigmoid(og[0:V_W, lo:lo + LANE]) * hn_ref[...]).astype(ogt_ref.dtype)
    vt_ref[...] = (_dot(wt_ref[2 * QK_W:2 * QK_W + V_W, :], xb, _NT) * rstd_row).astype(vt_ref.dtype)
    qt_ref[...] = (_dot(wt_ref[0:QK_W, :], xb, _NT) * (rstd_row * (DQK ** -0.5))).astype(qt_ref.dtype)
    k_ref[...] = (_dot(xb, wt_ref[QK_W:2 * QK_W, :], _NT) * rstd).astype(k_ref.dtype)


def _proj_layer(h2, w_t, head_norm_b, layer):
    T, D = h2.shape
    tm = PROJ_TILE
    tok = lambda i: (i, 0)
    lanes = lambda i: (0, i)
    tile = lambda i: (i, 0, 0)
    return pl.pallas_call(
        _proj_kernel,
        grid=(T // tm,),
        in_specs=[
            pl.BlockSpec((tm, D), tok),
            _resident(w_t.shape[1:], layer),
            _resident(head_norm_b.shape[1:], layer),
        ],
        out_specs=[
            pl.BlockSpec((None, QK_W, tm), tile),
            pl.BlockSpec((tm, QK_W), tok),
            pl.BlockSpec((None, V_W, tm), tile),
            pl.BlockSpec((None, V_W, tm), tile),
            pl.BlockSpec((N_GATES, tm), lanes),
        ],
        out_shape=[
            jax.ShapeDtypeStruct((T // tm, QK_W, tm), jnp.bfloat16),
            jax.ShapeDtypeStruct((T, QK_W), jnp.bfloat16),
            jax.ShapeDtypeStruct((T // tm, V_W, tm), jnp.bfloat16),
            jax.ShapeDtypeStruct((T // tm, V_W, tm), jnp.bfloat16),
            jax.ShapeDtypeStruct((N_GATES, T), jnp.float32),
        ],
        compiler_params=_params("arbitrary"),
        name="mlstm_proj",
    )(h2, w_t, head_norm_b)


def _segmented_scan(x, op, lane, seg):
    shift = 1
    while shift < seg:
        moved = pltpu.roll(x, shift, 1)
        x = jnp.where((lane % seg) >= shift, op(x, moved), x)
        shift *= 2
    return x


def _gate_kernel(gt_ref, bias_ref, r_ref):
    L = MLSTM_CHUNK
    B, _, S = r_ref.shape
    g = gt_ref[...] + bias_ref[...]
    per_seq = lambda lo: jnp.concatenate(
        [g[lo:lo + N_HEADS, s * S:(s + 1) * S] for s in range(B)], axis=0)
    i_pre = per_seq(0)
    f_pre = per_seq(N_HEADS)
    logf = jnp.minimum(f_pre, 0.0) - jnp.log1p(jnp.exp(-jnp.abs(f_pre)))
    lane = lax.broadcasted_iota(jnp.int32, (B * N_HEADS, S), 1)
    b = _segmented_scan(logf, jnp.add, lane, L)
    a = i_pre - b
    amax = _segmented_scan(a, jnp.maximum, lane, L)

    def put(kind, lanes, value):
        for s in range(B):
            r_ref[s, kind * N_HEADS:(kind + 1) * N_HEADS, lanes] = value[s * N_HEADS:(s + 1) * N_HEADS]

    put(0, slice(0, S), a)
    r_ref[:, 5 * N_HEADS:N_GATE_ROWS, :] = jnp.zeros((B, N_GATE_ROWS - 5 * N_HEADS, S), jnp.float32)
    m_prev = jnp.zeros((B * N_HEADS, 1), jnp.float32)
    for c in range(S // L):
        seg = slice(c * L, (c + 1) * L)
        big_m = jnp.maximum(m_prev, amax[:, seg])
        m_last = big_m[:, L - 1:L]
        put(1, seg, big_m)
        put(2, seg, -(b[:, seg] + big_m))
        put(3, seg, m_prev - big_m)
        put(4, seg, a[:, seg] - m_last)
        m_prev = b[:, (c + 1) * L - 1:(c + 1) * L] + m_last


def _gate_layer(gates_t, bias, B, S, layer):
    return pl.pallas_call(
        _gate_kernel,
        grid=(1,),
        in_specs=[
            pl.BlockSpec((N_GATES, B * S), lambda i: (0, 0)),
            _resident((N_GATES, 1), layer),
        ],
        out_specs=pl.BlockSpec((B, N_GATE_ROWS, S), lambda i: (0, 0, 0)),
        out_shape=jax.ShapeDtypeStruct((B, N_GATE_ROWS, S), jnp.float32),
        compiler_params=_params("arbitrary"),
        name="mlstm_gates",
    )(gates_t, bias)


def _hi_lo_rows(row):
    hi = row.astype(jnp.bfloat16).astype(jnp.float32)
    idx = lax.broadcasted_iota(jnp.int32, (16, row.shape[1]), 0)
    slab = jnp.where(idx == 0, hi, jnp.where(idx == 1, row - hi, 0.0))
    return slab.astype(jnp.bfloat16)


def _chunk_kernel(qt_ref, k_ref, vt_ref, ogt_ref, r_ref, out_ref, c_ref, n_ref):
    L = MLSTM_CHUNK
    H = L // 2

    @pl.when(pl.program_id(1) == 0)
    def _():
        c_ref[...] = jnp.zeros(c_ref.shape, jnp.float32)
        n_ref[...] = jnp.zeros(n_ref.shape, jnp.float32)

    causal = (lax.broadcasted_iota(jnp.int32, (H, H), 0)
              <= lax.broadcasted_iota(jnp.int32, (H, H), 1))

    tok = lambda c: slice(c * L, (c + 1) * L)
    qk = lambda h: slice(h * DQK, (h + 1) * DQK)
    vv = lambda h: slice(h * DV, (h + 1) * DV)
    units = [(c, h) for c in range(CHUNKS_PER_STEP) for h in range(N_HEADS)]
    rows = [r_ref[0, :, tok(c)] for c in range(CHUNKS_PER_STEP)]
    a_cols = [r[0:2 * N_HEADS, :].T for r in rows]
    gate_row = lambda c, h, kind: rows[c][kind * N_HEADS + h:kind * N_HEADS + h + 1, :]
    ct = [c_ref[h] for h in range(N_HEADS)]
    n = [n_ref[h] for h in range(N_HEADS)]

    def scores(c, h):
        return _dot(k_ref[tok(c), qk(h)], qt_ref[qk(h), tok(c)])

    def weigh(c, h, s):
        a_col = a_cols[c][:, h:h + 1]
        big_m = gate_row(c, h, 1)
        diag = lambda i: jnp.exp(jnp.where(
            causal, a_col[i * H:(i + 1) * H, :] - big_m[:, i * H:(i + 1) * H], -jnp.inf))
        above = jnp.exp(a_col[0:H, :] - big_m[:, H:L])
        top = s[0:H, :] * jnp.concatenate([diag(0), above], axis=1)
        bottom = jnp.concatenate([jnp.zeros((H, H), jnp.float32), s[H:L, H:L] * diag(1)], axis=1)
        st = jnp.concatenate([top, bottom], axis=0)
        den_intra = jnp.sum(st, axis=0, keepdims=True)
        num_intra = _dot(vt_ref[vv(h), tok(c)], st.astype(jnp.bfloat16))
        state = jnp.concatenate([ct[h].astype(jnp.bfloat16), _hi_lo_rows(n[h])], axis=0)
        from_state = _dot(state, qt_ref[qk(h), tok(c)])
        return den_intra, num_intra, from_state

    def finish(c, h, den_intra, num_intra, from_state):
        inter = jnp.exp(gate_row(c, h, 3))
        w = jnp.exp(gate_row(c, h, 4))
        qn = from_state[DV:DV + 1, :] + from_state[DV + 1:DV + 2, :]
        num = inter * from_state[0:DV, :] + num_intra
        den = inter * qn + den_intra
        r = 1.0 / jnp.maximum(jnp.abs(den), jnp.exp(gate_row(c, h, 2)))
        ss = jnp.sum(num * num, axis=0, keepdims=True)
        scale = r * lax.rsqrt(r * r * ss * (1.0 / DV) + EPS)
        gated = num * scale * ogt_ref[vv(h), tok(c)].astype(jnp.float32)
        lane0 = (c * L) % FFN_TILE
        out_ref[(c * L) // FFN_TILE, vv(h), lane0:lane0 + L] = gated.astype(out_ref.dtype)

        vw = jnp.concatenate([vt_ref[vv(h), tok(c)] * w.astype(jnp.bfloat16), _hi_lo_rows(w)], axis=0)
        upd = _dot(vw, k_ref[tok(c), qk(h)])
        decay = inter[:, L - 1:L]
        ct[h] = decay * ct[h] + upd[0:DV, :]
        n[h] = decay * n[h] + upd[DV:DV + 1, :] + upd[DV + 1:DV + 2, :]

    scored = {0: scores(*units[0])}
    if len(units) > 1:
        scored[1] = scores(*units[1])
    weighed = {0: weigh(*units[0], scored.pop(0))}
    for i, u in enumerate(units):
        if i + 2 < len(units):
            scored[i + 2] = scores(*units[i + 2])
        if i + 1 < len(units):
            weighed[i + 1] = weigh(*units[i + 1], scored.pop(i + 1))
        finish(*u, *weighed.pop(i))

    for h in range(N_HEADS):
        c_ref[h] = ct[h]
        n_ref[h] = n[h]


def _chunk_layer(qt, k, vt, ogt, rows, B, S):
    T = k.shape[0]
    step = MLSTM_CHUNK * CHUNKS_PER_STEP
    assert step == PROJ_TILE and step % FFN_TILE == 0
    ns = S // step
    tok = lambda b, c: (b * ns + c, 0)
    tile = lambda b, c: (b * ns + c, 0, 0)
    return pl.pallas_call(
        _chunk_kernel,
        grid=(B, ns),
        in_specs=[
            pl.BlockSpec((None, QK_W, step), tile),
            pl.BlockSpec((step, QK_W), tok),
            pl.BlockSpec((None, V_W, step), tile),
            pl.BlockSpec((None, V_W, step), tile),
            pl.BlockSpec((1, N_GATE_ROWS, step), lambda b, c: (b, 0, c)),
        ],
        out_specs=pl.BlockSpec((step // FFN_TILE, V_W, FFN_TILE), tile),
        out_shape=jax.ShapeDtypeStruct((T // FFN_TILE, V_W, FFN_TILE), jnp.bfloat16),
        scratch_shapes=[
            pltpu.VMEM((N_HEADS, DV, DQK), jnp.float32),
            pltpu.VMEM((N_HEADS, 1, DQK), jnp.float32),
        ],
        compiler_params=_params("arbitrary", "arbitrary"),
        name="mlstm_chunk",
    )(qt, k, vt, ogt, rows)


def kernel(x, pool_norm, pool_w, pool_scale, mlstm_norm, mlstm_w_in, mlstm_gate_bias,
           mlstm_head_norm, mlstm_w_out, ffn_norm, ffn_w_in, ffn_w_out, final_norm):
    B, S, D = x.shape
    depth = ffn_norm.shape[0]
    bf16 = jnp.bfloat16
    groups = pool_w.shape[:2] + (POOL_GROUP_DIM,)
    pool_w_b = (pool_norm.reshape(groups)[..., :, None] * pool_w
                * pool_scale.reshape(groups)[..., None, :]).astype(bf16)
    w_t_b = jnp.swapaxes(
        jnp.pad(mlstm_w_in * mlstm_norm[:, :, None], ((0, 0), (0, 0), (0, W_T_ROWS - mlstm_w_in.shape[2]))),
        1, 2).astype(bf16)
    head_norm_b = jnp.broadcast_to(mlstm_head_norm[:, :, None], mlstm_head_norm.shape + (LANE,))
    gate_bias = mlstm_gate_bias.reshape(-1, N_GATES, 1)
    w_mix_b = mlstm_w_out.astype(bf16)
    fgain = final_norm.reshape(1, D)
    ffn_gain_b = jnp.broadcast_to(ffn_norm[:, :, None], ffn_norm.shape + (LANE,))

    h2 = x.reshape(B * S, D)
    for i in range(depth):
        j = i // 2
        last = i == depth - 1
        if i % 2 == 0:
            h2 = _pool_layer(h2.reshape(B, S, D), pool_w_b, j)
            h2 = _ffn_layer(h2.reshape(B * S, D), ffn_gain_b, ffn_w_in, ffn_w_out, fgain, i, last)
        else:
            qt, k, vt, ogt, gates_t = _proj_layer(h2, w_t_b, head_norm_b, j)
            rows = _gate_layer(gates_t, gate_bias, B, S, j)
            heads = _chunk_layer(qt, k, vt, ogt, rows, B, S)
            h2 = _ffn_layer(h2, ffn_gain_b, ffn_w_in, ffn_w_out, fgain, i, last,
                            heads=heads, w_mix=w_mix_b, mix_layer=j)
    return h2.reshape(B, S, D)
```

```python
import functools
import math

import jax
import jax.numpy as jnp
from jax import lax
from jax.experimental import pallas as pl
from jax.experimental.pallas import tpu as pltpu

D_MODEL = 1024
POOL_WINDOWS = (2, 4, 8, 16)
POOL_GROUP_DIM = D_MODEL // len(POOL_WINDOWS)
POOL_HALO = 32

N_HEADS = 4
DV = D_MODEL // N_HEADS
DQK = DV // 2
QK_W = N_HEADS * DQK
V_W = N_HEADS * DV
MAIN_W = 2 * QK_W + 2 * V_W
N_GATES = 2 * N_HEADS
W_T_ROWS = MAIN_W + 16

D_FF = int(math.ceil(8 * D_MODEL / 3 / 256) * 256)
FF_CHUNK = 256
LANE = 128

EPS = 1e-6

POOL_BLOCK = 128
FFN_TILE = 512
PROJ_TILE = 1024
MLSTM_CHUNK = 256
CHUNKS_PER_STEP = 4
N_GATE_ROWS = 24

VMEM_LIMIT = 56 * 1024 * 1024

_NT = (((1,), (1,)), ((), ()))


def _params(*semantics):
    return pltpu.CompilerParams(dimension_semantics=semantics,
                                vmem_limit_bytes=VMEM_LIMIT)


def _resident(shape, layer=None):
    zeros = (0,) * len(shape)
    if layer is None:
        return pl.BlockSpec(shape, lambda *_: zeros, pipeline_mode=pl.Buffered(1))
    return pl.BlockSpec((None,) + shape, lambda *_: (layer,) + zeros,
                        pipeline_mode=pl.Buffered(1))


def _rms_norm(x, gain):
    return x * lax.rsqrt(jnp.mean(x * x, axis=-1, keepdims=True) + EPS) * gain


def _dot(a, b, dims=None):
    if dims is None:
        return jnp.dot(a, b, preferred_element_type=jnp.float32)
    return lax.dot_general(a, b, dims, preferred_element_type=jnp.float32)


def _pool_rows(x_ref, r0, r1, tile_in_seq, ext_ref, stage_refs, pooled_ref):
    gd = POOL_GROUP_DIM
    lo, hi = POOL_HALO + r0, POOL_HALO + r1
    x = x_ref[r0:r1, :]
    xn = x * lax.rsqrt(jnp.mean(x * x, axis=-1, keepdims=True) + EPS)
    ext_ref[lo:hi, :] = xn

    prev = ext_ref
    for j, cur in enumerate(stage_refs, start=1):
        shift = 2 ** (j - 1)
        start = 8 * j if r0 == 0 else lo
        skip = 0 if j == 1 else gd
        cur[start:hi, :] = prev[start:hi, skip:] + prev[start - shift:hi - shift, skip:]
        prev = cur

    ts = x_ref.shape[0]
    pos = (tile_in_seq * ts + r0 + 1
           + lax.broadcasted_iota(jnp.int32, (r1 - r0, 1), 0)).astype(jnp.float32)
    for g, win in enumerate(POOL_WINDOWS):
        cols = slice(g * gd, (g + 1) * gd)
        inv_cnt = 1.0 / jnp.minimum(pos, float(win))
        pooled = stage_refs[g][lo:hi, 0:gd] * inv_cnt - xn[:, cols]
        pooled_ref[r0:r1, cols] = pooled.astype(jnp.bfloat16)


def _pool_finish(x_ref, w_ref, ext_ref, pooled_ref, out_ref):
    ts = x_ref.shape[0]
    gd = POOL_GROUP_DIM
    for g in range(len(POOL_WINDOWS)):
        cols = slice(g * gd, (g + 1) * gd)
        out_ref[:, cols] = x_ref[:, cols] + _dot(pooled_ref[:, cols], w_ref[g])
    ext_ref[0:POOL_HALO, :] = ext_ref[ts:ts + POOL_HALO, :]


def _pool_jobs(x_ref, tile_in_seq, w_ref, ext_ref, stage_refs, pooled_ref, out_ref):
    jobs = [functools.partial(_pool_rows, x_ref, r0, r0 + POOL_BLOCK, tile_in_seq, ext_ref,
                              stage_refs, pooled_ref)
            for r0 in range(0, x_ref.shape[0], POOL_BLOCK)]
    jobs.append(functools.partial(_pool_finish, x_ref, w_ref, ext_ref, pooled_ref, out_ref))
    return jobs


def _ffn_kernel(*refs, layer, mixer_proj, final_norm, tiles_per_seq):
    if mixer_proj:
        x_ref, heads_ref, w_mix_ref, *refs = refs
    else:
        x_first_ref, x_next_ref, pool_w_ref, *refs = refs
    (gain_ref, w_in_hbm, w_out_hbm, fgain_ref, o_ref,
     w_in_ref, w_out_ref, act_ref, stage_in, stage_out, sem, *pool_scratch) = refs
    n_chunks = D_FF // FF_CHUNK
    step = pl.program_id(0)
    if not mixer_proj:
        mixed_ref, cur_ref, pooled_ref, ext_ref, *stage_refs = pool_scratch
        next_in_seq = (step + 1) % tiles_per_seq

        @pl.when(next_in_seq == 0)
        def _():
            ext_ref[0:POOL_HALO, :] = jnp.zeros((POOL_HALO, D_MODEL), jnp.float32)

    def chunk_copies(j, slot):
        lo = j * FF_CHUNK
        return (
            pltpu.make_async_copy(w_in_hbm.at[layer, :, lo:lo + FF_CHUNK],
                                  stage_in.at[slot, 0], sem.at[slot, 0]),
            pltpu.make_async_copy(w_in_hbm.at[layer, :, D_FF + lo:D_FF + lo + FF_CHUNK],
                                  stage_in.at[slot, 1], sem.at[slot, 1]),
            pltpu.make_async_copy(w_out_hbm.at[layer, lo:lo + FF_CHUNK, :],
                                  stage_out.at[slot], sem.at[slot, 2]),
        )

    def body(stream_weights):
        if stream_weights:
            for cp in chunk_copies(0, 0):
                cp.start()
        side_jobs = []
        if mixer_proj:
            x = x_ref[...] + _dot(heads_ref[...], w_mix_ref[...], (((0,), (0,)), ((), ())))
        else:
            if stream_weights:
                ext_ref[0:POOL_HALO, :] = jnp.zeros((POOL_HALO, D_MODEL), jnp.float32)
                for job in _pool_jobs(x_first_ref, 0, pool_w_ref, ext_ref, stage_refs, pooled_ref,
                                      mixed_ref):
                    job()
            cur_ref[...] = mixed_ref[...]
            x = cur_ref[...]
            side_jobs = _pool_jobs(x_next_ref, next_in_seq, pool_w_ref, ext_ref, stage_refs,
                                   pooled_ref, mixed_ref)
        xb = x.astype(jnp.bfloat16)
        rstd = lax.rsqrt(jnp.mean(x * x, axis=-1, keepdims=True) + EPS)
        for j in range(n_chunks):
            lo = j * FF_CHUNK
            if stream_weights:
                slot = j % 2
                if j + 1 < n_chunks:
                    for cp in chunk_copies(j + 1, 1 - slot):
                        cp.start()
                for cp in chunk_copies(j, slot):
                    cp.wait()
                for part, col in ((0, lo), (1, D_FF + lo)):
                    for sub in range(0, FF_CHUNK, LANE):
                        w_in_ref[:, col + sub:col + sub + LANE] = (
                            stage_in[slot, part, :, sub:sub + LANE] * gain_ref[...]).astype(jnp.bfloat16)
                w_out_ref[lo:lo + FF_CHUNK, :] = stage_out[slot].astype(jnp.bfloat16)
            gate = _dot(xb, w_in_ref[:, lo:lo + FF_CHUNK]) * rstd
            up = _dot(xb, w_in_ref[:, D_FF + lo:D_FF + lo + FF_CHUNK]) * rstd
            act_ref[:, lo:lo + FF_CHUNK] = (gate * jax.nn.sigmoid(gate) * up).astype(jnp.bfloat16)
            if j < len(side_jobs):
                side_jobs[j]()
        out = x + _dot(act_ref[...], w_out_ref[...])
        if final_norm:
            out = _rms_norm(out, fgain_ref[...])
        o_ref[...] = out

    first = step == 0
    pl.when(first)(functools.partial(body, True))
    pl.when(jnp.logical_not(first))(functools.partial(body, False))


def _ffn_layer(h2, S, gain, w_in, w_out, final_gain, layer, final_norm, heads=None, w_mix=None,
               pool_w=None, mix_layer=None):
    T, D = h2.shape
    tm = FFN_TILE
    nt = T // tm
    assert S // tm > 1 and tm % POOL_BLOCK == 0 and tm // POOL_BLOCK < D_FF // FF_CHUNK
    tok = pl.BlockSpec((tm, D), lambda i: (i, 0))
    hbm = pl.BlockSpec(memory_space=pl.ANY)
    mixer_proj = heads is not None
    if mixer_proj:
        mix_specs = [tok, pl.BlockSpec((None, D, tm), lambda i: (i, 0, 0)),
                     _resident(w_mix.shape[1:], mix_layer)]
        mix_args = (h2, heads, w_mix)
        pool_scratch = []
    else:
        mix_specs = [pl.BlockSpec((tm, D), lambda i: (0, 0), pipeline_mode=pl.Buffered(1)),
                     pl.BlockSpec((tm, D), lambda i: (jnp.minimum(i + 1, nt - 1), 0)),
                     _resident(pool_w.shape[1:], mix_layer)]
        mix_args = (h2, h2, pool_w)
        pool_scratch = [
            pltpu.VMEM((tm, D), jnp.float32),
            pltpu.VMEM((tm, D), jnp.float32),
            pltpu.VMEM((tm, D), jnp.bfloat16),
            pltpu.VMEM((POOL_HALO + tm, D), jnp.float32),
        ] + [pltpu.VMEM((POOL_HALO + tm, D - g * POOL_GROUP_DIM), jnp.float32)
             for g in range(len(POOL_WINDOWS))]
    return pl.pallas_call(
        functools.partial(_ffn_kernel, layer=layer, mixer_proj=mixer_proj, final_norm=final_norm,
                          tiles_per_seq=S // tm),
        grid=(nt,),
        in_specs=mix_specs + [_resident((D, LANE), layer), hbm, hbm, _resident((1, D))],
        out_specs=tok,
        out_shape=jax.ShapeDtypeStruct(h2.shape, h2.dtype),
        scratch_shapes=[
            pltpu.VMEM((D, 2 * D_FF), jnp.bfloat16),
            pltpu.VMEM((D_FF, D), jnp.bfloat16),
            pltpu.VMEM((tm, D_FF), jnp.bfloat16),
            pltpu.VMEM((2, 2, D, FF_CHUNK), jnp.float32),
            pltpu.VMEM((2, FF_CHUNK, D), jnp.float32),
            pltpu.SemaphoreType.DMA((2, 3)),
        ] + pool_scratch,
        compiler_params=_params("arbitrary"),
        name="swiglu" + ("_mix" if mixer_proj else "_pool") + ("_final" if final_norm else ""),
    )(*mix_args, gain, w_in, w_out, final_gain)


def _proj_kernel(x_ref, wt_ref, hn_ref, qt_ref, k_ref, vt_ref, ogt_ref, gt_ref):
    x = x_ref[...]
    xb = x.astype(jnp.bfloat16)
    rstd = lax.rsqrt(jnp.mean(x * x, axis=-1, keepdims=True) + EPS)
    rstd_row = jnp.broadcast_to(rstd, (x.shape[0], LANE)).T[0:1, :]
    og = _dot(wt_ref[2 * QK_W + V_W:, :], xb, _NT) * rstd_row
    gt_ref[...] = og[V_W:V_W + N_GATES, :]
    for lo in range(0, og.shape[1], LANE):
        ogt_ref[:, lo:lo + LANE] = (jax.nn.sigmoid(og[0:V_W, lo:lo + LANE]) * hn_ref[...]).astype(ogt_ref.dtype)
    vt_ref[...] = (_dot(wt_ref[2 * QK_W:2 * QK_W + V_W, :], xb, _NT) * rstd_row).astype(vt_ref.dtype)
    qt_ref[...] = (_dot(wt_ref[0:QK_W, :], xb, _NT) * (rstd_row * (DQK ** -0.5))).astype(qt_ref.dtype)
    k_ref[...] = (_dot(xb, wt_ref[QK_W:2 * QK_W, :], _NT) * rstd).astype(k_ref.dtype)


def _proj_layer(h2, w_t, head_norm_b, layer):
    T, D = h2.shape
    tm = PROJ_TILE
    tok = lambda i: (i, 0)
    lanes = lambda i: (0, i)
    tile = lambda i: (i, 0, 0)
    return pl.pallas_call(
        _proj_kernel,
        grid=(T // tm,),
        in_specs=[
            pl.BlockSpec((tm, D), tok),
            _resident(w_t.shape[1:], layer),
            _resident(head_norm_b.shape[1:], layer),
        ],
        out_specs=[
            pl.BlockSpec((None, QK_W, tm), tile),
            pl.BlockSpec((tm, QK_W), tok),
            pl.BlockSpec((None, V_W, tm), tile),
            pl.BlockSpec((None, V_W, tm), tile),
            pl.BlockSpec((N_GATES, tm), lanes),
        ],
        out_shape=[
            jax.ShapeDtypeStruct((T // tm, QK_W, tm), jnp.bfloat16),
            jax.ShapeDtypeStruct((T, QK_W), jnp.bfloat16),
            jax.ShapeDtypeStruct((T // tm, V_W, tm), jnp.bfloat16),
            jax.ShapeDtypeStruct((T // tm, V_W, tm), jnp.bfloat16),
            jax.ShapeDtypeStruct((N_GATES, T), jnp.float32),
        ],
        compiler_params=_params("arbitrary"),
        name="mlstm_proj",
    )(h2, w_t, head_norm_b)


def _segmented_scan(x, op, lane, seg):
    shift = 1
    while shift < seg:
        moved = pltpu.roll(x, shift, 1)
        x = jnp.where((lane % seg) >= shift, op(x, moved), x)
        shift *= 2
    return x


def _gate_kernel(gt_ref, bias_ref, r_ref):
    L = MLSTM_CHUNK
    B, _, S = r_ref.shape
    g = gt_ref[...] + bias_ref[...]
    per_seq = lambda lo: jnp.concatenate(
        [g[lo:lo + N_HEADS, s * S:(s + 1) * S] for s in range(B)], axis=0)
    i_pre = per_seq(0)
    f_pre = per_seq(N_HEADS)
    logf = jnp.minimum(f_pre, 0.0) - jnp.log1p(jnp.exp(-jnp.abs(f_pre)))
    lane = lax.broadcasted_iota(jnp.int32, (B * N_HEADS, S), 1)
    b = _segmented_scan(logf, jnp.add, lane, L)
    a = i_pre - b
    amax = _segmented_scan(a, jnp.maximum, lane, L)

    def put(kind, lanes, value):
        for s in range(B):
            r_ref[s, kind * N_HEADS:(kind + 1) * N_HEADS, lanes] = value[s * N_HEADS:(s + 1) * N_HEADS]

    put(0, slice(0, S), a)
    r_ref[:, 5 * N_HEADS:N_GATE_ROWS, :] = jnp.zeros((B, N_GATE_ROWS - 5 * N_HEADS, S), jnp.float32)
    m_prev = jnp.zeros((B * N_HEADS, 1), jnp.float32)
    for c in range(S // L):
        seg = slice(c * L, (c + 1) * L)
        big_m = jnp.maximum(m_prev, amax[:, seg])
        m_last = big_m[:, L - 1:L]
        put(1, seg, big_m)
        put(2, seg, -(b[:, seg] + big_m))
        put(3, seg, m_prev - big_m)
        put(4, seg, a[:, seg] - m_last)
        m_prev = b[:, (c + 1) * L - 1:(c + 1) * L] + m_last


def _gate_layer(gates_t, bias, B, S, layer):
    return pl.pallas_call(
        _gate_kernel,
        grid=(1,),
        in_specs=[
            pl.BlockSpec((N_GATES, B * S), lambda i: (0, 0)),
            _resident((N_GATES, 1), layer),
        ],
        out_specs=pl.BlockSpec((B, N_GATE_ROWS, S), lambda i: (0, 0, 0)),
        out_shape=jax.ShapeDtypeStruct((B, N_GATE_ROWS, S), jnp.float32),
        compiler_params=_params("arbitrary"),
        name="mlstm_gates",
    )(gates_t, bias)


def _hi_lo_rows(row):
    hi = row.astype(jnp.bfloat16).astype(jnp.float32)
    idx = lax.broadcasted_iota(jnp.int32, (16, row.shape[1]), 0)
    slab = jnp.where(idx == 0, hi, jnp.where(idx == 1, row - hi, 0.0))
    return slab.astype(jnp.bfloat16)


def _chunk_kernel(qt_ref, k_ref, vt_ref, ogt_ref, r_ref, out_ref, c_ref, n_ref):
    L = MLSTM_CHUNK
    H = L // 2

    @pl.when(pl.program_id(1) == 0)
    def _():
        c_ref[...] = jnp.zeros(c_ref.shape, jnp.float32)
        n_ref[...] = jnp.zeros(n_ref.shape, jnp.float32)

    causal = (lax.broadcasted_iota(jnp.int32, (H, H), 0)
              <= lax.broadcasted_iota(jnp.int32, (H, H), 1))

    tok = lambda c: slice(c * L, (c + 1) * L)
    qk = lambda h: slice(h * DQK, (h + 1) * DQK)
    vv = lambda h: slice(h * DV, (h + 1) * DV)
    units = [(c, h) for c in range(CHUNKS_PER_STEP) for h in range(N_HEADS)]
    rows = [r_ref[0, :, tok(c)] for c in range(CHUNKS_PER_STEP)]
    a_cols = [r[0:2 * N_HEADS, :].T for r in rows]
    gate_row = lambda c, h, kind: rows[c][kind * N_HEADS + h:kind * N_HEADS + h + 1, :]
    ct = [c_ref[h] for h in range(N_HEADS)]
    n = [n_ref[h] for h in range(N_HEADS)]

    def scores(c, h):
        return _dot(k_ref[tok(c), qk(h)], qt_ref[qk(h), tok(c)])

    def weigh(c, h, s):
        a_col = a_cols[c][:, h:h + 1]
        big_m = gate_row(c, h, 1)
        diag = lambda i: jnp.exp(jnp.where(
            causal, a_col[i * H:(i + 1) * H, :] - big_m[:, i * H:(i + 1) * H], -jnp.inf))
        above = jnp.exp(a_col[0:H, :] - big_m[:, H:L])
        top = s[0:H, :] * jnp.concatenate([diag(0), above], axis=1)
        bottom = jnp.concatenate([jnp.zeros((H, H), jnp.float32), s[H:L, H:L] * diag(1)], axis=1)
        st = jnp.concatenate([top, bottom], axis=0)
        den_intra = jnp.sum(st, axis=0, keepdims=True)
        num_intra = _dot(vt_ref[vv(h), tok(c)], st.astype(jnp.bfloat16))
        state = jnp.concatenate([ct[h].astype(jnp.bfloat16), _hi_lo_rows(n[h])], axis=0)
        from_state = _dot(state, qt_ref[qk(h), tok(c)])
        return den_intra, num_intra, from_state

    def finish(c, h, den_intra, num_intra, from_state):
        inter = jnp.exp(gate_row(c, h, 3))
        w = jnp.exp(gate_row(c, h, 4))
        qn = from_state[DV:DV + 1, :] + from_state[DV + 1:DV + 2, :]
        num = inter * from_state[0:DV, :] + num_intra
        den = inter * qn + den_intra
        r = 1.0 / jnp.maximum(jnp.abs(den), jnp.exp(gate_row(c, h, 2)))
        ss = jnp.sum(num * num, axis=0, keepdims=True)
        scale = r * lax.rsqrt(r * r * ss * (1.0 / DV) + EPS)
        gated = num * scale * ogt_ref[vv(h), tok(c)].astype(jnp.float32)
        lane0 = (c * L) % FFN_TILE
        out_ref[(c * L) // FFN_TILE, vv(h), lane0:lane0 + L] = gated.astype(out_ref.dtype)

        vw = jnp.concatenate([vt_ref[vv(h), tok(c)] * w.astype(jnp.bfloat16), _hi_lo_rows(w)], axis=0)
        upd = _dot(vw, k_ref[tok(c), qk(h)])
        decay = inter[:, L - 1:L]
        ct[h] = decay * ct[h] + upd[0:DV, :]
        n[h] = decay * n[h] + upd[DV:DV + 1, :] + upd[DV + 1:DV + 2, :]

    scored = {0: scores(*units[0])}
    if len(units) > 1:
        scored[1] = scores(*units[1])
    weighed = {0: weigh(*units[0], scored.pop(0))}
    for i, u in enumerate(units):
        if i + 2 < len(units):
            scored[i + 2] = scores(*units[i + 2])
        if i + 1 < len(units):
            weighed[i + 1] = weigh(*units[i + 1], scored.pop(i + 1))
        finish(*u, *weighed.pop(i))

    for h in range(N_HEADS):
        c_ref[h] = ct[h]
        n_ref[h] = n[h]


def _chunk_layer(qt, k, vt, ogt, rows, B, S):
    T = k.shape[0]
    step = MLSTM_CHUNK * CHUNKS_PER_STEP
    assert step == PROJ_TILE and step % FFN_TILE == 0
    ns = S // step
    tok = lambda b, c: (b * ns + c, 0)
    tile = lambda b, c: (b * ns + c, 0, 0)
    return pl.pallas_call(
        _chunk_kernel,
        grid=(B, ns),
        in_specs=[
            pl.BlockSpec((None, QK_W, step), tile),
            pl.BlockSpec((step, QK_W), tok),
            pl.BlockSpec((None, V_W, step), tile),
            pl.BlockSpec((None, V_W, step), tile),
            pl.BlockSpec((1, N_GATE_ROWS, step), lambda b, c: (b, 0, c)),
        ],
        out_specs=pl.BlockSpec((step // FFN_TILE, V_W, FFN_TILE), tile),
        out_shape=jax.ShapeDtypeStruct((T // FFN_TILE, V_W, FFN_TILE), jnp.bfloat16),
        scratch_shapes=[
            pltpu.VMEM((N_HEADS, DV, DQK), jnp.float32),
            pltpu.VMEM((N_HEADS, 1, DQK), jnp.float32),
        ],
        compiler_params=_params("arbitrary", "arbitrary"),
        name="mlstm_chunk",
    )(qt, k, vt, ogt, rows)


def kernel(x, pool_norm, pool_w, pool_scale, mlstm_norm, mlstm_w_in, mlstm_gate_bias,
           mlstm_head_norm, mlstm_w_out, ffn_norm, ffn_w_in, ffn_w_out, final_norm):
    B, S, D = x.shape
    depth = ffn_norm.shape[0]
    bf16 = jnp.bfloat16
    groups = pool_w.shape[:2] + (POOL_GROUP_DIM,)
    pool_w_b = (pool_norm.reshape(groups)[..., :, None] * pool_w
                * pool_scale.reshape(groups)[..., None, :]).astype(bf16)
    w_t_b = jnp.swapaxes(
        jnp.pad(mlstm_w_in * mlstm_norm[:, :, None], ((0, 0), (0, 0), (0, W_T_ROWS - mlstm_w_in.shape[2]))),
        1, 2).astype(bf16)
    head_norm_b = jnp.broadcast_to(mlstm_head_norm[:, :, None], mlstm_head_norm.shape + (LANE,))
    gate_bias = mlstm_gate_bias.reshape(-1, N_GATES, 1)
    w_mix_b = mlstm_w_out.astype(bf16)
    fgain = final_norm.reshape(1, D)
    ffn_gain_b = jnp.broadcast_to(ffn_norm[:, :, None], ffn_norm.shape + (LANE,))

    h2 = x.reshape(B * S, D)
    for i in range(depth):
        j = i // 2
        last = i == depth - 1
        if i % 2 == 0:
            h2 = _ffn_layer(h2, S, ffn_gain_b, ffn_w_in, ffn_w_out, fgain, i, last,
                            pool_w=pool_w_b, mix_layer=j)
        else:
            qt, k, vt, ogt, gates_t = _proj_layer(h2, w_t_b, head_norm_b, j)
            rows = _gate_layer(gates_t, gate_bias, B, S, j)
            heads = _chunk_layer(qt, k, vt, ogt, rows, B, S)
            h2 = _ffn_layer(h2, S, ffn_gain_b, ffn_w_in, ffn_w_out, fgain, i, last,
                            heads=heads, w_mix=w_mix_b, mix_layer=j)
    return h2.reshape(B, S, D)
```

```python
import functools
import math

import jax
import jax.numpy as jnp
from jax import lax
from jax.experimental import pallas as pl
from jax.experimental.pallas import tpu as pltpu

D_MODEL = 1024
POOL_WINDOWS = (2, 4, 8, 16)
POOL_GROUP_DIM = D_MODEL // len(POOL_WINDOWS)
POOL_HALO = 32

N_HEADS = 4
DV = D_MODEL // N_HEADS
DQK = DV // 2
QK_W = N_HEADS * DQK
V_W = N_HEADS * DV
MAIN_W = 2 * QK_W + 2 * V_W
N_GATES = 2 * N_HEADS
W_T_ROWS = MAIN_W + 16

D_FF = int(math.ceil(8 * D_MODEL / 3 / 256) * 256)
FF_CHUNK = 256
LANE = 128

EPS = 1e-6

POOL_BLOCK = 128
FFN_TILE = 512
PROJ_TILE = 1024
MLSTM_CHUNK = 256
CHUNKS_PER_STEP = 8
N_GATE_ROWS = 24

VMEM_LIMIT = 56 * 1024 * 1024

_NT = (((1,), (1,)), ((), ()))


def _params(*semantics):
    return pltpu.CompilerParams(dimension_semantics=semantics,
                                vmem_limit_bytes=VMEM_LIMIT)


def _resident(shape, layer=None):
    zeros = (0,) * len(shape)
    if layer is None:
        return pl.BlockSpec(shape, lambda *_: zeros, pipeline_mode=pl.Buffered(1))
    return pl.BlockSpec((None,) + shape, lambda *_: (layer,) + zeros,
                        pipeline_mode=pl.Buffered(1))


def _rms_norm(x, gain):
    return x * lax.rsqrt(jnp.mean(x * x, axis=-1, keepdims=True) + EPS) * gain


def _dot(a, b, dims=None):
    if dims is None:
        return jnp.dot(a, b, preferred_element_type=jnp.float32)
    return lax.dot_general(a, b, dims, preferred_element_type=jnp.float32)


def _pool_rows(x_ref, r0, r1, tile_in_seq, ext_ref, stage_refs, pooled_ref):
    gd = POOL_GROUP_DIM
    lo, hi = POOL_HALO + r0, POOL_HALO + r1
    x = x_ref[r0:r1, :]
    xn = x * lax.rsqrt(jnp.mean(x * x, axis=-1, keepdims=True) + EPS)
    ext_ref[lo:hi, :] = xn

    prev = ext_ref
    for j, cur in enumerate(stage_refs, start=1):
        shift = 2 ** (j - 1)
        start = 8 * j if r0 == 0 else lo
        skip = 0 if j == 1 else gd
        cur[start:hi, :] = prev[start:hi, skip:] + prev[start - shift:hi - shift, skip:]
        prev = cur

    ts = x_ref.shape[0]
    pos = (tile_in_seq * ts + r0 + 1
           + lax.broadcasted_iota(jnp.int32, (r1 - r0, 1), 0)).astype(jnp.float32)
    for g, win in enumerate(POOL_WINDOWS):
        cols = slice(g * gd, (g + 1) * gd)
        inv_cnt = 1.0 / jnp.minimum(pos, float(win))
        pooled = stage_refs[g][lo:hi, 0:gd] * inv_cnt - xn[:, cols]
        pooled_ref[r0:r1, cols] = pooled.astype(jnp.bfloat16)


def _pool_finish(x_ref, w_ref, ext_ref, pooled_ref, out_ref):
    ts = x_ref.shape[0]
    gd = POOL_GROUP_DIM
    for g in range(len(POOL_WINDOWS)):
        cols = slice(g * gd, (g + 1) * gd)
        out_ref[:, cols] = x_ref[:, cols] + _dot(pooled_ref[:, cols], w_ref[g])
    ext_ref[0:POOL_HALO, :] = ext_ref[ts:ts + POOL_HALO, :]


def _pool_jobs(x_ref, tile_in_seq, w_ref, ext_ref, stage_refs, pooled_ref, out_ref):
    jobs = [functools.partial(_pool_rows, x_ref, r0, r0 + POOL_BLOCK, tile_in_seq, ext_ref,
                              stage_refs, pooled_ref)
            for r0 in range(0, x_ref.shape[0], POOL_BLOCK)]
    jobs.append(functools.partial(_pool_finish, x_ref, w_ref, ext_ref, pooled_ref, out_ref))
    return jobs


def _ffn_kernel(*refs, layer, mixer_proj, final_norm, tiles_per_seq):
    if mixer_proj:
        x_ref, heads_ref, w_mix_ref, *refs = refs
    else:
        x_first_ref, x_next_ref, pool_w_ref, *refs = refs
    (gain_ref, w_in_hbm, w_out_hbm, fgain_ref, o_ref,
     w_in_ref, w_out_ref, act_ref, stage_in, stage_out, sem, *pool_scratch) = refs
    n_chunks = D_FF // FF_CHUNK
    step = pl.program_id(0)
    if not mixer_proj:
        mixed_ref, cur_ref, pooled_ref, ext_ref, *stage_refs = pool_scratch
        next_in_seq = (step + 1) % tiles_per_seq

        @pl.when(next_in_seq == 0)
        def _():
            ext_ref[0:POOL_HALO, :] = jnp.zeros((POOL_HALO, D_MODEL), jnp.float32)

    def chunk_copies(j, slot):
        lo = j * FF_CHUNK
        return (
            pltpu.make_async_copy(w_in_hbm.at[layer, :, lo:lo + FF_CHUNK],
                                  stage_in.at[slot, 0], sem.at[slot, 0]),
            pltpu.make_async_copy(w_in_hbm.at[layer, :, D_FF + lo:D_FF + lo + FF_CHUNK],
                                  stage_in.at[slot, 1], sem.at[slot, 1]),
            pltpu.make_async_copy(w_out_hbm.at[layer, lo:lo + FF_CHUNK, :],
                                  stage_out.at[slot], sem.at[slot, 2]),
        )

    def body(stream_weights):
        if stream_weights:
            for cp in chunk_copies(0, 0):
                cp.start()
        side_jobs = []
        if mixer_proj:
            x = x_ref[...] + _dot(heads_ref[...], w_mix_ref[...], (((0,), (0,)), ((), ())))
        else:
            if stream_weights:
                ext_ref[0:POOL_HALO, :] = jnp.zeros((POOL_HALO, D_MODEL), jnp.float32)
                for job in _pool_jobs(x_first_ref, 0, pool_w_ref, ext_ref, stage_refs, pooled_ref,
                                      mixed_ref):
                    job()
            cur_ref[...] = mixed_ref[...]
            x = cur_ref[...]
            side_jobs = _pool_jobs(x_next_ref, next_in_seq, pool_w_ref, ext_ref, stage_refs,
                                   pooled_ref, mixed_ref)
        xb = x.astype(jnp.bfloat16)
        rstd = lax.rsqrt(jnp.mean(x * x, axis=-1, keepdims=True) + EPS)
        for j in range(n_chunks):
            lo = j * FF_CHUNK
            if stream_weights:
                slot = j % 2
                if j + 1 < n_chunks:
                    for cp in chunk_copies(j + 1, 1 - slot):
                        cp.start()
                for cp in chunk_copies(j, slot):
                    cp.wait()
                for part, col in ((0, lo), (1, D_FF + lo)):
                    for sub in range(0, FF_CHUNK, LANE):
                        w_in_ref[:, col + sub:col + sub + LANE] = (
                            stage_in[slot, part, :, sub:sub + LANE] * gain_ref[...]).astype(jnp.bfloat16)
                w_out_ref[lo:lo + FF_CHUNK, :] = stage_out[slot].astype(jnp.bfloat16)
            gate = _dot(xb, w_in_ref[:, lo:lo + FF_CHUNK]) * rstd
            up = _dot(xb, w_in_ref[:, D_FF + lo:D_FF + lo + FF_CHUNK]) * rstd
            act_ref[:, lo:lo + FF_CHUNK] = (gate * jax.nn.sigmoid(gate) * up).astype(jnp.bfloat16)
            if j < len(side_jobs):
                side_jobs[j]()
        out = x + _dot(act_ref[...], w_out_ref[...])
        if final_norm:
            out = _rms_norm(out, fgain_ref[...])
        o_ref[...] = out

    first = step == 0
    pl.when(first)(functools.partial(body, True))
    pl.when(jnp.logical_not(first))(functools.partial(body, False))


def _ffn_layer(h2, S, gain, w_in, w_out, final_gain, layer, final_norm, heads=None, w_mix=None,
               pool_w=None, mix_layer=None):
    T, D = h2.shape
    tm = FFN_TILE
    nt = T // tm
    assert S // tm > 1 and tm % POOL_BLOCK == 0 and tm // POOL_BLOCK < D_FF // FF_CHUNK
    tok = pl.BlockSpec((tm, D), lambda i: (i, 0))
    hbm = pl.BlockSpec(memory_space=pl.ANY)
    mixer_proj = heads is not None
    if mixer_proj:
        mix_specs = [tok, pl.BlockSpec((None, D, tm), lambda i: (i, 0, 0)),
                     _resident(w_mix.shape[1:], mix_layer)]
        mix_args = (h2, heads, w_mix)
        pool_scratch = []
    else:
        mix_specs = [pl.BlockSpec((tm, D), lambda i: (0, 0), pipeline_mode=pl.Buffered(1)),
                     pl.BlockSpec((tm, D), lambda i: (jnp.minimum(i + 1, nt - 1), 0)),
                     _resident(pool_w.shape[1:], mix_layer)]
        mix_args = (h2, h2, pool_w)
        pool_scratch = [
            pltpu.VMEM((tm, D), jnp.float32),
            pltpu.VMEM((tm, D), jnp.float32),
            pltpu.VMEM((tm, D), jnp.bfloat16),
            pltpu.VMEM((POOL_HALO + tm, D), jnp.float32),
        ] + [pltpu.VMEM((POOL_HALO + tm, D - g * POOL_GROUP_DIM), jnp.float32)
             for g in range(len(POOL_WINDOWS))]
    return pl.pallas_call(
        functools.partial(_ffn_kernel, layer=layer, mixer_proj=mixer_proj, final_norm=final_norm,
                          tiles_per_seq=S // tm),
        grid=(nt,),
        in_specs=mix_specs + [_resident((D, LANE), layer), hbm, hbm, _resident((1, D))],
        out_specs=tok,
        out_shape=jax.ShapeDtypeStruct(h2.shape, h2.dtype),
        scratch_shapes=[
            pltpu.VMEM((D, 2 * D_FF), jnp.bfloat16),
            pltpu.VMEM((D_FF, D), jnp.bfloat16),
            pltpu.VMEM((tm, D_FF), jnp.bfloat16),
            pltpu.VMEM((2, 2, D, FF_CHUNK), jnp.float32),
            pltpu.VMEM((2, FF_CHUNK, D), jnp.float32),
            pltpu.SemaphoreType.DMA((2, 3)),
        ] + pool_scratch,
        compiler_params=_params("arbitrary"),
        name="swiglu" + ("_mix" if mixer_proj else "_pool") + ("_final" if final_norm else ""),
    )(*mix_args, gain, w_in, w_out, final_gain)


def _proj_kernel(x_ref, wt_ref, hn_ref, qt_ref, k_ref, vt_ref, ogt_ref, gt_ref):
    x = x_ref[...]
    xb = x.astype(jnp.bfloat16)
    rstd = lax.rsqrt(jnp.mean(x * x, axis=-1, keepdims=True) + EPS)
    rstd_row = jnp.broadcast_to(rstd, (x.shape[0], LANE)).T[0:1, :]
    og = _dot(wt_ref[2 * QK_W + V_W:, :], xb, _NT) * rstd_row
    gt_ref[...] = og[V_W:V_W + N_GATES, :]
    for lo in range(0, og.shape[1], LANE):
        ogt_ref[:, lo:lo + LANE] = (jax.nn.sigmoid(og[0:V_W, lo:lo + LANE]) * hn_ref[...]).astype(ogt_ref.dtype)
    vt_ref[...] = (_dot(wt_ref[2 * QK_W:2 * QK_W + V_W, :], xb, _NT) * rstd_row).astype(vt_ref.dtype)
    qt_ref[...] = (_dot(wt_ref[0:QK_W, :], xb, _NT) * (rstd_row * (DQK ** -0.5))).astype(qt_ref.dtype)
    k_ref[...] = (_dot(xb, wt_ref[QK_W:2 * QK_W, :], _NT) * rstd).astype(k_ref.dtype)


def _proj_layer(h2, w_t, head_norm_b, layer):
    T, D = h2.shape
    tm = PROJ_TILE
    tok = lambda i: (i, 0)
    lanes = lambda i: (0, i)
    tile = lambda i: (i, 0, 0)
    return pl.pallas_call(
        _proj_kernel,
        grid=(T // tm,),
        in_specs=[
            pl.BlockSpec((tm, D), tok),
            _resident(w_t.shape[1:], layer),
            _resident(head_norm_b.shape[1:], layer),
        ],
        out_specs=[
            pl.BlockSpec((None, QK_W, tm), tile),
            pl.BlockSpec((tm, QK_W), tok),
            pl.BlockSpec((None, V_W, tm), tile),
            pl.BlockSpec((None, V_W, tm), tile),
            pl.BlockSpec((N_GATES, tm), lanes),
        ],
        out_shape=[
            jax.ShapeDtypeStruct((T // tm, QK_W, tm), jnp.bfloat16),
            jax.ShapeDtypeStruct((T, QK_W), jnp.bfloat16),
            jax.ShapeDtypeStruct((T // tm, V_W, tm), jnp.bfloat16),
            jax.ShapeDtypeStruct((T // tm, V_W, tm), jnp.bfloat16),
            jax.ShapeDtypeStruct((N_GATES, T), jnp.float32),
        ],
        compiler_params=_params("arbitrary"),
        name="mlstm_proj",
    )(h2, w_t, head_norm_b)


def _segmented_scan(x, op, lane, seg):
    shift = 1
    while shift < seg:
        moved = pltpu.roll(x, shift, 1)
        x = jnp.where((lane % seg) >= shift, op(x, moved), x)
        shift *= 2
    return x


def _gate_kernel(gt_ref, bias_ref, r_ref):
    L = MLSTM_CHUNK
    B, _, S = r_ref.shape
    g = gt_ref[...] + bias_ref[...]
    per_seq = lambda lo: jnp.concatenate(
        [g[lo:lo + N_HEADS, s * S:(s + 1) * S] for s in range(B)], axis=0)
    i_pre = per_seq(0)
    f_pre = per_seq(N_HEADS)
    logf = jnp.minimum(f_pre, 0.0) - jnp.log1p(jnp.exp(-jnp.abs(f_pre)))
    lane = lax.broadcasted_iota(jnp.int32, (B * N_HEADS, S), 1)
    b = _segmented_scan(logf, jnp.add, lane, L)
    a = i_pre - b
    amax = _segmented_scan(a, jnp.maximum, lane, L)

    def put(kind, lanes, value):
        for s in range(B):
            r_ref[s, kind * N_HEADS:(kind + 1) * N_HEADS, lanes] = value[s * N_HEADS:(s + 1) * N_HEADS]

    put(0, slice(0, S), a)
    r_ref[:, 5 * N_HEADS:N_GATE_ROWS, :] = jnp.zeros((B, N_GATE_ROWS - 5 * N_HEADS, S), jnp.float32)
    m_prev = jnp.zeros((B * N_HEADS, 1), jnp.float32)
    for c in range(S // L):
        seg = slice(c * L, (c + 1) * L)
        big_m = jnp.maximum(m_prev, amax[:, seg])
        m_last = big_m[:, L - 1:L]
        put(1, seg, big_m)
        put(2, seg, -(b[:, seg] + big_m))
        put(3, seg, m_prev - big_m)
        put(4, seg, a[:, seg] - m_last)
        m_prev = b[:, (c + 1) * L - 1:(c + 1) * L] + m_last


def _gate_layer(gates_t, bias, B, S, layer):
    return pl.pallas_call(
        _gate_kernel,
        grid=(1,),
        in_specs=[
            pl.BlockSpec((N_GATES, B * S), lambda i: (0, 0)),
            _resident((N_GATES, 1), layer),
        ],
        out_specs=pl.BlockSpec((B, N_GATE_ROWS, S), lambda i: (0, 0, 0)),
        out_shape=jax.ShapeDtypeStruct((B, N_GATE_ROWS, S), jnp.float32),
        compiler_params=_params("arbitrary"),
        name="mlstm_gates",
    )(gates_t, bias)


def _hi_lo_rows(row):
    hi = row.astype(jnp.bfloat16).astype(jnp.float32)
    idx = lax.broadcasted_iota(jnp.int32, (16, row.shape[1]), 0)
    slab = jnp.where(idx == 0, hi, jnp.where(idx == 1, row - hi, 0.0))
    return slab.astype(jnp.bfloat16)


def _chunk_kernel(qt_ref, k_ref, vt_ref, ogt_ref, r_ref, out_ref, c_ref, n_ref):
    L = MLSTM_CHUNK
    H = L // 2

    @pl.when(pl.program_id(1) == 0)
    def _():
        c_ref[...] = jnp.zeros(c_ref.shape, jnp.float32)
        n_ref[...] = jnp.zeros(n_ref.shape, jnp.float32)

    causal = (lax.broadcasted_iota(jnp.int32, (H, H), 0)
              <= lax.broadcasted_iota(jnp.int32, (H, H), 1))

    tok = lambda c: slice(c * L, (c + 1) * L)
    lanes = lambda ref, rows, c: ref[(c * L) // PROJ_TILE, rows,
                                     (c * L) % PROJ_TILE:(c * L) % PROJ_TILE + L]
    qk = lambda h: slice(h * DQK, (h + 1) * DQK)
    vv = lambda h: slice(h * DV, (h + 1) * DV)
    units = [(c, h) for c in range(CHUNKS_PER_STEP) for h in range(N_HEADS)]
    rows = [r_ref[0, :, tok(c)] for c in range(CHUNKS_PER_STEP)]
    a_cols = [r[0:2 * N_HEADS, :].T for r in rows]
    gate_row = lambda c, h, kind: rows[c][kind * N_HEADS + h:kind * N_HEADS + h + 1, :]
    ct = [c_ref[h] for h in range(N_HEADS)]
    n = [n_ref[h] for h in range(N_HEADS)]

    def scores(c, h):
        return _dot(k_ref[tok(c), qk(h)], lanes(qt_ref, qk(h), c))

    def weigh(c, h, s):
        a_col = a_cols[c][:, h:h + 1]
        big_m = gate_row(c, h, 1)
        diag = lambda i: jnp.exp(jnp.where(
            causal, a_col[i * H:(i + 1) * H, :] - big_m[:, i * H:(i + 1) * H], -jnp.inf))
        above = jnp.exp(a_col[0:H, :] - big_m[:, H:L])
        top = s[0:H, :] * jnp.concatenate([diag(0), above], axis=1)
        bottom = jnp.concatenate([jnp.zeros((H, H), jnp.float32), s[H:L, H:L] * diag(1)], axis=1)
        st = jnp.concatenate([top, bottom], axis=0)
        den_intra = jnp.sum(st, axis=0, keepdims=True)
        num_intra = _dot(lanes(vt_ref, vv(h), c), st.astype(jnp.bfloat16))
        state = jnp.concatenate([ct[h].astype(jnp.bfloat16), _hi_lo_rows(n[h])], axis=0)
        from_state = _dot(state, lanes(qt_ref, qk(h), c))
        return den_intra, num_intra, from_state

    def finish(c, h, den_intra, num_intra, from_state):
        inter = jnp.exp(gate_row(c, h, 3))
        w = jnp.exp(gate_row(c, h, 4))
        qn = from_state[DV:DV + 1, :] + from_state[DV + 1:DV + 2, :]
        num = inter * from_state[0:DV, :] + num_intra
        den = inter * qn + den_intra
        r = 1.0 / jnp.maximum(jnp.abs(den), jnp.exp(gate_row(c, h, 2)))
        ss = jnp.sum(num * num, axis=0, keepdims=True)
        scale = r * lax.rsqrt(r * r * ss * (1.0 / DV) + EPS)
        gated = num * scale * lanes(ogt_ref, vv(h), c).astype(jnp.float32)
        lane0 = (c * L) % FFN_TILE
        out_ref[(c * L) // FFN_TILE, vv(h), lane0:lane0 + L] = gated.astype(out_ref.dtype)

        vw = jnp.concatenate([lanes(vt_ref, vv(h), c) * w.astype(jnp.bfloat16), _hi_lo_rows(w)], axis=0)
        upd = _dot(vw, k_ref[tok(c), qk(h)])
        decay = inter[:, L - 1:L]
        ct[h] = decay * ct[h] + upd[0:DV, :]
        n[h] = decay * n[h] + upd[DV:DV + 1, :] + upd[DV + 1:DV + 2, :]

    scored = {0: scores(*units[0])}
    if len(units) > 1:
        scored[1] = scores(*units[1])
    weighed = {0: weigh(*units[0], scored.pop(0))}
    for i, u in enumerate(units):
        if i + 2 < len(units):
            scored[i + 2] = scores(*units[i + 2])
        if i + 1 < len(units):
            weighed[i + 1] = weigh(*units[i + 1], scored.pop(i + 1))
        finish(*u, *weighed.pop(i))

    for h in range(N_HEADS):
        c_ref[h] = ct[h]
        n_ref[h] = n[h]


def _chunk_layer(qt, k, vt, ogt, rows, B, S):
    T = k.shape[0]
    step = MLSTM_CHUNK * CHUNKS_PER_STEP
    assert step % PROJ_TILE == 0 and step % FFN_TILE == 0
    pt = step // PROJ_TILE
    ns = S // step
    tok = lambda b, c: (b * ns + c, 0)
    tile = lambda b, c: (b * ns + c, 0, 0)
    return pl.pallas_call(
        _chunk_kernel,
        grid=(B, ns),
        in_specs=[
            pl.BlockSpec((pt, QK_W, PROJ_TILE), tile),
            pl.BlockSpec((step, QK_W), tok),
            pl.BlockSpec((pt, V_W, PROJ_TILE), tile),
            pl.BlockSpec((pt, V_W, PROJ_TILE), tile),
            pl.BlockSpec((1, N_GATE_ROWS, step), lambda b, c: (b, 0, c)),
        ],
        out_specs=pl.BlockSpec((step // FFN_TILE, V_W, FFN_TILE), tile),
        out_shape=jax.ShapeDtypeStruct((T // FFN_TILE, V_W, FFN_TILE), jnp.bfloat16),
        scratch_shapes=[
            pltpu.VMEM((N_HEADS, DV, DQK), jnp.float32),
            pltpu.VMEM((N_HEADS, 1, DQK), jnp.float32),
        ],
        compiler_params=_params("arbitrary", "arbitrary"),
        name="mlstm_chunk",
    )(qt, k, vt, ogt, rows)


def kernel(x, pool_norm, pool_w, pool_scale, mlstm_norm, mlstm_w_in, mlstm_gate_bias,
           mlstm_head_norm, mlstm_w_out, ffn_norm, ffn_w_in, ffn_w_out, final_norm):
    B, S, D = x.shape
    depth = ffn_norm.shape[0]
    bf16 = jnp.bfloat16
    groups = pool_w.shape[:2] + (POOL_GROUP_DIM,)
    pool_w_b = (pool_norm.reshape(groups)[..., :, None] * pool_w
                * pool_scale.reshape(groups)[..., None, :]).astype(bf16)
    w_t_b = jnp.swapaxes(
        jnp.pad(mlstm_w_in * mlstm_norm[:, :, None], ((0, 0), (0, 0), (0, W_T_ROWS - mlstm_w_in.shape[2]))),
        1, 2).astype(bf16)
    head_norm_b = jnp.broadcast_to(mlstm_head_norm[:, :, None], mlstm_head_norm.shape + (LANE,))
    gate_bias = mlstm_gate_bias.reshape(-1, N_GATES, 1)
    w_mix_b = mlstm_w_out.astype(bf16)
    fgain = final_norm.reshape(1, D)
    ffn_gain_b = jnp.broadcast_to(ffn_norm[:, :, None], ffn_norm.shape + (LANE,))

    h2 = x.reshape(B * S, D)
    for i in range(depth):
        j = i // 2
        last = i == depth - 1
        if i % 2 == 0:
            h2 = _ffn_layer(h2, S, ffn_gain_b, ffn_w_in, ffn_w_out, fgain, i, last,
                            pool_w=pool_w_b, mix_layer=j)
        else:
            qt, k, vt, ogt, gates_t = _proj_layer(h2, w_t_b, head_norm_b, j)
            rows = _gate_layer(gates_t, gate_bias, B, S, j)
            heads = _chunk_layer(qt, k, vt, ogt, rows, B, S)
            h2 = _ffn_layer(h2, S, ffn_gain_b, ffn_w_in, ffn_w_out, fgain, i, last,
                            heads=heads, w_mix=w_mix_b, mix_layer=j)
    return h2.reshape(B, S, D)
```

```python
import functools
import math

import jax
import jax.numpy as jnp
from jax import lax
from jax.experimental import pallas as pl
from jax.experimental.pallas import tpu as pltpu

D_MODEL = 1024
POOL_WINDOWS = (2, 4, 8, 16)
POOL_GROUP_DIM = D_MODEL // len(POOL_WINDOWS)
POOL_HALO = 32

N_HEADS = 4
DV = D_MODEL // N_HEADS
DQK = DV // 2
QK_W = N_HEADS * DQK
V_W = N_HEADS * DV
MAIN_W = 2 * QK_W + 2 * V_W
N_GATES = 2 * N_HEADS
W_T_ROWS = MAIN_W + 16

D_FF = int(math.ceil(8 * D_MODEL / 3 / 256) * 256)
FF_CHUNK = 256
LANE = 128

EPS = 1e-6

POOL_BLOCK = 128
FFN_TILE = 512
PROJ_TILE = 1024
MLSTM_CHUNK = 256
N_GATE_ROWS = 24

VMEM_LIMIT = 56 * 1024 * 1024

_NT = (((1,), (1,)), ((), ()))


def _params(*semantics):
    return pltpu.CompilerParams(dimension_semantics=semantics,
                                vmem_limit_bytes=VMEM_LIMIT)


def _resident(shape, layer=None):
    zeros = (0,) * len(shape)
    if layer is None:
        return pl.BlockSpec(shape, lambda *_: zeros, pipeline_mode=pl.Buffered(1))
    return pl.BlockSpec((None,) + shape, lambda *_: (layer,) + zeros,
                        pipeline_mode=pl.Buffered(1))


def _rms_norm(x, gain):
    return x * lax.rsqrt(jnp.mean(x * x, axis=-1, keepdims=True) + EPS) * gain


def _dot(a, b, dims=None):
    if dims is None:
        return jnp.dot(a, b, preferred_element_type=jnp.float32)
    return lax.dot_general(a, b, dims, preferred_element_type=jnp.float32)


def _pool_rows(x_ref, r0, r1, tile_in_seq, ext_ref, stage_refs, pooled_ref):
    gd = POOL_GROUP_DIM
    lo, hi = POOL_HALO + r0, POOL_HALO + r1
    x = x_ref[r0:r1, :]
    xn = x * lax.rsqrt(jnp.mean(x * x, axis=-1, keepdims=True) + EPS)
    ext_ref[lo:hi, :] = xn

    prev = ext_ref
    for j, cur in enumerate(stage_refs, start=1):
        shift = 2 ** (j - 1)
        start = 8 * j if r0 == 0 else lo
        skip = 0 if j == 1 else gd
        cur[start:hi, :] = prev[start:hi, skip:] + prev[start - shift:hi - shift, skip:]
        prev = cur

    ts = x_ref.shape[0]
    pos = (tile_in_seq * ts + r0 + 1
           + lax.broadcasted_iota(jnp.int32, (r1 - r0, 1), 0)).astype(jnp.float32)
    for g, win in enumerate(POOL_WINDOWS):
        cols = slice(g * gd, (g + 1) * gd)
        inv_cnt = 1.0 / jnp.minimum(pos, float(win))
        pooled = stage_refs[g][lo:hi, 0:gd] * inv_cnt - xn[:, cols]
        pooled_ref[r0:r1, cols] = pooled.astype(jnp.bfloat16)


def _pool_finish(x_ref, w_ref, ext_ref, pooled_ref, out_ref):
    ts = x_ref.shape[0]
    gd = POOL_GROUP_DIM
    for g in range(len(POOL_WINDOWS)):
        cols = slice(g * gd, (g + 1) * gd)
        out_ref[:, cols] = x_ref[:, cols] + _dot(pooled_ref[:, cols], w_ref[g])
    ext_ref[0:POOL_HALO, :] = ext_ref[ts:ts + POOL_HALO, :]


def _pool_jobs(x_ref, tile_in_seq, w_ref, ext_ref, stage_refs, pooled_ref, out_ref):
    jobs = [functools.partial(_pool_rows, x_ref, r0, r0 + POOL_BLOCK, tile_in_seq, ext_ref,
                              stage_refs, pooled_ref)
            for r0 in range(0, x_ref.shape[0], POOL_BLOCK)]
    jobs.append(functools.partial(_pool_finish, x_ref, w_ref, ext_ref, pooled_ref, out_ref))
    return jobs


def _ffn_kernel(*refs, layer, mixer_proj, final_norm, tiles_per_seq):
    if mixer_proj:
        x_ref, qt_ref, k_ref, vt_ref, r_ref, og_ref, w_mix_ref, *refs = refs
    else:
        x_first_ref, x_next_ref, pool_w_ref, *refs = refs
    (gain_ref, w_in_hbm, w_out_hbm, fgain_ref, o_ref,
     w_in_ref, w_out_ref, act_ref, stage_in, stage_out, sem, *mixer_scratch) = refs
    n_chunks = D_FF // FF_CHUNK
    step = pl.program_id(0)
    if mixer_proj:
        heads_ref, c_ref, n_ref = mixer_scratch
        tiles_per_proj = PROJ_TILE // FFN_TILE

        @pl.when(step % tiles_per_seq == 0)
        def _():
            c_ref[...] = jnp.zeros(c_ref.shape, jnp.float32)
            n_ref[...] = jnp.zeros(n_ref.shape, jnp.float32)

        @pl.when(step % tiles_per_proj == 0)
        def _():
            _mlstm_chunks(qt_ref, k_ref, vt_ref, r_ref, heads_ref, c_ref, n_ref, PROJ_TILE // MLSTM_CHUNK)
    else:
        mixed_ref, cur_ref, pooled_ref, ext_ref, *stage_refs = mixer_scratch
        next_in_seq = (step + 1) % tiles_per_seq

        @pl.when(next_in_seq == 0)
        def _():
            ext_ref[0:POOL_HALO, :] = jnp.zeros((POOL_HALO, D_MODEL), jnp.float32)

    def chunk_copies(j, slot):
        lo = j * FF_CHUNK
        return (
            pltpu.make_async_copy(w_in_hbm.at[layer, :, lo:lo + FF_CHUNK],
                                  stage_in.at[slot, 0], sem.at[slot, 0]),
            pltpu.make_async_copy(w_in_hbm.at[layer, :, D_FF + lo:D_FF + lo + FF_CHUNK],
                                  stage_in.at[slot, 1], sem.at[slot, 1]),
            pltpu.make_async_copy(w_out_hbm.at[layer, lo:lo + FF_CHUNK, :],
                                  stage_out.at[slot], sem.at[slot, 2]),
        )

    def body(stream_weights):
        if stream_weights:
            for cp in chunk_copies(0, 0):
                cp.start()
        side_jobs = []
        if mixer_proj:
            heads = heads_ref[step % tiles_per_proj]
            x = x_ref[...] + _dot(heads * og_ref[...], w_mix_ref[...], (((0,), (0,)), ((), ())))
        else:
            if stream_weights:
                ext_ref[0:POOL_HALO, :] = jnp.zeros((POOL_HALO, D_MODEL), jnp.float32)
                for job in _pool_jobs(x_first_ref, 0, pool_w_ref, ext_ref, stage_refs, pooled_ref,
                                      mixed_ref):
                    job()
            cur_ref[...] = mixed_ref[...]
            x = cur_ref[...]
            side_jobs = _pool_jobs(x_next_ref, next_in_seq, pool_w_ref, ext_ref, stage_refs,
                                   pooled_ref, mixed_ref)
        xb = x.astype(jnp.bfloat16)
        rstd = lax.rsqrt(jnp.mean(x * x, axis=-1, keepdims=True) + EPS)
        for j in range(n_chunks):
            lo = j * FF_CHUNK
            if stream_weights:
                slot = j % 2
                if j + 1 < n_chunks:
                    for cp in chunk_copies(j + 1, 1 - slot):
                        cp.start()
                for cp in chunk_copies(j, slot):
                    cp.wait()
                for part, col in ((0, lo), (1, D_FF + lo)):
                    for sub in range(0, FF_CHUNK, LANE):
                        w_in_ref[:, col + sub:col + sub + LANE] = (
                            stage_in[slot, part, :, sub:sub + LANE] * gain_ref[...]).astype(jnp.bfloat16)
                w_out_ref[lo:lo + FF_CHUNK, :] = stage_out[slot].astype(jnp.bfloat16)
            gate = _dot(xb, w_in_ref[:, lo:lo + FF_CHUNK]) * rstd
            up = _dot(xb, w_in_ref[:, D_FF + lo:D_FF + lo + FF_CHUNK]) * rstd
            act_ref[:, lo:lo + FF_CHUNK] = (gate * jax.nn.sigmoid(gate) * up).astype(jnp.bfloat16)
            if j < len(side_jobs):
                side_jobs[j]()
        out = x + _dot(act_ref[...], w_out_ref[...])
        if final_norm:
            out = _rms_norm(out, fgain_ref[...])
        o_ref[...] = out

    first = step == 0
    pl.when(first)(functools.partial(body, True))
    pl.when(jnp.logical_not(first))(functools.partial(body, False))


def _ffn_layer(h2, S, gain, w_in, w_out, final_gain, layer, final_norm, mlstm=None, out_gate=None,
               w_mix=None, pool_w=None, mix_layer=None):
    T, D = h2.shape
    tm = FFN_TILE
    nt = T // tm
    assert S // tm > 1 and tm % POOL_BLOCK == 0 and tm // POOL_BLOCK < D_FF // FF_CHUNK
    tok = pl.BlockSpec((tm, D), lambda i: (i, 0))
    hbm = pl.BlockSpec(memory_space=pl.ANY)
    mixer_proj = mlstm is not None
    if mixer_proj:
        tpp = PROJ_TILE // tm
        ppq = S // PROJ_TILE
        assert PROJ_TILE % tm == 0 and S % PROJ_TILE == 0
        proj_tile = lambda i: (i // tpp, 0, 0)
        mix_specs = [
            tok,
            pl.BlockSpec((1, QK_W, PROJ_TILE), proj_tile),
            pl.BlockSpec((PROJ_TILE, QK_W), lambda i: (i // tpp, 0)),
            pl.BlockSpec((1, V_W, PROJ_TILE), proj_tile),
            pl.BlockSpec((1, N_GATE_ROWS, PROJ_TILE), lambda i: (i // tpp // ppq, 0, i // tpp % ppq)),
            pl.BlockSpec((None, D, tm), lambda i: (i, 0, 0)),
            _resident(w_mix.shape[1:], mix_layer),
        ]
        mix_args = (h2, *mlstm, out_gate, w_mix)
        mixer_scratch = [
            pltpu.VMEM((tpp, V_W, tm), jnp.bfloat16),
            pltpu.VMEM((N_HEADS, DV, DQK), jnp.float32),
            pltpu.VMEM((N_HEADS, 1, DQK), jnp.float32),
        ]
    else:
        mix_specs = [pl.BlockSpec((tm, D), lambda i: (0, 0), pipeline_mode=pl.Buffered(1)),
                     pl.BlockSpec((tm, D), lambda i: (jnp.minimum(i + 1, nt - 1), 0)),
                     _resident(pool_w.shape[1:], mix_layer)]
        mix_args = (h2, h2, pool_w)
        mixer_scratch = [
            pltpu.VMEM((tm, D), jnp.float32),
            pltpu.VMEM((tm, D), jnp.float32),
            pltpu.VMEM((tm, D), jnp.bfloat16),
            pltpu.VMEM((POOL_HALO + tm, D), jnp.float32),
        ] + [pltpu.VMEM((POOL_HALO + tm, D - g * POOL_GROUP_DIM), jnp.float32)
             for g in range(len(POOL_WINDOWS))]
    return pl.pallas_call(
        functools.partial(_ffn_kernel, layer=layer, mixer_proj=mixer_proj, final_norm=final_norm,
                          tiles_per_seq=S // tm),
        grid=(nt,),
        in_specs=mix_specs + [_resident((D, LANE), layer), hbm, hbm, _resident((1, D))],
        out_specs=tok,
        out_shape=jax.ShapeDtypeStruct(h2.shape, h2.dtype),
        scratch_shapes=[
            pltpu.VMEM((D, 2 * D_FF), jnp.bfloat16),
            pltpu.VMEM((D_FF, D), jnp.bfloat16),
            pltpu.VMEM((tm, D_FF), jnp.bfloat16),
            pltpu.VMEM((2, 2, D, FF_CHUNK), jnp.float32),
            pltpu.VMEM((2, FF_CHUNK, D), jnp.float32),
            pltpu.SemaphoreType.DMA((2, 3)),
        ] + mixer_scratch,
        compiler_params=_params("arbitrary"),
        name="swiglu" + ("_mix" if mixer_proj else "_pool") + ("_final" if final_norm else ""),
    )(*mix_args, gain, w_in, w_out, final_gain)


def _proj_kernel(x_ref, wt_ref, hn_ref, qt_ref, k_ref, vt_ref, ogt_ref, gt_ref):
    x = x_ref[...]
    xb = x.astype(jnp.bfloat16)
    rstd = lax.rsqrt(jnp.mean(x * x, axis=-1, keepdims=True) + EPS)
    rstd_row = jnp.broadcast_to(rstd, (x.shape[0], LANE)).T[0:1, :]
    og = _dot(wt_ref[2 * QK_W + V_W:, :], xb, _NT) * rstd_row
    gt_ref[...] = og[V_W:V_W + N_GATES, :]
    for lo in range(0, og.shape[1], LANE):
        ogt_ref[lo // FFN_TILE, :, lo % FFN_TILE:lo % FFN_TILE + LANE] = (
            jax.nn.sigmoid(og[0:V_W, lo:lo + LANE]) * hn_ref[...]).astype(ogt_ref.dtype)
    vt_ref[...] = (_dot(wt_ref[2 * QK_W:2 * QK_W + V_W, :], xb, _NT) * rstd_row).astype(vt_ref.dtype)
    qt_ref[...] = (_dot(wt_ref[0:QK_W, :], xb, _NT) * (rstd_row * (DQK ** -0.5))).astype(qt_ref.dtype)
    k_ref[...] = (_dot(xb, wt_ref[QK_W:2 * QK_W, :], _NT) * rstd).astype(k_ref.dtype)


def _proj_layer(h2, w_t, head_norm_b, layer):
    T, D = h2.shape
    tm = PROJ_TILE
    tok = lambda i: (i, 0)
    lanes = lambda i: (0, i)
    tile = lambda i: (i, 0, 0)
    return pl.pallas_call(
        _proj_kernel,
        grid=(T // tm,),
        in_specs=[
            pl.BlockSpec((tm, D), tok),
            _resident(w_t.shape[1:], layer),
            _resident(head_norm_b.shape[1:], layer),
        ],
        out_specs=[
            pl.BlockSpec((None, QK_W, tm), tile),
            pl.BlockSpec((tm, QK_W), tok),
            pl.BlockSpec((None, V_W, tm), tile),
            pl.BlockSpec((tm // FFN_TILE, V_W, FFN_TILE), tile),
            pl.BlockSpec((N_GATES, tm), lanes),
        ],
        out_shape=[
            jax.ShapeDtypeStruct((T // tm, QK_W, tm), jnp.bfloat16),
            jax.ShapeDtypeStruct((T, QK_W), jnp.bfloat16),
            jax.ShapeDtypeStruct((T // tm, V_W, tm), jnp.bfloat16),
            jax.ShapeDtypeStruct((T // FFN_TILE, V_W, FFN_TILE), jnp.bfloat16),
            jax.ShapeDtypeStruct((N_GATES, T), jnp.float32),
        ],
        compiler_params=_params("arbitrary"),
        name="mlstm_proj",
    )(h2, w_t, head_norm_b)


def _segmented_scan(x, op, lane, seg):
    shift = 1
    while shift < seg:
        moved = pltpu.roll(x, shift, 1)
        x = jnp.where((lane % seg) >= shift, op(x, moved), x)
        shift *= 2
    return x


def _gate_kernel(gt_ref, bias_ref, r_ref):
    L = MLSTM_CHUNK
    B, _, S = r_ref.shape
    g = gt_ref[...] + bias_ref[...]
    per_seq = lambda lo: jnp.concatenate(
        [g[lo:lo + N_HEADS, s * S:(s + 1) * S] for s in range(B)], axis=0)
    i_pre = per_seq(0)
    f_pre = per_seq(N_HEADS)
    logf = jnp.minimum(f_pre, 0.0) - jnp.log1p(jnp.exp(-jnp.abs(f_pre)))
    lane = lax.broadcasted_iota(jnp.int32, (B * N_HEADS, S), 1)
    b = _segmented_scan(logf, jnp.add, lane, L)
    a = i_pre - b
    amax = _segmented_scan(a, jnp.maximum, lane, L)

    def put(kind, lanes, value):
        for s in range(B):
            r_ref[s, kind * N_HEADS:(kind + 1) * N_HEADS, lanes] = value[s * N_HEADS:(s + 1) * N_HEADS]

    put(0, slice(0, S), a)
    r_ref[:, 5 * N_HEADS:N_GATE_ROWS, :] = jnp.zeros((B, N_GATE_ROWS - 5 * N_HEADS, S), jnp.float32)
    m_prev = jnp.zeros((B * N_HEADS, 1), jnp.float32)
    for c in range(S // L):
        seg = slice(c * L, (c + 1) * L)
        big_m = jnp.maximum(m_prev, amax[:, seg])
        m_last = big_m[:, L - 1:L]
        put(1, seg, big_m)
        put(2, seg, -(b[:, seg] + big_m))
        put(3, seg, m_prev - big_m)
        put(4, seg, a[:, seg] - m_last)
        m_prev = b[:, (c + 1) * L - 1:(c + 1) * L] + m_last


def _gate_layer(gates_t, bias, B, S, layer):
    return pl.pallas_call(
        _gate_kernel,
        grid=(1,),
        in_specs=[
            pl.BlockSpec((N_GATES, B * S), lambda i: (0, 0)),
            _resident((N_GATES, 1), layer),
        ],
        out_specs=pl.BlockSpec((B, N_GATE_ROWS, S), lambda i: (0, 0, 0)),
        out_shape=jax.ShapeDtypeStruct((B, N_GATE_ROWS, S), jnp.float32),
        compiler_params=_params("arbitrary"),
        name="mlstm_gates",
    )(gates_t, bias)


def _hi_lo_rows(row):
    hi = row.astype(jnp.bfloat16).astype(jnp.float32)
    idx = lax.broadcasted_iota(jnp.int32, (16, row.shape[1]), 0)
    slab = jnp.where(idx == 0, hi, jnp.where(idx == 1, row - hi, 0.0))
    return slab.astype(jnp.bfloat16)


def _mlstm_chunks(qt_ref, k_ref, vt_ref, r_ref, out_ref, c_ref, n_ref, n_chunks):
    L = MLSTM_CHUNK
    H = L // 2

    causal = (lax.broadcasted_iota(jnp.int32, (H, H), 0)
              <= lax.broadcasted_iota(jnp.int32, (H, H), 1))

    tok = lambda c: slice(c * L, (c + 1) * L)
    lanes = lambda ref, rows, c: ref[(c * L) // PROJ_TILE, rows,
                                     (c * L) % PROJ_TILE:(c * L) % PROJ_TILE + L]
    qk = lambda h: slice(h * DQK, (h + 1) * DQK)
    vv = lambda h: slice(h * DV, (h + 1) * DV)
    units = [(c, h) for c in range(n_chunks) for h in range(N_HEADS)]
    rows = [r_ref[0, :, tok(c)] for c in range(n_chunks)]
    a_cols = [r[0:2 * N_HEADS, :].T for r in rows]
    gate_row = lambda c, h, kind: rows[c][kind * N_HEADS + h:kind * N_HEADS + h + 1, :]
    ct = [c_ref[h] for h in range(N_HEADS)]
    n = [n_ref[h] for h in range(N_HEADS)]

    def scores(c, h):
        return _dot(k_ref[tok(c), qk(h)], lanes(qt_ref, qk(h), c))

    ones_rows = (lax.broadcasted_iota(jnp.int32, (16, L), 0) == 0).astype(jnp.bfloat16)

    def weigh(c, h, s):
        a_col = a_cols[c][:, h:h + 1]
        big_m = gate_row(c, h, 1)
        diag = lambda i: jnp.exp(jnp.where(
            causal, a_col[i * H:(i + 1) * H, :] - big_m[:, i * H:(i + 1) * H], -jnp.inf))
        above = jnp.exp(a_col[0:H, :] - big_m[:, H:L])
        top = s[0:H, :] * jnp.concatenate([diag(0), above], axis=1)
        bottom = jnp.concatenate([jnp.zeros((H, H), jnp.float32), s[H:L, H:L] * diag(1)], axis=1)
        st = jnp.concatenate([top, bottom], axis=0).astype(jnp.bfloat16)
        inter = jnp.exp(gate_row(c, h, 3))
        qt_i = lanes(qt_ref, qk(h), c) * inter.astype(jnp.bfloat16)
        lhs = jnp.concatenate([
            jnp.concatenate([ct[h].astype(jnp.bfloat16), _hi_lo_rows(n[h])], axis=0),
            jnp.concatenate([lanes(vt_ref, vv(h), c), ones_rows], axis=0)], axis=1)
        return (_dot(lhs, jnp.concatenate([qt_i, st], axis=0)),)

    def finish(c, h, mm):
        inter = jnp.exp(gate_row(c, h, 3))
        w = jnp.exp(gate_row(c, h, 4))
        num = mm[0:DV, :]
        den = mm[DV:DV + 1, :] + mm[DV + 1:DV + 2, :]
        r = 1.0 / jnp.maximum(jnp.abs(den), jnp.exp(gate_row(c, h, 2)))
        ss = jnp.sum(num * num, axis=0, keepdims=True)
        scale = r * lax.rsqrt(r * r * ss * (1.0 / DV) + EPS)
        lane0 = (c * L) % FFN_TILE
        out_ref[(c * L) // FFN_TILE, vv(h), lane0:lane0 + L] = (num * scale).astype(out_ref.dtype)

        vw = jnp.concatenate([lanes(vt_ref, vv(h), c) * w.astype(jnp.bfloat16), _hi_lo_rows(w)], axis=0)
        upd = _dot(vw, k_ref[tok(c), qk(h)])
        decay = inter[:, L - 1:L]
        ct[h] = decay * ct[h] + upd[0:DV, :]
        n[h] = decay * n[h] + upd[DV:DV + 1, :] + upd[DV + 1:DV + 2, :]

    scored = {0: scores(*units[0])}
    if len(units) > 1:
        scored[1] = scores(*units[1])
    weighed = {0: weigh(*units[0], scored.pop(0))}
    for i, u in enumerate(units):
        if i + 2 < len(units):
            scored[i + 2] = scores(*units[i + 2])
        if i + 1 < len(units):
            weighed[i + 1] = weigh(*units[i + 1], scored.pop(i + 1))
        finish(*u, *weighed.pop(i))

    for h in range(N_HEADS):
        c_ref[h] = ct[h]
        n_ref[h] = n[h]


def kernel(x, pool_norm, pool_w, pool_scale, mlstm_norm, mlstm_w_in, mlstm_gate_bias,
           mlstm_head_norm, mlstm_w_out, ffn_norm, ffn_w_in, ffn_w_out, final_norm):
    B, S, D = x.shape
    depth = ffn_norm.shape[0]
    bf16 = jnp.bfloat16
    groups = pool_w.shape[:2] + (POOL_GROUP_DIM,)
    pool_w_b = (pool_norm.reshape(groups)[..., :, None] * pool_w
                * pool_scale.reshape(groups)[..., None, :]).astype(bf16)
    w_t_b = jnp.swapaxes(
        jnp.pad(mlstm_w_in * mlstm_norm[:, :, None], ((0, 0), (0, 0), (0, W_T_ROWS - mlstm_w_in.shape[2]))),
        1, 2).astype(bf16)
    head_norm_b = jnp.broadcast_to(mlstm_head_norm[:, :, None], mlstm_head_norm.shape + (LANE,))
    gate_bias = mlstm_gate_bias.reshape(-1, N_GATES, 1)
    w_mix_b = mlstm_w_out.astype(bf16)
    fgain = final_norm.reshape(1, D)
    ffn_gain_b = jnp.broadcast_to(ffn_norm[:, :, None], ffn_norm.shape + (LANE,))

    h2 = x.reshape(B * S, D)
    for i in range(depth):
        j = i // 2
        last = i == depth - 1
        if i % 2 == 0:
            h2 = _ffn_layer(h2, S, ffn_gain_b, ffn_w_in, ffn_w_out, fgain, i, last,
                            pool_w=pool_w_b, mix_layer=j)
        else:
            qt, k, vt, ogt, gates_t = _proj_layer(h2, w_t_b, head_norm_b, j)
            rows = _gate_layer(gates_t, gate_bias, B, S, j)
            h2 = _ffn_layer(h2, S, ffn_gain_b, ffn_w_in, ffn_w_out, fgain, i, last,
                            mlstm=(qt, k, vt, rows), out_gate=ogt, w_mix=w_mix_b, mix_layer=j)
    return h2.reshape(B, S, D)
```

```python
import functools
import math

import jax
import jax.numpy as jnp
from jax import lax
from jax.experimental import pallas as pl
from jax.experimental.pallas import tpu as pltpu

D_MODEL = 1024
POOL_WINDOWS = (2, 4, 8, 16)
POOL_GROUP_DIM = D_MODEL // len(POOL_WINDOWS)
POOL_HALO = 32

N_HEADS = 4
DV = D_MODEL // N_HEADS
DQK = DV // 2
QK_W = N_HEADS * DQK
V_W = N_HEADS * DV
MAIN_W = 2 * QK_W + 2 * V_W
N_GATES = 2 * N_HEADS
W_T_ROWS = MAIN_W + 16

D_FF = int(math.ceil(8 * D_MODEL / 3 / 256) * 256)
FF_CHUNK = 256
LANE = 128

EPS = 1e-6

POOL_BLOCK = 128
FFN_TILE = 512
PROJ_TILE = 1024
MLSTM_CHUNK = 256
N_GATE_ROWS = 24

VMEM_LIMIT = 56 * 1024 * 1024

_NT = (((1,), (1,)), ((), ()))


def _params(*semantics):
    return pltpu.CompilerParams(dimension_semantics=semantics,
                                vmem_limit_bytes=VMEM_LIMIT)


def _resident(shape, layer=None):
    zeros = (0,) * len(shape)
    if layer is None:
        return pl.BlockSpec(shape, lambda *_: zeros, pipeline_mode=pl.Buffered(1))
    return pl.BlockSpec((None,) + shape, lambda *_: (layer,) + zeros,
                        pipeline_mode=pl.Buffered(1))


def _rms_norm(x, gain):
    return x * lax.rsqrt(jnp.mean(x * x, axis=-1, keepdims=True) + EPS) * gain


def _dot(a, b, dims=None):
    if dims is None:
        return jnp.dot(a, b, preferred_element_type=jnp.float32)
    return lax.dot_general(a, b, dims, preferred_element_type=jnp.float32)


def _pool_rows(x_ref, r0, r1, tile_in_seq, ext_ref, stage_refs, pooled_ref):
    gd = POOL_GROUP_DIM
    lo, hi = POOL_HALO + r0, POOL_HALO + r1
    x = x_ref[r0:r1, :]
    xn = x * lax.rsqrt(jnp.mean(x * x, axis=-1, keepdims=True) + EPS)
    ext_ref[lo:hi, :] = xn

    prev = ext_ref
    for j, cur in enumerate(stage_refs, start=1):
        shift = 2 ** (j - 1)
        start = 8 * j if r0 == 0 else lo
        skip = 0 if j == 1 else gd
        cur[start:hi, :] = prev[start:hi, skip:] + prev[start - shift:hi - shift, skip:]
        prev = cur

    ts = x_ref.shape[0]
    pos = (tile_in_seq * ts + r0 + 1
           + lax.broadcasted_iota(jnp.int32, (r1 - r0, 1), 0)).astype(jnp.float32)
    for g, win in enumerate(POOL_WINDOWS):
        cols = slice(g * gd, (g + 1) * gd)
        inv_cnt = 1.0 / jnp.minimum(pos, float(win))
        pooled = stage_refs[g][lo:hi, 0:gd] * inv_cnt - xn[:, cols]
        pooled_ref[r0:r1, cols] = pooled.astype(jnp.bfloat16)


def _pool_finish(x_ref, w_ref, ext_ref, pooled_ref, out_ref):
    ts = x_ref.shape[0]
    gd = POOL_GROUP_DIM
    for g in range(len(POOL_WINDOWS)):
        cols = slice(g * gd, (g + 1) * gd)
        out_ref[:, cols] = x_ref[:, cols] + _dot(pooled_ref[:, cols], w_ref[g])
    ext_ref[0:POOL_HALO, :] = ext_ref[ts:ts + POOL_HALO, :]


def _pool_jobs(x_ref, tile_in_seq, w_ref, ext_ref, stage_refs, pooled_ref, out_ref):
    jobs = [functools.partial(_pool_rows, x_ref, r0, r0 + POOL_BLOCK, tile_in_seq, ext_ref,
                              stage_refs, pooled_ref)
            for r0 in range(0, x_ref.shape[0], POOL_BLOCK)]
    jobs.append(functools.partial(_pool_finish, x_ref, w_ref, ext_ref, pooled_ref, out_ref))
    return jobs


def _ffn_kernel(*refs, layer, mixer_proj, final_norm, tiles_per_seq):
    if mixer_proj:
        x_ref, qt_ref, k_ref, vt_ref, r_ref, og_ref, w_mix_ref, *refs = refs
    else:
        x_first_ref, x_next_ref, pool_w_ref, *refs = refs
    (gain_ref, w_in_hbm, w_out_hbm, fgain_ref, o_ref,
     w_in_ref, w_out_ref, act_ref, stage_in, stage_out, sem, *mixer_scratch) = refs
    n_chunks = D_FF // FF_CHUNK
    step = pl.program_id(0)
    if mixer_proj:
        heads_ref, c_ref, n_ref = mixer_scratch
        tiles_per_proj = PROJ_TILE // FFN_TILE

        @pl.when(step % tiles_per_seq == 0)
        def _():
            c_ref[...] = jnp.zeros(c_ref.shape, jnp.float32)
            n_ref[...] = jnp.zeros(n_ref.shape, jnp.float32)

        @pl.when(step % tiles_per_proj == 0)
        def _():
            _mlstm_chunks(qt_ref, k_ref, vt_ref, r_ref, heads_ref, c_ref, n_ref, PROJ_TILE // MLSTM_CHUNK)
    else:
        mixed_ref, cur_ref, pooled_ref, ext_ref, *stage_refs = mixer_scratch
        next_in_seq = (step + 1) % tiles_per_seq

        @pl.when(next_in_seq == 0)
        def _():
            ext_ref[0:POOL_HALO, :] = jnp.zeros((POOL_HALO, D_MODEL), jnp.float32)

    def chunk_copies(j, slot):
        lo = j * FF_CHUNK
        return (
            pltpu.make_async_copy(w_in_hbm.at[layer, :, lo:lo + FF_CHUNK],
                                  stage_in.at[slot, 0], sem.at[slot, 0]),
            pltpu.make_async_copy(w_in_hbm.at[layer, :, D_FF + lo:D_FF + lo + FF_CHUNK],
                                  stage_in.at[slot, 1], sem.at[slot, 1]),
            pltpu.make_async_copy(w_out_hbm.at[layer, lo:lo + FF_CHUNK, :],
                                  stage_out.at[slot], sem.at[slot, 2]),
        )

    def body(stream_weights):
        if stream_weights:
            for cp in chunk_copies(0, 0):
                cp.start()
        side_jobs = []
        if mixer_proj:
            heads = heads_ref[step % tiles_per_proj]
            x = x_ref[...] + _dot(heads * og_ref[...], w_mix_ref[...], (((0,), (0,)), ((), ())))
        else:
            if stream_weights:
                ext_ref[0:POOL_HALO, :] = jnp.zeros((POOL_HALO, D_MODEL), jnp.float32)
                for job in _pool_jobs(x_first_ref, 0, pool_w_ref, ext_ref, stage_refs, pooled_ref,
                                      mixed_ref):
                    job()
            cur_ref[...] = mixed_ref[...]
            x = cur_ref[...]
            side_jobs = _pool_jobs(x_next_ref, next_in_seq, pool_w_ref, ext_ref, stage_refs,
                                   pooled_ref, mixed_ref)
        xb = x.astype(jnp.bfloat16)
        rstd = lax.rsqrt(jnp.mean(x * x, axis=-1, keepdims=True) + EPS)
        for j in range(n_chunks):
            lo = j * FF_CHUNK
            if stream_weights:
                slot = j % 2
                if j + 1 < n_chunks:
                    for cp in chunk_copies(j + 1, 1 - slot):
                        cp.start()
                for cp in chunk_copies(j, slot):
                    cp.wait()
                for part, col in ((0, lo), (1, D_FF + lo)):
                    for sub in range(0, FF_CHUNK, LANE):
                        w_in_ref[:, col + sub:col + sub + LANE] = (
                            stage_in[slot, part, :, sub:sub + LANE] * gain_ref[...]).astype(jnp.bfloat16)
                w_out_ref[lo:lo + FF_CHUNK, :] = stage_out[slot].astype(jnp.bfloat16)
            gate = _dot(xb, w_in_ref[:, lo:lo + FF_CHUNK]) * rstd
            up = _dot(xb, w_in_ref[:, D_FF + lo:D_FF + lo + FF_CHUNK]) * rstd
            act_ref[:, lo:lo + FF_CHUNK] = (gate * jax.nn.sigmoid(gate) * up).astype(jnp.bfloat16)
            if j < len(side_jobs):
                side_jobs[j]()
        out = x + _dot(act_ref[...], w_out_ref[...])
        if final_norm:
            out = _rms_norm(out, fgain_ref[...])
        o_ref[...] = out

    first = step == 0
    pl.when(first)(functools.partial(body, True))
    pl.when(jnp.logical_not(first))(functools.partial(body, False))


def _ffn_layer(h2, S, gain, w_in, w_out, final_gain, layer, final_norm, mlstm=None, out_gate=None,
               w_mix=None, pool_w=None, mix_layer=None):
    T, D = h2.shape
    tm = FFN_TILE
    nt = T // tm
    assert S // tm > 1 and tm % POOL_BLOCK == 0 and tm // POOL_BLOCK < D_FF // FF_CHUNK
    tok = pl.BlockSpec((tm, D), lambda i: (i, 0))
    hbm = pl.BlockSpec(memory_space=pl.ANY)
    mixer_proj = mlstm is not None
    if mixer_proj:
        tpp = PROJ_TILE // tm
        ppq = S // PROJ_TILE
        assert PROJ_TILE % tm == 0 and S % PROJ_TILE == 0
        proj_tile = lambda i: (i // tpp, 0, 0)
        mix_specs = [
            tok,
            pl.BlockSpec((1, QK_W, PROJ_TILE), proj_tile),
            pl.BlockSpec((PROJ_TILE, QK_W), lambda i: (i // tpp, 0)),
            pl.BlockSpec((1, V_W, PROJ_TILE), proj_tile),
            pl.BlockSpec((1, N_GATE_ROWS, PROJ_TILE), lambda i: (i // tpp // ppq, 0, i // tpp % ppq)),
            pl.BlockSpec((None, D, tm), lambda i: (i, 0, 0)),
            _resident(w_mix.shape[1:], mix_layer),
        ]
        mix_args = (h2, *mlstm, out_gate, w_mix)
        mixer_scratch = [
            pltpu.VMEM((tpp, V_W, tm), jnp.bfloat16),
            pltpu.VMEM((N_HEADS, DV, DQK), jnp.float32),
            pltpu.VMEM((N_HEADS, 1, DQK), jnp.float32),
        ]
    else:
        mix_specs = [pl.BlockSpec((tm, D), lambda i: (0, 0), pipeline_mode=pl.Buffered(1)),
                     pl.BlockSpec((tm, D), lambda i: (jnp.minimum(i + 1, nt - 1), 0)),
                     _resident(pool_w.shape[1:], mix_layer)]
        mix_args = (h2, h2, pool_w)
        mixer_scratch = [
            pltpu.VMEM((tm, D), jnp.float32),
            pltpu.VMEM((tm, D), jnp.float32),
            pltpu.VMEM((tm, D), jnp.bfloat16),
            pltpu.VMEM((POOL_HALO + tm, D), jnp.float32),
        ] + [pltpu.VMEM((POOL_HALO + tm, D - g * POOL_GROUP_DIM), jnp.float32)
             for g in range(len(POOL_WINDOWS))]
    return pl.pallas_call(
        functools.partial(_ffn_kernel, layer=layer, mixer_proj=mixer_proj, final_norm=final_norm,
                          tiles_per_seq=S // tm),
        grid=(nt,),
        in_specs=mix_specs + [_resident((D, LANE), layer), hbm, hbm, _resident((1, D))],
        out_specs=tok,
        out_shape=jax.ShapeDtypeStruct(h2.shape, h2.dtype),
        scratch_shapes=[
            pltpu.VMEM((D, 2 * D_FF), jnp.bfloat16),
            pltpu.VMEM((D_FF, D), jnp.bfloat16),
            pltpu.VMEM((tm, D_FF), jnp.bfloat16),
            pltpu.VMEM((2, 2, D, FF_CHUNK), jnp.float32),
            pltpu.VMEM((2, FF_CHUNK, D), jnp.float32),
            pltpu.SemaphoreType.DMA((2, 3)),
        ] + mixer_scratch,
        compiler_params=_params("arbitrary"),
        name="swiglu" + ("_mix" if mixer_proj else "_pool") + ("_final" if final_norm else ""),
    )(*mix_args, gain, w_in, w_out, final_gain)


def _proj_kernel(x_ref, wt_ref, hn_ref, qt_ref, k_ref, vt_ref, ogt_ref, gt_ref):
    x = x_ref[...]
    xb = x.astype(jnp.bfloat16)
    rstd = lax.rsqrt(jnp.mean(x * x, axis=-1, keepdims=True) + EPS)
    rstd_row = jnp.broadcast_to(rstd, (x.shape[0], LANE)).T[0:1, :]
    og = _dot(wt_ref[2 * QK_W + V_W:, :], xb, _NT) * rstd_row
    gt_ref[...] = og[V_W:V_W + N_GATES, :]
    for lo in range(0, og.shape[1], LANE):
        ogt_ref[lo // FFN_TILE, :, lo % FFN_TILE:lo % FFN_TILE + LANE] = (
            jax.nn.sigmoid(og[0:V_W, lo:lo + LANE]) * hn_ref[...]).astype(ogt_ref.dtype)
    vt_ref[...] = (_dot(wt_ref[2 * QK_W:2 * QK_W + V_W, :], xb, _NT) * rstd_row).astype(vt_ref.dtype)
    qt_ref[...] = (_dot(wt_ref[0:QK_W, :], xb, _NT) * (rstd_row * (DQK ** -0.5))).astype(qt_ref.dtype)
    k_ref[...] = (_dot(xb, wt_ref[QK_W:2 * QK_W, :], _NT) * rstd).astype(k_ref.dtype)


def _proj_layer(h2, w_t, head_norm_b, layer):
    T, D = h2.shape
    tm = PROJ_TILE
    tok = lambda i: (i, 0)
    lanes = lambda i: (0, i)
    tile = lambda i: (i, 0, 0)
    return pl.pallas_call(
        _proj_kernel,
        grid=(T // tm,),
        in_specs=[
            pl.BlockSpec((tm, D), tok),
            _resident(w_t.shape[1:], layer),
            _resident(head_norm_b.shape[1:], layer),
        ],
        out_specs=[
            pl.BlockSpec((None, QK_W, tm), tile),
            pl.BlockSpec((tm, QK_W), tok),
            pl.BlockSpec((None, V_W, tm), tile),
            pl.BlockSpec((tm // FFN_TILE, V_W, FFN_TILE), tile),
            pl.BlockSpec((N_GATES, tm), lanes),
        ],
        out_shape=[
            jax.ShapeDtypeStruct((T // tm, QK_W, tm), jnp.bfloat16),
            jax.ShapeDtypeStruct((T, QK_W), jnp.bfloat16),
            jax.ShapeDtypeStruct((T // tm, V_W, tm), jnp.bfloat16),
            jax.ShapeDtypeStruct((T // FFN_TILE, V_W, FFN_TILE), jnp.bfloat16),
            jax.ShapeDtypeStruct((N_GATES, T), jnp.float32),
        ],
        compiler_params=_params("arbitrary"),
        name="mlstm_proj",
    )(h2, w_t, head_norm_b)


def _segmented_scan(x, op, lane, seg):
    shift = 1
    while shift < seg:
        moved = pltpu.roll(x, shift, 1)
        x = jnp.where((lane % seg) >= shift, op(x, moved), x)
        shift *= 2
    return x


def _gate_kernel(gt_ref, bias_ref, r_ref):
    L = MLSTM_CHUNK
    B, _, S = r_ref.shape
    g = gt_ref[...] + bias_ref[...]
    per_seq = lambda lo: jnp.concatenate(
        [g[lo:lo + N_HEADS, s * S:(s + 1) * S] for s in range(B)], axis=0)
    i_pre = per_seq(0)
    f_pre = per_seq(N_HEADS)
    logf = jnp.minimum(f_pre, 0.0) - jnp.log1p(jnp.exp(-jnp.abs(f_pre)))
    lane = lax.broadcasted_iota(jnp.int32, (B * N_HEADS, S), 1)
    b = _segmented_scan(logf, jnp.add, lane, L)
    a = i_pre - b
    amax = _segmented_scan(a, jnp.maximum, lane, L)

    def put(kind, lanes, value):
        for s in range(B):
            r_ref[s, kind * N_HEADS:(kind + 1) * N_HEADS, lanes] = value[s * N_HEADS:(s + 1) * N_HEADS]

    put(0, slice(0, S), a)
    r_ref[:, 5 * N_HEADS:N_GATE_ROWS, :] = jnp.zeros((B, N_GATE_ROWS - 5 * N_HEADS, S), jnp.float32)
    m_prev = jnp.zeros((B * N_HEADS, 1), jnp.float32)
    for c in range(S // L):
        seg = slice(c * L, (c + 1) * L)
        big_m = jnp.maximum(m_prev, amax[:, seg])
        m_last = big_m[:, L - 1:L]
        put(1, seg, big_m)
        put(2, seg, -(b[:, seg] + big_m))
        put(3, seg, m_prev - big_m)
        put(4, seg, a[:, seg] - m_last)
        m_prev = b[:, (c + 1) * L - 1:(c + 1) * L] + m_last


def _gate_layer(gates_t, bias, B, S, layer):
    return pl.pallas_call(
        _gate_kernel,
        grid=(1,),
        in_specs=[
            pl.BlockSpec((N_GATES, B * S), lambda i: (0, 0)),
            _resident((N_GATES, 1), layer),
        ],
        out_specs=pl.BlockSpec((B, N_GATE_ROWS, S), lambda i: (0, 0, 0)),
        out_shape=jax.ShapeDtypeStruct((B, N_GATE_ROWS, S), jnp.float32),
        compiler_params=_params("arbitrary"),
        name="mlstm_gates",
    )(gates_t, bias)


def _hi_lo_rows(row):
    hi = row.astype(jnp.bfloat16).astype(jnp.float32)
    idx = lax.broadcasted_iota(jnp.int32, (16, row.shape[1]), 0)
    slab = jnp.where(idx == 0, hi, jnp.where(idx == 1, row - hi, 0.0))
    return slab.astype(jnp.bfloat16)


def _mlstm_chunks(qt_ref, k_ref, vt_ref, r_ref, out_ref, c_ref, n_ref, n_chunks):
    L = MLSTM_CHUNK
    H = L // 2

    causal = (lax.broadcasted_iota(jnp.int32, (H, H), 0)
              <= lax.broadcasted_iota(jnp.int32, (H, H), 1))

    tok = lambda c: slice(c * L, (c + 1) * L)
    lanes = lambda ref, rows, c: ref[(c * L) // PROJ_TILE, rows,
                                     (c * L) % PROJ_TILE:(c * L) % PROJ_TILE + L]
    qk = lambda h: slice(h * DQK, (h + 1) * DQK)
    vv = lambda h: slice(h * DV, (h + 1) * DV)
    units = [(c, h) for c in range(n_chunks) for h in range(N_HEADS)]
    rows = [r_ref[0, :, tok(c)] for c in range(n_chunks)]
    a_cols = [r[0:2 * N_HEADS, :].T for r in rows]
    gate_row = lambda c, h, kind: rows[c][kind * N_HEADS + h:kind * N_HEADS + h + 1, :]
    ct = [c_ref[h] for h in range(N_HEADS)]
    n = [n_ref[h] for h in range(N_HEADS)]

    def scores(c, h):
        lhs = jnp.concatenate([k_ref[tok(c), qk(h)], ct[h].astype(jnp.bfloat16), _hi_lo_rows(n[h])], axis=0)
        return _dot(lhs, lanes(qt_ref, qk(h), c))

    def weigh(c, h, both):
        s, from_state = both[0:L, :], both[L:, :]
        a_col = a_cols[c][:, h:h + 1]
        big_m = gate_row(c, h, 1)
        diag = lambda i: jnp.exp(jnp.where(
            causal, a_col[i * H:(i + 1) * H, :] - big_m[:, i * H:(i + 1) * H], -jnp.inf))
        above = jnp.exp(a_col[0:H, :] - big_m[:, H:L])
        top = s[0:H, :] * jnp.concatenate([diag(0), above], axis=1)
        bottom = jnp.concatenate([jnp.zeros((H, H), jnp.float32), s[H:L, H:L] * diag(1)], axis=1)
        st = jnp.concatenate([top, bottom], axis=0)
        den_intra = jnp.sum(st, axis=0, keepdims=True)
        num_intra = _dot(lanes(vt_ref, vv(h), c), st.astype(jnp.bfloat16))
        return den_intra, num_intra, from_state

    def finish(c, h, den_intra, num_intra, from_state):
        inter = jnp.exp(gate_row(c, h, 3))
        w = jnp.exp(gate_row(c, h, 4))
        qn = from_state[DV:DV + 1, :] + from_state[DV + 1:DV + 2, :]
        num = inter * from_state[0:DV, :] + num_intra
        den = inter * qn + den_intra
        r = 1.0 / jnp.maximum(jnp.abs(den), jnp.exp(gate_row(c, h, 2)))
        ss = jnp.sum(num * num, axis=0, keepdims=True)
        scale = r * lax.rsqrt(r * r * ss * (1.0 / DV) + EPS)
        lane0 = (c * L) % FFN_TILE
        out_ref[(c * L) // FFN_TILE, vv(h), lane0:lane0 + L] = (num * scale).astype(out_ref.dtype)

        vw = jnp.concatenate([lanes(vt_ref, vv(h), c) * w.astype(jnp.bfloat16), _hi_lo_rows(w)], axis=0)
        upd = _dot(vw, k_ref[tok(c), qk(h)])
        decay = inter[:, L - 1:L]
        ct[h] = decay * ct[h] + upd[0:DV, :]
        n[h] = decay * n[h] + upd[DV:DV + 1, :] + upd[DV + 1:DV + 2, :]

    scored = {0: scores(*units[0])}
    if len(units) > 1:
        scored[1] = scores(*units[1])
    weighed = {0: weigh(*units[0], scored.pop(0))}
    for i, u in enumerate(units):
        if i + 2 < len(units):
            scored[i + 2] = scores(*units[i + 2])
        if i + 1 < len(units):
            weighed[i + 1] = weigh(*units[i + 1], scored.pop(i + 1))
        finish(*u, *weighed.pop(i))

    for h in range(N_HEADS):
        c_ref[h] = ct[h]
        n_ref[h] = n[h]


def kernel(x, pool_norm, pool_w, pool_scale, mlstm_norm, mlstm_w_in, mlstm_gate_bias,
           mlstm_head_norm, mlstm_w_out, ffn_norm, ffn_w_in, ffn_w_out, final_norm):
    B, S, D = x.shape
    depth = ffn_norm.shape[0]
    bf16 = jnp.bfloat16
    groups = pool_w.shape[:2] + (POOL_GROUP_DIM,)
    pool_w_b = (pool_norm.reshape(groups)[..., :, None] * pool_w
                * pool_scale.reshape(groups)[..., None, :]).astype(bf16)
    w_t_b = jnp.swapaxes(
        jnp.pad(mlstm_w_in * mlstm_norm[:, :, None], ((0, 0), (0, 0), (0, W_T_ROWS - mlstm_w_in.shape[2]))),
        1, 2).astype(bf16)
    head_norm_b = jnp.broadcast_to(mlstm_head_norm[:, :, None], mlstm_head_norm.shape + (LANE,))
    gate_bias = mlstm_gate_bias.reshape(-1, N_GATES, 1)
    w_mix_b = mlstm_w_out.astype(bf16)
    fgain = final_norm.reshape(1, D)
    ffn_gain_b = jnp.broadcast_to(ffn_norm[:, :, None], ffn_norm.shape + (LANE,))

    h2 = x.reshape(B * S, D)
    for i in range(depth):
        j = i // 2
        last = i == depth - 1
        if i % 2 == 0:
            h2 = _ffn_layer(h2, S, ffn_gain_b, ffn_w_in, ffn_w_out, fgain, i, last,
                            pool_w=pool_w_b, mix_layer=j)
        else:
            qt, k, vt, ogt, gates_t = _proj_layer(h2, w_t_b, head_norm_b, j)
            rows = _gate_layer(gates_t, gate_bias, B, S, j)
            h2 = _ffn_layer(h2, S, ffn_gain_b, ffn_w_in, ffn_w_out, fgain, i, last,
                            mlstm=(qt, k, vt, rows), out_gate=ogt, w_mix=w_mix_b, mix_layer=j)
    return h2.reshape(B, S, D)
```

```python
import functools
import math

import jax
import jax.numpy as jnp
from jax import lax
from jax.experimental import pallas as pl
from jax.experimental.pallas import tpu as pltpu

D_MODEL = 1024
POOL_WINDOWS = (2, 4, 8, 16)
POOL_GROUP_DIM = D_MODEL // len(POOL_WINDOWS)
POOL_HALO = 32

N_HEADS = 4
DV = D_MODEL // N_HEADS
DQK = DV // 2
QK_W = N_HEADS * DQK
V_W = N_HEADS * DV
MAIN_W = 2 * QK_W + 2 * V_W
N_GATES = 2 * N_HEADS
W_T_ROWS = MAIN_W + 16

D_FF = int(math.ceil(8 * D_MODEL / 3 / 256) * 256)
FF_CHUNK = 256
LANE = 128

EPS = 1e-6

POOL_BLOCK = 128
FFN_TILE = 512
PROJ_TILE = 1024
MLSTM_CHUNK = 256
N_GATE_ROWS = 24

VMEM_LIMIT = 56 * 1024 * 1024

_NT = (((1,), (1,)), ((), ()))


def _params(*semantics):
    return pltpu.CompilerParams(dimension_semantics=semantics,
                                vmem_limit_bytes=VMEM_LIMIT)


def _resident(shape, layer=None):
    zeros = (0,) * len(shape)
    if layer is None:
        return pl.BlockSpec(shape, lambda *_: zeros, pipeline_mode=pl.Buffered(1))
    return pl.BlockSpec((None,) + shape, lambda *_: (layer,) + zeros,
                        pipeline_mode=pl.Buffered(1))


def _rms_norm(x, gain):
    return x * lax.rsqrt(jnp.mean(x * x, axis=-1, keepdims=True) + EPS) * gain


def _dot(a, b, dims=None):
    if dims is None:
        return jnp.dot(a, b, preferred_element_type=jnp.float32)
    return lax.dot_general(a, b, dims, preferred_element_type=jnp.float32)


def _pool_rows(x_ref, r0, r1, tile_in_seq, ext_ref, stage_refs, pooled_ref):
    gd = POOL_GROUP_DIM
    lo, hi = POOL_HALO + r0, POOL_HALO + r1
    x = x_ref[r0:r1, :]
    xn = x * lax.rsqrt(jnp.mean(x * x, axis=-1, keepdims=True) + EPS)
    ext_ref[lo:hi, :] = xn

    prev = ext_ref
    for j, cur in enumerate(stage_refs, start=1):
        shift = 2 ** (j - 1)
        start = 8 * j if r0 == 0 else lo
        skip = 0 if j == 1 else gd
        cur[start:hi, :] = prev[start:hi, skip:] + prev[start - shift:hi - shift, skip:]
        prev = cur

    ts = x_ref.shape[0]
    pos = (tile_in_seq * ts + r0 + 1
           + lax.broadcasted_iota(jnp.int32, (r1 - r0, 1), 0)).astype(jnp.float32)
    for g, win in enumerate(POOL_WINDOWS):
        cols = slice(g * gd, (g + 1) * gd)
        inv_cnt = 1.0 / jnp.minimum(pos, float(win))
        pooled = stage_refs[g][lo:hi, 0:gd] * inv_cnt - xn[:, cols]
        pooled_ref[r0:r1, cols] = pooled.astype(jnp.bfloat16)


def _pool_finish(x_ref, w_ref, ext_ref, pooled_ref, out_ref):
    ts = x_ref.shape[0]
    gd = POOL_GROUP_DIM
    for g in range(len(POOL_WINDOWS)):
        cols = slice(g * gd, (g + 1) * gd)
        out_ref[:, cols] = x_ref[:, cols] + _dot(pooled_ref[:, cols], w_ref[g])
    ext_ref[0:POOL_HALO, :] = ext_ref[ts:ts + POOL_HALO, :]


def _pool_jobs(x_ref, tile_in_seq, w_ref, ext_ref, stage_refs, pooled_ref, out_ref):
    jobs = [functools.partial(_pool_rows, x_ref, r0, r0 + POOL_BLOCK, tile_in_seq, ext_ref,
                              stage_refs, pooled_ref)
            for r0 in range(0, x_ref.shape[0], POOL_BLOCK)]
    jobs.append(functools.partial(_pool_finish, x_ref, w_ref, ext_ref, pooled_ref, out_ref))
    return jobs


def _ffn_kernel(*refs, layer, mixer_proj, final_norm, tiles_per_seq):
    if mixer_proj:
        x_ref, qt_ref, k_ref, vt_ref, r_ref, og_ref, w_mix_ref, *refs = refs
    else:
        x_first_ref, x_next_ref, pool_w_ref, *refs = refs
    (gain_ref, w_in_hbm, w_out_hbm, fgain_ref, o_ref,
     w_in_ref, w_out_ref, act_ref, stage_in, stage_out, sem, *mixer_scratch) = refs
    n_chunks = D_FF // FF_CHUNK
    step = pl.program_id(0)
    if mixer_proj:
        heads_ref, c_ref, n_ref = mixer_scratch
        tiles_per_proj = PROJ_TILE // FFN_TILE

        @pl.when(step % tiles_per_seq == 0)
        def _():
            c_ref[...] = jnp.zeros(c_ref.shape, jnp.float32)
            n_ref[...] = jnp.zeros(n_ref.shape, jnp.float32)

        @pl.when(step % tiles_per_proj == 0)
        def _():
            _mlstm_chunks(qt_ref, k_ref, vt_ref, r_ref, heads_ref, c_ref, n_ref, PROJ_TILE // MLSTM_CHUNK)
    else:
        mixed_ref, cur_ref, pooled_ref, ext_ref, *stage_refs = mixer_scratch
        next_in_seq = (step + 1) % tiles_per_seq

        @pl.when(next_in_seq == 0)
        def _():
            ext_ref[0:POOL_HALO, :] = jnp.zeros((POOL_HALO, D_MODEL), jnp.float32)

    def chunk_copies(j, slot):
        lo = j * FF_CHUNK
        return (
            pltpu.make_async_copy(w_in_hbm.at[layer, :, lo:lo + FF_CHUNK],
                                  stage_in.at[slot, 0], sem.at[slot, 0]),
            pltpu.make_async_copy(w_in_hbm.at[layer, :, D_FF + lo:D_FF + lo + FF_CHUNK],
                                  stage_in.at[slot, 1], sem.at[slot, 1]),
            pltpu.make_async_copy(w_out_hbm.at[layer, lo:lo + FF_CHUNK, :],
                                  stage_out.at[slot], sem.at[slot, 2]),
        )

    def body(stream_weights):
        if stream_weights:
            for cp in chunk_copies(0, 0):
                cp.start()
        side_jobs = []
        if mixer_proj:
            heads = heads_ref[step % tiles_per_proj]
            x = x_ref[...] + _dot(heads * og_ref[...], w_mix_ref[...], (((0,), (0,)), ((), ())))
        else:
            if stream_weights:
                ext_ref[0:POOL_HALO, :] = jnp.zeros((POOL_HALO, D_MODEL), jnp.float32)
                for job in _pool_jobs(x_first_ref, 0, pool_w_ref, ext_ref, stage_refs, pooled_ref,
                                      mixed_ref):
                    job()
            cur_ref[...] = mixed_ref[...]
            x = cur_ref[...]
            side_jobs = _pool_jobs(x_next_ref, next_in_seq, pool_w_ref, ext_ref, stage_refs,
                                   pooled_ref, mixed_ref)
        xb = x.astype(jnp.bfloat16)
        rstd = lax.rsqrt(jnp.mean(x * x, axis=-1, keepdims=True) + EPS)
        for j in range(n_chunks):
            lo = j * FF_CHUNK
            if stream_weights:
                slot = j % 2
                if j + 1 < n_chunks:
                    for cp in chunk_copies(j + 1, 1 - slot):
                        cp.start()
                for cp in chunk_copies(j, slot):
                    cp.wait()
                for part, col in ((0, lo), (1, D_FF + lo)):
                    for sub in range(0, FF_CHUNK, LANE):
                        w_in_ref[:, col + sub:col + sub + LANE] = (
                            stage_in[slot, part, :, sub:sub + LANE] * gain_ref[...]).astype(jnp.bfloat16)
                w_out_ref[lo:lo + FF_CHUNK, :] = stage_out[slot].astype(jnp.bfloat16)
            gate = _dot(xb, w_in_ref[:, lo:lo + FF_CHUNK]) * rstd
            up = _dot(xb, w_in_ref[:, D_FF + lo:D_FF + lo + FF_CHUNK]) * rstd
            act_ref[:, lo:lo + FF_CHUNK] = (gate * jax.nn.sigmoid(gate) * up).astype(jnp.bfloat16)
            if j < len(side_jobs):
                side_jobs[j]()
        out = x + _dot(act_ref[...], w_out_ref[...])
        if final_norm:
            out = _rms_norm(out, fgain_ref[...])
        o_ref[...] = out

    first = step == 0
    pl.when(first)(functools.partial(body, True))
    pl.when(jnp.logical_not(first))(functools.partial(body, False))


def _ffn_layer(h2, S, gain, w_in, w_out, final_gain, layer, final_norm, mlstm=None, out_gate=None,
               w_mix=None, pool_w=None, mix_layer=None):
    T, D = h2.shape
    tm = FFN_TILE
    nt = T // tm
    assert S // tm > 1 and tm % POOL_BLOCK == 0 and tm // POOL_BLOCK < D_FF // FF_CHUNK
    tok = pl.BlockSpec((tm, D), lambda i: (i, 0))
    hbm = pl.BlockSpec(memory_space=pl.ANY)
    mixer_proj = mlstm is not None
    if mixer_proj:
        tpp = PROJ_TILE // tm
        ppq = S // PROJ_TILE
        assert PROJ_TILE % tm == 0 and S % PROJ_TILE == 0
        proj_tile = lambda i: (i // tpp, 0, 0)
        mix_specs = [
            tok,
            pl.BlockSpec((1, QK_W, PROJ_TILE), proj_tile),
            pl.BlockSpec((PROJ_TILE, QK_W), lambda i: (i // tpp, 0)),
            pl.BlockSpec((1, V_W, PROJ_TILE), proj_tile),
            pl.BlockSpec((1, N_GATE_ROWS, PROJ_TILE), lambda i: (i // tpp // ppq, 0, i // tpp % ppq)),
            pl.BlockSpec((None, D, tm), lambda i: (i, 0, 0)),
            _resident(w_mix.shape[1:], mix_layer),
        ]
        mix_args = (h2, *mlstm, out_gate, w_mix)
        mixer_scratch = [
            pltpu.VMEM((tpp, V_W, tm), jnp.bfloat16),
            pltpu.VMEM((N_HEADS, DV, DQK), jnp.float32),
            pltpu.VMEM((N_HEADS, 1, DQK), jnp.float32),
        ]
    else:
        mix_specs = [pl.BlockSpec((tm, D), lambda i: (0, 0), pipeline_mode=pl.Buffered(1)),
                     pl.BlockSpec((tm, D), lambda i: (jnp.minimum(i + 1, nt - 1), 0)),
                     _resident(pool_w.shape[1:], mix_layer)]
        mix_args = (h2[:tm], h2, pool_w)
        mixer_scratch = [
            pltpu.VMEM((tm, D), jnp.float32),
            pltpu.VMEM((tm, D), jnp.float32),
            pltpu.VMEM((tm, D), jnp.bfloat16),
            pltpu.VMEM((POOL_HALO + tm, D), jnp.float32),
        ] + [pltpu.VMEM((POOL_HALO + tm, D - g * POOL_GROUP_DIM), jnp.float32)
             for g in range(len(POOL_WINDOWS))]
    return pl.pallas_call(
        functools.partial(_ffn_kernel, layer=layer, mixer_proj=mixer_proj, final_norm=final_norm,
                          tiles_per_seq=S // tm),
        grid=(nt,),
        in_specs=mix_specs + [_resident((D, LANE), layer), hbm, hbm, _resident((1, D))],
        out_specs=tok,
        out_shape=jax.ShapeDtypeStruct(h2.shape, h2.dtype),
        scratch_shapes=[
            pltpu.VMEM((D, 2 * D_FF), jnp.bfloat16),
            pltpu.VMEM((D_FF, D), jnp.bfloat16),
            pltpu.VMEM((tm, D_FF), jnp.bfloat16),
            pltpu.VMEM((2, 2, D, FF_CHUNK), jnp.float32),
            pltpu.VMEM((2, FF_CHUNK, D), jnp.float32),
            pltpu.SemaphoreType.DMA((2, 3)),
        ] + mixer_scratch,
        compiler_params=_params("arbitrary"),
        name="swiglu" + ("_mix" if mixer_proj else "_pool") + ("_final" if final_norm else ""),
    )(*mix_args, gain, w_in, w_out, final_gain)


def _proj_kernel(x_ref, wt_ref, hn_ref, qt_ref, k_ref, vt_ref, ogt_ref, gt_ref):
    x = x_ref[...]
    xb = x.astype(jnp.bfloat16)
    rstd = lax.rsqrt(jnp.mean(x * x, axis=-1, keepdims=True) + EPS)
    rstd_row = jnp.broadcast_to(rstd, (x.shape[0], LANE)).T[0:1, :]
    og = _dot(wt_ref[2 * QK_W + V_W:, :], xb, _NT) * rstd_row
    gt_ref[...] = og[V_W:V_W + N_GATES, :]
    for lo in range(0, og.shape[1], LANE):
        ogt_ref[lo // FFN_TILE, :, lo % FFN_TILE:lo % FFN_TILE + LANE] = (
            jax.nn.sigmoid(og[0:V_W, lo:lo + LANE]) * hn_ref[...]).astype(ogt_ref.dtype)
    vt_ref[...] = (_dot(wt_ref[2 * QK_W:2 * QK_W + V_W, :], xb, _NT) * rstd_row).astype(vt_ref.dtype)
    qt_ref[...] = (_dot(wt_ref[0:QK_W, :], xb, _NT) * (rstd_row * (DQK ** -0.5))).astype(qt_ref.dtype)
    k_ref[...] = (_dot(xb, wt_ref[QK_W:2 * QK_W, :], _NT) * rstd).astype(k_ref.dtype)


def _proj_layer(h2, w_t, head_norm_b, layer):
    T, D = h2.shape
    tm = PROJ_TILE
    tok = lambda i: (i, 0)
    lanes = lambda i: (0, i)
    tile = lambda i: (i, 0, 0)
    return pl.pallas_call(
        _proj_kernel,
        grid=(T // tm,),
        in_specs=[
            pl.BlockSpec((tm, D), tok),
            _resident(w_t.shape[1:], layer),
            _resident(head_norm_b.shape[1:], layer),
        ],
        out_specs=[
            pl.BlockSpec((None, QK_W, tm), tile),
            pl.BlockSpec((tm, QK_W), tok),
            pl.BlockSpec((None, V_W, tm), tile),
            pl.BlockSpec((tm // FFN_TILE, V_W, FFN_TILE), tile),
            pl.BlockSpec((N_GATES, tm), lanes),
        ],
        out_shape=[
            jax.ShapeDtypeStruct((T // tm, QK_W, tm), jnp.bfloat16),
            jax.ShapeDtypeStruct((T, QK_W), jnp.bfloat16),
            jax.ShapeDtypeStruct((T // tm, V_W, tm), jnp.bfloat16),
            jax.ShapeDtypeStruct((T // FFN_TILE, V_W, FFN_TILE), jnp.bfloat16),
            jax.ShapeDtypeStruct((N_GATES, T), jnp.float32),
        ],
        compiler_params=_params("arbitrary"),
        name="mlstm_proj",
    )(h2, w_t, head_norm_b)


def _segmented_scan(x, op, lane, seg):
    shift = 1
    while shift < seg:
        moved = pltpu.roll(x, shift, 1)
        x = jnp.where((lane % seg) >= shift, op(x, moved), x)
        shift *= 2
    return x


def _gate_kernel(gt_ref, bias_ref, r_ref):
    L = MLSTM_CHUNK
    B, _, S = r_ref.shape
    g = gt_ref[...] + bias_ref[...]
    per_seq = lambda lo: jnp.concatenate(
        [g[lo:lo + N_HEADS, s * S:(s + 1) * S] for s in range(B)], axis=0)
    i_pre = per_seq(0)
    f_pre = per_seq(N_HEADS)
    logf = jnp.minimum(f_pre, 0.0) - jnp.log1p(jnp.exp(-jnp.abs(f_pre)))
    lane = lax.broadcasted_iota(jnp.int32, (B * N_HEADS, S), 1)
    b = _segmented_scan(logf, jnp.add, lane, L)
    a = i_pre - b
    amax = _segmented_scan(a, jnp.maximum, lane, L)

    def put(kind, lanes, value):
        for s in range(B):
            r_ref[s, kind * N_HEADS:(kind + 1) * N_HEADS, lanes] = value[s * N_HEADS:(s + 1) * N_HEADS]

    put(0, slice(0, S), a)
    r_ref[:, 5 * N_HEADS:N_GATE_ROWS, :] = jnp.zeros((B, N_GATE_ROWS - 5 * N_HEADS, S), jnp.float32)
    m_prev = jnp.zeros((B * N_HEADS, 1), jnp.float32)
    for c in range(S // L):
        seg = slice(c * L, (c + 1) * L)
        big_m = jnp.maximum(m_prev, amax[:, seg])
        m_last = big_m[:, L - 1:L]
        put(1, seg, big_m)
        put(2, seg, -(b[:, seg] + big_m))
        put(3, seg, m_prev - big_m)
        put(4, seg, a[:, seg] - m_last)
        m_prev = b[:, (c + 1) * L - 1:(c + 1) * L] + m_last


def _gate_layer(gates_t, bias, B, S, layer):
    return pl.pallas_call(
        _gate_kernel,
        grid=(1,),
        in_specs=[
            pl.BlockSpec((N_GATES, B * S), lambda i: (0, 0)),
            _resident((N_GATES, 1), layer),
        ],
        out_specs=pl.BlockSpec((B, N_GATE_ROWS, S), lambda i: (0, 0, 0)),
        out_shape=jax.ShapeDtypeStruct((B, N_GATE_ROWS, S), jnp.float32),
        compiler_params=_params("arbitrary"),
        name="mlstm_gates",
    )(gates_t, bias)


def _hi_lo_rows(row):
    hi = row.astype(jnp.bfloat16).astype(jnp.float32)
    idx = lax.broadcasted_iota(jnp.int32, (16, row.shape[1]), 0)
    slab = jnp.where(idx == 0, hi, jnp.where(idx == 1, row - hi, 0.0))
    return slab.astype(jnp.bfloat16)


def _mlstm_chunks(qt_ref, k_ref, vt_ref, r_ref, out_ref, c_ref, n_ref, n_chunks):
    L = MLSTM_CHUNK
    H = L // 2

    causal = (lax.broadcasted_iota(jnp.int32, (H, H), 0)
              <= lax.broadcasted_iota(jnp.int32, (H, H), 1))

    tok = lambda c: slice(c * L, (c + 1) * L)
    lanes = lambda ref, rows, c: ref[(c * L) // PROJ_TILE, rows,
                                     (c * L) % PROJ_TILE:(c * L) % PROJ_TILE + L]
    qk = lambda h: slice(h * DQK, (h + 1) * DQK)
    vv = lambda h: slice(h * DV, (h + 1) * DV)
    units = [(c, h) for c in range(n_chunks) for h in range(N_HEADS)]
    rows = [r_ref[0, :, tok(c)] for c in range(n_chunks)]
    a_cols = [r[0:2 * N_HEADS, :].T for r in rows]
    gate_row = lambda c, h, kind: rows[c][kind * N_HEADS + h:kind * N_HEADS + h + 1, :]
    ct = [c_ref[h] for h in range(N_HEADS)]
    n = [n_ref[h] for h in range(N_HEADS)]

    def scores(c, h):
        lhs = jnp.concatenate([k_ref[tok(c), qk(h)], ct[h].astype(jnp.bfloat16), _hi_lo_rows(n[h])], axis=0)
        return _dot(lhs, lanes(qt_ref, qk(h), c))

    def weigh(c, h, both):
        s, from_state = both[0:L, :], both[L:, :]
        a_col = a_cols[c][:, h:h + 1]
        big_m = gate_row(c, h, 1)
        diag = lambda i: jnp.exp(jnp.where(
            causal, a_col[i * H:(i + 1) * H, :] - big_m[:, i * H:(i + 1) * H], -jnp.inf))
        above = jnp.exp(a_col[0:H, :] - big_m[:, H:L])
        top = s[0:H, :] * jnp.concatenate([diag(0), above], axis=1)
        bottom = jnp.concatenate([jnp.zeros((H, H), jnp.float32), s[H:L, H:L] * diag(1)], axis=1)
        st = jnp.concatenate([top, bottom], axis=0)
        den_intra = jnp.sum(st, axis=0, keepdims=True)
        num_intra = _dot(lanes(vt_ref, vv(h), c), st.astype(jnp.bfloat16))
        return den_intra, num_intra, from_state

    def finish(c, h, den_intra, num_intra, from_state):
        inter = jnp.exp(gate_row(c, h, 3))
        w = jnp.exp(gate_row(c, h, 4))
        qn = from_state[DV:DV + 1, :] + from_state[DV + 1:DV + 2, :]
        num = inter * from_state[0:DV, :] + num_intra
        den = inter * qn + den_intra
        r = 1.0 / jnp.maximum(jnp.abs(den), jnp.exp(gate_row(c, h, 2)))
        ss = jnp.sum(num * num, axis=0, keepdims=True)
        scale = r * lax.rsqrt(r * r * ss * (1.0 / DV) + EPS)
        lane0 = (c * L) % FFN_TILE
        out_ref[(c * L) // FFN_TILE, vv(h), lane0:lane0 + L] = (num * scale).astype(out_ref.dtype)

        vw = jnp.concatenate([lanes(vt_ref, vv(h), c) * w.astype(jnp.bfloat16), _hi_lo_rows(w)], axis=0)
        upd = _dot(vw, k_ref[tok(c), qk(h)])
        decay = inter[:, L - 1:L]
        ct[h] = decay * ct[h] + upd[0:DV, :]
        n[h] = decay * n[h] + upd[DV:DV + 1, :] + upd[DV + 1:DV + 2, :]

    scored = {0: scores(*units[0])}
    if len(units) > 1:
        scored[1] = scores(*units[1])
    weighed = {0: weigh(*units[0], scored.pop(0))}
    for i, u in enumerate(units):
        if i + 2 < len(units):
            scored[i + 2] = scores(*units[i + 2])
        if i + 1 < len(units):
            weighed[i + 1] = weigh(*units[i + 1], scored.pop(i + 1))
        finish(*u, *weighed.pop(i))

    for h in range(N_HEADS):
        c_ref[h] = ct[h]
        n_ref[h] = n[h]


def kernel(x, pool_norm, pool_w, pool_scale, mlstm_norm, mlstm_w_in, mlstm_gate_bias,
           mlstm_head_norm, mlstm_w_out, ffn_norm, ffn_w_in, ffn_w_out, final_norm):
    B, S, D = x.shape
    depth = ffn_norm.shape[0]
    bf16 = jnp.bfloat16
    groups = pool_w.shape[:2] + (POOL_GROUP_DIM,)
    pool_w_b = (pool_norm.reshape(groups)[..., :, None] * pool_w
                * pool_scale.reshape(groups)[..., None, :]).astype(bf16)
    w_t_b = jnp.swapaxes(
        jnp.pad(mlstm_w_in * mlstm_norm[:, :, None], ((0, 0), (0, 0), (0, W_T_ROWS - mlstm_w_in.shape[2]))),
        1, 2).astype(bf16)
    head_norm_b = jnp.broadcast_to(mlstm_head_norm[:, :, None], mlstm_head_norm.shape + (LANE,))
    gate_bias = mlstm_gate_bias.reshape(-1, N_GATES, 1)
    w_mix_b = mlstm_w_out.astype(bf16)
    fgain = final_norm.reshape(1, D)
    ffn_gain_b = jnp.broadcast_to(ffn_norm[:, :, None], ffn_norm.shape + (LANE,))

    h2 = x.reshape(B * S, D)
    for i in range(depth):
        j = i // 2
        last = i == depth - 1
        if i % 2 == 0:
            h2 = _ffn_layer(h2, S, ffn_gain_b, ffn_w_in, ffn_w_out, fgain, i, last,
                            pool_w=pool_w_b, mix_layer=j)
        else:
            qt, k, vt, ogt, gates_t = _proj_layer(h2, w_t_b, head_norm_b, j)
            rows = _gate_layer(gates_t, gate_bias, B, S, j)
            h2 = _ffn_layer(h2, S, ffn_gain_b, ffn_w_in, ffn_w_out, fgain, i, last,
                            mlstm=(qt, k, vt, rows), out_gate=ogt, w_mix=w_mix_b, mix_layer=j)
    return h2.reshape(B, S, D)
```

```python
import functools
import math

import jax
import jax.numpy as jnp
from jax import lax
from jax.experimental import pallas as pl
from jax.experimental.pallas import tpu as pltpu

D_MODEL = 1024
POOL_WINDOWS = (2, 4, 8, 16)
POOL_GROUP_DIM = D_MODEL // len(POOL_WINDOWS)
POOL_HALO = 32

N_HEADS = 4
DV = D_MODEL // N_HEADS
DQK = DV // 2
QK_W = N_HEADS * DQK
V_W = N_HEADS * DV
MAIN_W = 2 * QK_W + 2 * V_W
N_GATES = 2 * N_HEADS
W_T_ROWS = MAIN_W + 16

D_FF = int(math.ceil(8 * D_MODEL / 3 / 256) * 256)
FF_CHUNK = 256
LANE = 128

EPS = 1e-6

POOL_BLOCK = 128
FFN_TILE = 512
PROJ_TILE = 1024
MLSTM_CHUNK = 256
N_GATE_ROWS = 24

VMEM_LIMIT = 56 * 1024 * 1024

_NT = (((1,), (1,)), ((), ()))


def _params(*semantics):
    return pltpu.CompilerParams(dimension_semantics=semantics,
                                vmem_limit_bytes=VMEM_LIMIT)


def _resident(shape, layer=None):
    zeros = (0,) * len(shape)
    if layer is None:
        return pl.BlockSpec(shape, lambda *_: zeros, pipeline_mode=pl.Buffered(1))
    return pl.BlockSpec((None,) + shape, lambda *_: (layer,) + zeros,
                        pipeline_mode=pl.Buffered(1))


def _rms_norm(x, gain):
    return x * lax.rsqrt(jnp.mean(x * x, axis=-1, keepdims=True) + EPS) * gain


def _dot(a, b, dims=None):
    if dims is None:
        return jnp.dot(a, b, preferred_element_type=jnp.float32)
    return lax.dot_general(a, b, dims, preferred_element_type=jnp.float32)


def _pool_rows(x_ref, r0, r1, tile_in_seq, ext_ref, stage_refs, pooled_ref):
    gd = POOL_GROUP_DIM
    lo, hi = POOL_HALO + r0, POOL_HALO + r1
    x = x_ref[r0:r1, :]
    xn = x * lax.rsqrt(jnp.mean(x * x, axis=-1, keepdims=True) + EPS)
    ext_ref[lo:hi, :] = xn

    prev = ext_ref
    for j, cur in enumerate(stage_refs, start=1):
        shift = 2 ** (j - 1)
        start = 8 * j if r0 == 0 else lo
        skip = 0 if j == 1 else gd
        cur[start:hi, :] = prev[start:hi, skip:] + prev[start - shift:hi - shift, skip:]
        prev = cur

    ts = x_ref.shape[0]
    pos = (tile_in_seq * ts + r0 + 1
           + lax.broadcasted_iota(jnp.int32, (r1 - r0, 1), 0)).astype(jnp.float32)
    for g, win in enumerate(POOL_WINDOWS):
        cols = slice(g * gd, (g + 1) * gd)
        inv_cnt = 1.0 / jnp.minimum(pos, float(win))
        pooled = stage_refs[g][lo:hi, 0:gd] * inv_cnt - xn[:, cols]
        pooled_ref[r0:r1, cols] = pooled.astype(jnp.bfloat16)


def _pool_finish(x_ref, w_ref, ext_ref, pooled_ref, out_ref):
    ts = x_ref.shape[0]
    gd = POOL_GROUP_DIM
    for g in range(len(POOL_WINDOWS)):
        cols = slice(g * gd, (g + 1) * gd)
        out_ref[:, cols] = x_ref[:, cols] + _dot(pooled_ref[:, cols], w_ref[g])
    ext_ref[0:POOL_HALO, :] = ext_ref[ts:ts + POOL_HALO, :]


def _pool_jobs(x_ref, tile_in_seq, w_ref, ext_ref, stage_refs, pooled_ref, out_ref):
    jobs = [functools.partial(_pool_rows, x_ref, r0, r0 + POOL_BLOCK, tile_in_seq, ext_ref,
                              stage_refs, pooled_ref)
            for r0 in range(0, x_ref.shape[0], POOL_BLOCK)]
    jobs.append(functools.partial(_pool_finish, x_ref, w_ref, ext_ref, pooled_ref, out_ref))
    return jobs


def _ffn_kernel(*refs, layer, mixer_proj, final_norm, tiles_per_seq):
    if mixer_proj:
        x_ref, qt_ref, k_ref, vt_ref, r_ref, og_ref, w_mix_ref, *refs = refs
    else:
        x_first_ref, x_next_ref, pool_w_ref, *refs = refs
    (gain_ref, w_in_hbm, w_out_hbm, fgain_ref, o_ref,
     w_in_ref, w_out_ref, act_ref, stage_in, stage_out, sem, *mixer_scratch) = refs
    n_chunks = D_FF // FF_CHUNK
    step = pl.program_id(0)
    if mixer_proj:
        heads_ref, c_ref, n_ref = mixer_scratch
        tiles_per_proj = PROJ_TILE // FFN_TILE

        @pl.when(step % tiles_per_seq == 0)
        def _():
            c_ref[...] = jnp.zeros(c_ref.shape, jnp.float32)
            n_ref[...] = jnp.zeros(n_ref.shape, jnp.float32)

        @pl.when(step % tiles_per_proj == 0)
        def _():
            _mlstm_chunks(qt_ref, k_ref, vt_ref, r_ref, heads_ref, c_ref, n_ref, PROJ_TILE // MLSTM_CHUNK)
    else:
        mixed_ref, cur_ref, pooled_ref, ext_ref, *stage_refs = mixer_scratch
        next_in_seq = (step + 1) % tiles_per_seq

        @pl.when(next_in_seq == 0)
        def _():
            ext_ref[0:POOL_HALO, :] = jnp.zeros((POOL_HALO, D_MODEL), jnp.float32)

    def chunk_copies(j, slot):
        lo = j * FF_CHUNK
        return (
            pltpu.make_async_copy(w_in_hbm.at[layer, :, lo:lo + FF_CHUNK],
                                  stage_in.at[slot, 0], sem.at[slot, 0]),
            pltpu.make_async_copy(w_in_hbm.at[layer, :, D_FF + lo:D_FF + lo + FF_CHUNK],
                                  stage_in.at[slot, 1], sem.at[slot, 1]),
            pltpu.make_async_copy(w_out_hbm.at[layer, lo:lo + FF_CHUNK, :],
                                  stage_out.at[slot], sem.at[slot, 2]),
        )

    def body(stream_weights):
        if stream_weights:
            for j in range(min(2, n_chunks)):
                for cp in chunk_copies(j, j):
                    cp.start()
        side_jobs = []
        if mixer_proj:
            heads = heads_ref[step % tiles_per_proj]
            x = x_ref[...] + _dot(heads * og_ref[...], w_mix_ref[...], (((0,), (0,)), ((), ())))
        else:
            if stream_weights:
                ext_ref[0:POOL_HALO, :] = jnp.zeros((POOL_HALO, D_MODEL), jnp.float32)
                for job in _pool_jobs(x_first_ref, 0, pool_w_ref, ext_ref, stage_refs, pooled_ref,
                                      mixed_ref):
                    job()
            cur_ref[...] = mixed_ref[...]
            x = cur_ref[...]
            side_jobs = _pool_jobs(x_next_ref, next_in_seq, pool_w_ref, ext_ref, stage_refs,
                                   pooled_ref, mixed_ref)
        xb = x.astype(jnp.bfloat16)
        rstd = lax.rsqrt(jnp.mean(x * x, axis=-1, keepdims=True) + EPS)
        for j in range(n_chunks):
            lo = j * FF_CHUNK
            if stream_weights:
                slot = j % 2
                for cp in chunk_copies(j, slot):
                    cp.wait()
                for part, col in ((0, lo), (1, D_FF + lo)):
                    for sub in range(0, FF_CHUNK, LANE):
                        w_in_ref[:, col + sub:col + sub + LANE] = (
                            stage_in[slot, part, :, sub:sub + LANE] * gain_ref[...]).astype(jnp.bfloat16)
                w_out_ref[lo:lo + FF_CHUNK, :] = stage_out[slot].astype(jnp.bfloat16)
                if j + 2 < n_chunks:
                    for cp in chunk_copies(j + 2, slot):
                        cp.start()
            gate = _dot(xb, w_in_ref[:, lo:lo + FF_CHUNK]) * rstd
            up = _dot(xb, w_in_ref[:, D_FF + lo:D_FF + lo + FF_CHUNK]) * rstd
            act_ref[:, lo:lo + FF_CHUNK] = (gate * jax.nn.sigmoid(gate) * up).astype(jnp.bfloat16)
            if j < len(side_jobs):
                side_jobs[j]()
        out = x + _dot(act_ref[...], w_out_ref[...])
        if final_norm:
            out = _rms_norm(out, fgain_ref[...])
        o_ref[...] = out

    first = step == 0
    pl.when(first)(functools.partial(body, True))
    pl.when(jnp.logical_not(first))(functools.partial(body, False))


def _ffn_layer(h2, S, gain, w_in, w_out, final_gain, layer, final_norm, mlstm=None, out_gate=None,
               w_mix=None, pool_w=None, mix_layer=None):
    T, D = h2.shape
    tm = FFN_TILE
    nt = T // tm
    assert S // tm > 1 and tm % POOL_BLOCK == 0 and tm // POOL_BLOCK < D_FF // FF_CHUNK
    tok = pl.BlockSpec((tm, D), lambda i: (i, 0))
    hbm = pl.BlockSpec(memory_space=pl.ANY)
    mixer_proj = mlstm is not None
    if mixer_proj:
        tpp = PROJ_TILE // tm
        ppq = S // PROJ_TILE
        assert PROJ_TILE % tm == 0 and S % PROJ_TILE == 0
        proj_tile = lambda i: (i // tpp, 0, 0)
        mix_specs = [
            tok,
            pl.BlockSpec((1, QK_W, PROJ_TILE), proj_tile),
            pl.BlockSpec((PROJ_TILE, QK_W), lambda i: (i // tpp, 0)),
            pl.BlockSpec((1, V_W, PROJ_TILE), proj_tile),
            pl.BlockSpec((1, N_GATE_ROWS, PROJ_TILE), lambda i: (i // tpp // ppq, 0, i // tpp % ppq)),
            pl.BlockSpec((None, D, tm), lambda i: (i, 0, 0)),
            _resident(w_mix.shape[1:], mix_layer),
        ]
        mix_args = (h2, *mlstm, out_gate, w_mix)
        mixer_scratch = [
            pltpu.VMEM((tpp, V_W, tm), jnp.bfloat16),
            pltpu.VMEM((N_HEADS, DV, DQK), jnp.float32),
            pltpu.VMEM((N_HEADS, 1, DQK), jnp.float32),
        ]
    else:
        mix_specs = [pl.BlockSpec((tm, D), lambda i: (0, 0), pipeline_mode=pl.Buffered(1)),
                     pl.BlockSpec((tm, D), lambda i: (jnp.minimum(i + 1, nt - 1), 0)),
                     _resident(pool_w.shape[1:], mix_layer)]
        mix_args = (h2, h2, pool_w)
        mixer_scratch = [
            pltpu.VMEM((tm, D), jnp.float32),
            pltpu.VMEM((tm, D), jnp.float32),
            pltpu.VMEM((tm, D), jnp.bfloat16),
            pltpu.VMEM((POOL_HALO + tm, D), jnp.float32),
        ] + [pltpu.VMEM((POOL_HALO + tm, D - g * POOL_GROUP_DIM), jnp.float32)
             for g in range(len(POOL_WINDOWS))]
    return pl.pallas_call(
        functools.partial(_ffn_kernel, layer=layer, mixer_proj=mixer_proj, final_norm=final_norm,
                          tiles_per_seq=S // tm),
        grid=(nt,),
        in_specs=mix_specs + [_resident((D, LANE), layer), hbm, hbm, _resident((1, D))],
        out_specs=tok,
        out_shape=jax.ShapeDtypeStruct(h2.shape, h2.dtype),
        scratch_shapes=[
            pltpu.VMEM((D, 2 * D_FF), jnp.bfloat16),
            pltpu.VMEM((D_FF, D), jnp.bfloat16),
            pltpu.VMEM((tm, D_FF), jnp.bfloat16),
            pltpu.VMEM((2, 2, D, FF_CHUNK), jnp.float32),
            pltpu.VMEM((2, FF_CHUNK, D), jnp.float32),
            pltpu.SemaphoreType.DMA((2, 3)),
        ] + mixer_scratch,
        compiler_params=_params("arbitrary"),
        name="swiglu" + ("_mix" if mixer_proj else "_pool") + ("_final" if final_norm else ""),
    )(*mix_args, gain, w_in, w_out, final_gain)


def _proj_kernel(x_ref, wt_ref, hn_ref, qt_ref, k_ref, vt_ref, ogt_ref, gt_ref):
    x = x_ref[...]
    xb = x.astype(jnp.bfloat16)
    rstd = lax.rsqrt(jnp.mean(x * x, axis=-1, keepdims=True) + EPS)
    rstd_row = jnp.broadcast_to(rstd, (x.shape[0], LANE)).T[0:1, :]
    og = _dot(wt_ref[2 * QK_W + V_W:, :], xb, _NT) * rstd_row
    gt_ref[...] = og[V_W:V_W + N_GATES, :]
    for lo in range(0, og.shape[1], LANE):
        ogt_ref[lo // FFN_TILE, :, lo % FFN_TILE:lo % FFN_TILE + LANE] = (
            jax.nn.sigmoid(og[0:V_W, lo:lo + LANE]) * hn_ref[...]).astype(ogt_ref.dtype)
    vt_ref[...] = (_dot(wt_ref[2 * QK_W:2 * QK_W + V_W, :], xb, _NT) * rstd_row).astype(vt_ref.dtype)
    qt_ref[...] = (_dot(wt_ref[0:QK_W, :], xb, _NT) * (rstd_row * (DQK ** -0.5))).astype(qt_ref.dtype)
    k_ref[...] = (_dot(xb, wt_ref[QK_W:2 * QK_W, :], _NT) * rstd).astype(k_ref.dtype)


def _proj_layer(h2, w_t, head_norm_b, layer):
    T, D = h2.shape
    tm = PROJ_TILE
    tok = lambda i: (i, 0)
    lanes = lambda i: (0, i)
    tile = lambda i: (i, 0, 0)
    return pl.pallas_call(
        _proj_kernel,
        grid=(T // tm,),
        in_specs=[
            pl.BlockSpec((tm, D), tok),
            _resident(w_t.shape[1:], layer),
            _resident(head_norm_b.shape[1:], layer),
        ],
        out_specs=[
            pl.BlockSpec((None, QK_W, tm), tile),
            pl.BlockSpec((tm, QK_W), tok),
            pl.BlockSpec((None, V_W, tm), tile),
            pl.BlockSpec((tm // FFN_TILE, V_W, FFN_TILE), tile),
            pl.BlockSpec((N_GATES, tm), lanes),
        ],
        out_shape=[
            jax.ShapeDtypeStruct((T // tm, QK_W, tm), jnp.bfloat16),
            jax.ShapeDtypeStruct((T, QK_W), jnp.bfloat16),
            jax.ShapeDtypeStruct((T // tm, V_W, tm), jnp.bfloat16),
            jax.ShapeDtypeStruct((T // FFN_TILE, V_W, FFN_TILE), jnp.bfloat16),
            jax.ShapeDtypeStruct((N_GATES, T), jnp.float32),
        ],
        compiler_params=_params("arbitrary"),
        name="mlstm_proj",
    )(h2, w_t, head_norm_b)


def _segmented_scan(x, op, lane, seg):
    shift = 1
    while shift < seg:
        moved = pltpu.roll(x, shift, 1)
        x = jnp.where((lane % seg) >= shift, op(x, moved), x)
        shift *= 2
    return x


def _gate_kernel(gt_ref, bias_ref, r_ref):
    L = MLSTM_CHUNK
    B, _, S = r_ref.shape
    g = gt_ref[...] + bias_ref[...]
    per_seq = lambda lo: jnp.concatenate(
        [g[lo:lo + N_HEADS, s * S:(s + 1) * S] for s in range(B)], axis=0)
    i_pre = per_seq(0)
    f_pre = per_seq(N_HEADS)
    logf = jnp.minimum(f_pre, 0.0) - jnp.log1p(jnp.exp(-jnp.abs(f_pre)))
    lane = lax.broadcasted_iota(jnp.int32, (B * N_HEADS, S), 1)
    b = _segmented_scan(logf, jnp.add, lane, L)
    a = i_pre - b
    amax = _segmented_scan(a, jnp.maximum, lane, L)

    def put(kind, lanes, value):
        for s in range(B):
            r_ref[s, kind * N_HEADS:(kind + 1) * N_HEADS, lanes] = value[s * N_HEADS:(s + 1) * N_HEADS]

    put(0, slice(0, S), a)
    r_ref[:, 5 * N_HEADS:N_GATE_ROWS, :] = jnp.zeros((B, N_GATE_ROWS - 5 * N_HEADS, S), jnp.float32)
    m_prev = jnp.zeros((B * N_HEADS, 1), jnp.float32)
    for c in range(S // L):
        seg = slice(c * L, (c + 1) * L)
        big_m = jnp.maximum(m_prev, amax[:, seg])
        m_last = big_m[:, L - 1:L]
        put(1, seg, big_m)
        put(2, seg, -(b[:, seg] + big_m))
        put(3, seg, m_prev - big_m)
        put(4, seg, a[:, seg] - m_last)
        m_prev = b[:, (c + 1) * L - 1:(c + 1) * L] + m_last


def _gate_layer(gates_t, bias, B, S, layer):
    return pl.pallas_call(
        _gate_kernel,
        grid=(1,),
        in_specs=[
            pl.BlockSpec((N_GATES, B * S), lambda i: (0, 0)),
            _resident((N_GATES, 1), layer),
        ],
        out_specs=pl.BlockSpec((B, N_GATE_ROWS, S), lambda i: (0, 0, 0)),
        out_shape=jax.ShapeDtypeStruct((B, N_GATE_ROWS, S), jnp.float32),
        compiler_params=_params("arbitrary"),
        name="mlstm_gates",
    )(gates_t, bias)


def _hi_lo_rows(row):
    hi = row.astype(jnp.bfloat16).astype(jnp.float32)
    idx = lax.broadcasted_iota(jnp.int32, (16, row.shape[1]), 0)
    slab = jnp.where(idx == 0, hi, jnp.where(idx == 1, row - hi, 0.0))
    return slab.astype(jnp.bfloat16)


def _mlstm_chunks(qt_ref, k_ref, vt_ref, r_ref, out_ref, c_ref, n_ref, n_chunks):
    L = MLSTM_CHUNK
    H = L // 2

    causal = (lax.broadcasted_iota(jnp.int32, (H, H), 0)
              <= lax.broadcasted_iota(jnp.int32, (H, H), 1))

    tok = lambda c: slice(c * L, (c + 1) * L)
    lanes = lambda ref, rows, c: ref[(c * L) // PROJ_TILE, rows,
                                     (c * L) % PROJ_TILE:(c * L) % PROJ_TILE + L]
    qk = lambda h: slice(h * DQK, (h + 1) * DQK)
    vv = lambda h: slice(h * DV, (h + 1) * DV)
    units = [(c, h) for c in range(n_chunks) for h in range(N_HEADS)]
    rows = [r_ref[0, :, tok(c)] for c in range(n_chunks)]
    a_cols = [r[0:2 * N_HEADS, :].T for r in rows]
    gate_row = lambda c, h, kind: rows[c][kind * N_HEADS + h:kind * N_HEADS + h + 1, :]
    ct = [c_ref[h] for h in range(N_HEADS)]
    n = [n_ref[h] for h in range(N_HEADS)]

    def scores(c, h):
        lhs = jnp.concatenate([k_ref[tok(c), qk(h)], ct[h].astype(jnp.bfloat16), _hi_lo_rows(n[h])], axis=0)
        return _dot(lhs, lanes(qt_ref, qk(h), c))

    def weigh(c, h, both):
        s, from_state = both[0:L, :], both[L:, :]
        a_col = a_cols[c][:, h:h + 1]
        big_m = gate_row(c, h, 1)
        diag = lambda i: jnp.exp(jnp.where(
            causal, a_col[i * H:(i + 1) * H, :] - big_m[:, i * H:(i + 1) * H], -jnp.inf))
        above = jnp.exp(a_col[0:H, :] - big_m[:, H:L])
        top = s[0:H, :] * jnp.concatenate([diag(0), above], axis=1)
        bottom = jnp.concatenate([jnp.zeros((H, H), jnp.float32), s[H:L, H:L] * diag(1)], axis=1)
        st = jnp.concatenate([top, bottom], axis=0)
        den_intra = jnp.sum(st, axis=0, keepdims=True)
        num_intra = _dot(lanes(vt_ref, vv(h), c), st.astype(jnp.bfloat16))
        return den_intra, num_intra, from_state

    def finish(c, h, den_intra, num_intra, from_state):
        inter = jnp.exp(gate_row(c, h, 3))
        w = jnp.exp(gate_row(c, h, 4))
        qn = from_state[DV:DV + 1, :] + from_state[DV + 1:DV + 2, :]
        num = inter * from_state[0:DV, :] + num_intra
        den = inter * qn + den_intra
        r = 1.0 / jnp.maximum(jnp.abs(den), jnp.exp(gate_row(c, h, 2)))
        ss = jnp.sum(num * num, axis=0, keepdims=True)
        scale = r * lax.rsqrt(r * r * ss * (1.0 / DV) + EPS)
        lane0 = (c * L) % FFN_TILE
        out_ref[(c * L) // FFN_TILE, vv(h), lane0:lane0 + L] = (num * scale).astype(out_ref.dtype)

        vw = jnp.concatenate([lanes(vt_ref, vv(h), c) * w.astype(jnp.bfloat16), _hi_lo_rows(w)], axis=0)
        upd = _dot(vw, k_ref[tok(c), qk(h)])
        decay = inter[:, L - 1:L]
        ct[h] = decay * ct[h] + upd[0:DV, :]
        n[h] = decay * n[h] + upd[DV:DV + 1, :] + upd[DV + 1:DV + 2, :]

    scored = {0: scores(*units[0])}
    if len(units) > 1:
        scored[1] = scores(*units[1])
    weighed = {0: weigh(*units[0], scored.pop(0))}
    for i, u in enumerate(units):
        if i + 2 < len(units):
            scored[i + 2] = scores(*units[i + 2])
        if i + 1 < len(units):
            weighed[i + 1] = weigh(*units[i + 1], scored.pop(i + 1))
        finish(*u, *weighed.pop(i))

    for h in range(N_HEADS):
        c_ref[h] = ct[h]
        n_ref[h] = n[h]


def kernel(x, pool_norm, pool_w, pool_scale, mlstm_norm, mlstm_w_in, mlstm_gate_bias,
           mlstm_head_norm, mlstm_w_out, ffn_norm, ffn_w_in, ffn_w_out, final_norm):
    B, S, D = x.shape
    depth = ffn_norm.shape[0]
    bf16 = jnp.bfloat16
    groups = pool_w.shape[:2] + (POOL_GROUP_DIM,)
    pool_w_b = (pool_norm.reshape(groups)[..., :, None] * pool_w
                * pool_scale.reshape(groups)[..., None, :]).astype(bf16)
    w_t_b = jnp.swapaxes(
        jnp.pad(mlstm_w_in * mlstm_norm[:, :, None], ((0, 0), (0, 0), (0, W_T_ROWS - mlstm_w_in.shape[2]))),
        1, 2).astype(bf16)
    head_norm_b = jnp.broadcast_to(mlstm_head_norm[:, :, None], mlstm_head_norm.shape + (LANE,))
    gate_bias = mlstm_gate_bias.reshape(-1, N_GATES, 1)
    w_mix_b = mlstm_w_out.astype(bf16)
    fgain = final_norm.reshape(1, D)
    ffn_gain_b = jnp.broadcast_to(ffn_norm[:, :, None], ffn_norm.shape + (LANE,))

    h2 = x.reshape(B * S, D)
    for i in range(depth):
        j = i // 2
        last = i == depth - 1
        if i % 2 == 0:
            h2 = _ffn_layer(h2, S, ffn_gain_b, ffn_w_in, ffn_w_out, fgain, i, last,
                            pool_w=pool_w_b, mix_layer=j)
        else:
            qt, k, vt, ogt, gates_t = _proj_layer(h2, w_t_b, head_norm_b, j)
            rows = _gate_layer(gates_t, gate_bias, B, S, j)
            h2 = _ffn_layer(h2, S, ffn_gain_b, ffn_w_in, ffn_w_out, fgain, i, last,
                            mlstm=(qt, k, vt, rows), out_gate=ogt, w_mix=w_mix_b, mix_layer=j)
    return h2.reshape(B, S, D)
```

```python
import functools
import math

import jax
import jax.numpy as jnp
from jax import lax
from jax.experimental import pallas as pl
from jax.experimental.pallas import tpu as pltpu

D_MODEL = 1024
POOL_WINDOWS = (2, 4, 8, 16)
POOL_GROUP_DIM = D_MODEL // len(POOL_WINDOWS)
POOL_HALO = 32

N_HEADS = 4
DV = D_MODEL // N_HEADS
DQK = DV // 2
QK_W = N_HEADS * DQK
V_W = N_HEADS * DV
MAIN_W = 2 * QK_W + 2 * V_W
N_GATES = 2 * N_HEADS
W_T_ROWS = MAIN_W + 16

D_FF = int(math.ceil(8 * D_MODEL / 3 / 256) * 256)
FF_CHUNK = 256
LANE = 128

EPS = 1e-6

POOL_BLOCK = 128
FFN_TILE = 512
PROJ_TILE = 1024
MLSTM_CHUNK = 256
N_GATE_ROWS = 24

VMEM_LIMIT = 56 * 1024 * 1024

_NT = (((1,), (1,)), ((), ()))


def _params(*semantics):
    return pltpu.CompilerParams(dimension_semantics=semantics,
                                vmem_limit_bytes=VMEM_LIMIT)


def _resident(shape, layer=None):
    zeros = (0,) * len(shape)
    if layer is None:
        return pl.BlockSpec(shape, lambda *_: zeros, pipeline_mode=pl.Buffered(1))
    return pl.BlockSpec((None,) + shape, lambda *_: (layer,) + zeros,
                        pipeline_mode=pl.Buffered(1))


def _rms_norm(x, gain):
    return x * lax.rsqrt(jnp.mean(x * x, axis=-1, keepdims=True) + EPS) * gain


def _dot(a, b, dims=None):
    if dims is None:
        return jnp.dot(a, b, preferred_element_type=jnp.float32)
    return lax.dot_general(a, b, dims, preferred_element_type=jnp.float32)


def _pool_rows(x_ref, r0, r1, tile_in_seq, ext_ref, stage_refs, pooled_ref):
    gd = POOL_GROUP_DIM
    lo, hi = POOL_HALO + r0, POOL_HALO + r1
    x = x_ref[r0:r1, :]
    xn = x * lax.rsqrt(jnp.mean(x * x, axis=-1, keepdims=True) + EPS)
    ext_ref[lo:hi, :] = xn

    prev = ext_ref
    for j, cur in enumerate(stage_refs, start=1):
        shift = 2 ** (j - 1)
        start = 8 * j if r0 == 0 else lo
        skip = 0 if j == 1 else gd
        cur[start:hi, :] = prev[start:hi, skip:] + prev[start - shift:hi - shift, skip:]
        prev = cur

    ts = x_ref.shape[0]
    pos = (tile_in_seq * ts + r0 + 1
           + lax.broadcasted_iota(jnp.int32, (r1 - r0, 1), 0)).astype(jnp.float32)
    for g, win in enumerate(POOL_WINDOWS):
        cols = slice(g * gd, (g + 1) * gd)
        inv_cnt = 1.0 / jnp.minimum(pos, float(win))
        pooled = stage_refs[g][lo:hi, 0:gd] * inv_cnt - xn[:, cols]
        pooled_ref[r0:r1, cols] = pooled.astype(jnp.bfloat16)


def _pool_finish(x_ref, w_ref, ext_ref, pooled_ref, out_ref):
    ts = x_ref.shape[0]
    gd = POOL_GROUP_DIM
    for g in range(len(POOL_WINDOWS)):
        cols = slice(g * gd, (g + 1) * gd)
        out_ref[:, cols] = x_ref[:, cols] + _dot(pooled_ref[:, cols], w_ref[g])
    ext_ref[0:POOL_HALO, :] = ext_ref[ts:ts + POOL_HALO, :]


def _pool_jobs(x_ref, tile_in_seq, w_ref, ext_ref, stage_refs, pooled_ref, out_ref):
    jobs = [functools.partial(_pool_rows, x_ref, r0, r0 + POOL_BLOCK, tile_in_seq, ext_ref,
                              stage_refs, pooled_ref)
            for r0 in range(0, x_ref.shape[0], POOL_BLOCK)]
    jobs.append(functools.partial(_pool_finish, x_ref, w_ref, ext_ref, pooled_ref, out_ref))
    return jobs


def _ffn_kernel(*refs, layer, mixer_proj, final_norm, tiles_per_seq):
    if mixer_proj:
        x_ref, qt_ref, k_ref, vt_ref, r_ref, og_ref, w_mix_ref, *refs = refs
    else:
        x_first_ref, x_next_ref, pool_w_ref, *refs = refs
    (gain_ref, w_in_hbm, w_out_hbm, fgain_ref, o_ref,
     w_in_ref, w_out_ref, act_ref, stage_in, stage_out, sem, *mixer_scratch) = refs
    n_chunks = D_FF // FF_CHUNK
    step = pl.program_id(0)
    if mixer_proj:
        heads_ref, c_ref, n_ref = mixer_scratch
        tiles_per_proj = PROJ_TILE // FFN_TILE

        @pl.when(step % tiles_per_seq == 0)
        def _():
            c_ref[...] = jnp.zeros(c_ref.shape, jnp.float32)
            n_ref[...] = jnp.zeros(n_ref.shape, jnp.float32)

        @pl.when(step % tiles_per_proj == 0)
        def _():
            _mlstm_chunks(qt_ref, k_ref, vt_ref, r_ref, heads_ref, c_ref, n_ref, PROJ_TILE // MLSTM_CHUNK)
    else:
        mixed_ref, cur_ref, pooled_ref, ext_ref, *stage_refs = mixer_scratch
        next_in_seq = (step + 1) % tiles_per_seq

        @pl.when(next_in_seq == 0)
        def _():
            ext_ref[0:POOL_HALO, :] = jnp.zeros((POOL_HALO, D_MODEL), jnp.float32)

    def chunk_copies(j, slot):
        lo = j * FF_CHUNK
        return (
            pltpu.make_async_copy(w_in_hbm.at[layer, :, lo:lo + FF_CHUNK],
                                  stage_in.at[slot, 0], sem.at[slot, 0]),
            pltpu.make_async_copy(w_in_hbm.at[layer, :, D_FF + lo:D_FF + lo + FF_CHUNK],
                                  stage_in.at[slot, 1], sem.at[slot, 1]),
            pltpu.make_async_copy(w_out_hbm.at[layer, lo:lo + FF_CHUNK, :],
                                  stage_out.at[slot], sem.at[slot, 2]),
        )

    def body(stream_weights):
        if stream_weights:
            for j in range(min(2, n_chunks)):
                for cp in chunk_copies(j, j):
                    cp.start()
        side_jobs = []
        if mixer_proj:
            heads = heads_ref[step % tiles_per_proj]
            x = x_ref[...] + _dot(heads * og_ref[...], w_mix_ref[...], (((0,), (0,)), ((), ())))
        else:
            if stream_weights:
                ext_ref[0:POOL_HALO, :] = jnp.zeros((POOL_HALO, D_MODEL), jnp.float32)
                for job in _pool_jobs(x_first_ref, 0, pool_w_ref, ext_ref, stage_refs, pooled_ref,
                                      mixed_ref):
                    job()
            cur_ref[...] = mixed_ref[...]
            x = cur_ref[...]
            side_jobs = _pool_jobs(x_next_ref, next_in_seq, pool_w_ref, ext_ref, stage_refs,
                                   pooled_ref, mixed_ref)
        xb = x.astype(jnp.bfloat16)
        rstd = lax.rsqrt(jnp.mean(x * x, axis=-1, keepdims=True) + EPS)
        for j in range(n_chunks):
            lo = j * FF_CHUNK
            if stream_weights:
                slot = j % 2
                for cp in chunk_copies(j, slot):
                    cp.wait()
                for part, col in ((0, lo), (1, D_FF + lo)):
                    for sub in range(0, FF_CHUNK, LANE):
                        w_in_ref[:, col + sub:col + sub + LANE] = (
                            stage_in[slot, part, :, sub:sub + LANE] * gain_ref[...]).astype(jnp.bfloat16)
                w_out_ref[lo:lo + FF_CHUNK, :] = stage_out[slot].astype(jnp.bfloat16)
                if j + 2 < n_chunks:
                    for cp in chunk_copies(j + 2, slot):
                        cp.start()
            gate = _dot(xb, w_in_ref[:, lo:lo + FF_CHUNK]) * rstd
            up = _dot(xb, w_in_ref[:, D_FF + lo:D_FF + lo + FF_CHUNK]) * rstd
            act_ref[:, lo:lo + FF_CHUNK] = (gate * jax.nn.sigmoid(gate) * up).astype(jnp.bfloat16)
            if j < len(side_jobs):
                side_jobs[j]()
        out = x + _dot(act_ref[...], w_out_ref[...])
        if final_norm:
            out = _rms_norm(out, fgain_ref[...])
        o_ref[...] = out

    first = step == 0
    pl.when(first)(functools.partial(body, True))
    pl.when(jnp.logical_not(first))(functools.partial(body, False))


def _ffn_layer(h2, S, gain, w_in, w_out, final_gain, layer, final_norm, mlstm=None, out_gate=None,
               w_mix=None, pool_w=None, mix_layer=None):
    T, D = h2.shape
    tm = FFN_TILE
    nt = T // tm
    assert S // tm > 1 and tm % POOL_BLOCK == 0 and tm // POOL_BLOCK < D_FF // FF_CHUNK
    tok = pl.BlockSpec((tm, D), lambda i: (i, 0))
    hbm = pl.BlockSpec(memory_space=pl.ANY)
    mixer_proj = mlstm is not None
    if mixer_proj:
        tpp = PROJ_TILE // tm
        ppq = S // PROJ_TILE
        assert PROJ_TILE % tm == 0 and S % PROJ_TILE == 0
        proj_tile = lambda i: (i // tpp, 0, 0)
        mix_specs = [
            tok,
            pl.BlockSpec((1, QK_W, PROJ_TILE), proj_tile),
            pl.BlockSpec((PROJ_TILE, QK_W), lambda i: (i // tpp, 0)),
            pl.BlockSpec((1, V_W, PROJ_TILE), proj_tile),
            pl.BlockSpec((1, N_GATE_ROWS, PROJ_TILE), lambda i: (i // tpp // ppq, 0, i // tpp % ppq)),
            pl.BlockSpec((None, D, tm), lambda i: (i, 0, 0)),
            _resident(w_mix.shape[1:], mix_layer),
        ]
        mix_args = (h2, *mlstm, out_gate, w_mix)
        mixer_scratch = [
            pltpu.VMEM((tpp, V_W, tm), jnp.bfloat16),
            pltpu.VMEM((N_HEADS, DV, DQK), jnp.float32),
            pltpu.VMEM((N_HEADS, 1, DQK), jnp.float32),
        ]
    else:
        mix_specs = [pl.BlockSpec((tm, D), lambda i: (0, 0), pipeline_mode=pl.Buffered(1)),
                     pl.BlockSpec((tm, D), lambda i: (jnp.minimum(i + 1, nt - 1), 0)),
                     _resident(pool_w.shape[1:], mix_layer)]
        mix_args = (h2, h2, pool_w)
        mixer_scratch = [
            pltpu.VMEM((tm, D), jnp.float32),
            pltpu.VMEM((tm, D), jnp.float32),
            pltpu.VMEM((tm, D), jnp.bfloat16),
            pltpu.VMEM((POOL_HALO + tm, D), jnp.float32),
        ] + [pltpu.VMEM((POOL_HALO + tm, D - g * POOL_GROUP_DIM), jnp.float32)
             for g in range(len(POOL_WINDOWS))]
    return pl.pallas_call(
        functools.partial(_ffn_kernel, layer=layer, mixer_proj=mixer_proj, final_norm=final_norm,
                          tiles_per_seq=S // tm),
        grid=(nt,),
        in_specs=mix_specs + [_resident((D, LANE), layer), hbm, hbm, _resident((1, D))],
        out_specs=tok,
        out_shape=jax.ShapeDtypeStruct(h2.shape, h2.dtype),
        scratch_shapes=[
            pltpu.VMEM((D, 2 * D_FF), jnp.bfloat16),
            pltpu.VMEM((D_FF, D), jnp.bfloat16),
            pltpu.VMEM((tm, D_FF), jnp.bfloat16),
            pltpu.VMEM((2, 2, D, FF_CHUNK), jnp.float32),
            pltpu.VMEM((2, FF_CHUNK, D), jnp.float32),
            pltpu.SemaphoreType.DMA((2, 3)),
        ] + mixer_scratch,
        compiler_params=_params("arbitrary"),
        name="swiglu" + ("_mix" if mixer_proj else "_pool") + ("_final" if final_norm else ""),
    )(*mix_args, gain, w_in, w_out, final_gain)


def _proj_kernel(x_hbm, wt_ref, hn_ref, qt_ref, k_ref, vt_ref, ogt_ref, gt_ref, ring_ref, sem, *, n_steps):
    step = pl.program_id(0)
    tm = ring_ref.shape[1]

    def fetch(s, slot):
        return pltpu.make_async_copy(x_hbm.at[pl.ds(s * tm, tm), :], ring_ref.at[slot], sem.at[slot])

    @pl.when(step == 0)
    def _():
        for s in range(min(2, n_steps)):
            fetch(s, s).start()

    @pl.when(step + 2 < n_steps)
    def _():
        fetch(step + 2, (step + 2) % 3).start()

    slot = step % 3
    fetch(step, slot).wait()
    x = ring_ref[slot]
    xb = x.astype(jnp.bfloat16)
    rstd = lax.rsqrt(jnp.mean(x * x, axis=-1, keepdims=True) + EPS)
    rstd_row = jnp.broadcast_to(rstd, (x.shape[0], LANE)).T[0:1, :]
    og = _dot(wt_ref[2 * QK_W + V_W:, :], xb, _NT) * rstd_row
    gt_ref[...] = og[V_W:V_W + N_GATES, :]
    for lo in range(0, og.shape[1], LANE):
        ogt_ref[lo // FFN_TILE, :, lo % FFN_TILE:lo % FFN_TILE + LANE] = (
            jax.nn.sigmoid(og[0:V_W, lo:lo + LANE]) * hn_ref[...]).astype(ogt_ref.dtype)
    vt_ref[...] = (_dot(wt_ref[2 * QK_W:2 * QK_W + V_W, :], xb, _NT) * rstd_row).astype(vt_ref.dtype)
    qt_ref[...] = (_dot(wt_ref[0:QK_W, :], xb, _NT) * (rstd_row * (DQK ** -0.5))).astype(qt_ref.dtype)
    k_ref[...] = (_dot(xb, wt_ref[QK_W:2 * QK_W, :], _NT) * rstd).astype(k_ref.dtype)


def _proj_layer(h2, w_t, head_norm_b, layer):
    T, D = h2.shape
    tm = PROJ_TILE
    tok = lambda i: (i, 0)
    lanes = lambda i: (0, i)
    tile = lambda i: (i, 0, 0)
    return pl.pallas_call(
        functools.partial(_proj_kernel, n_steps=T // tm),
        grid=(T // tm,),
        in_specs=[
            pl.BlockSpec(memory_space=pl.ANY),
            _resident(w_t.shape[1:], layer),
            _resident(head_norm_b.shape[1:], layer),
        ],
        out_specs=[
            pl.BlockSpec((None, QK_W, tm), tile),
            pl.BlockSpec((tm, QK_W), tok),
            pl.BlockSpec((None, V_W, tm), tile),
            pl.BlockSpec((tm // FFN_TILE, V_W, FFN_TILE), tile),
            pl.BlockSpec((N_GATES, tm), lanes),
        ],
        out_shape=[
            jax.ShapeDtypeStruct((T // tm, QK_W, tm), jnp.bfloat16),
            jax.ShapeDtypeStruct((T, QK_W), jnp.bfloat16),
            jax.ShapeDtypeStruct((T // tm, V_W, tm), jnp.bfloat16),
            jax.ShapeDtypeStruct((T // FFN_TILE, V_W, FFN_TILE), jnp.bfloat16),
            jax.ShapeDtypeStruct((N_GATES, T), jnp.float32),
        ],
        scratch_shapes=[
            pltpu.VMEM((3, tm, D), jnp.float32),
            pltpu.SemaphoreType.DMA((3,)),
        ],
        compiler_params=_params("arbitrary"),
        name="mlstm_proj",
    )(h2, w_t, head_norm_b)


def _segmented_scan(x, op, lane, seg):
    shift = 1
    while shift < seg:
        moved = pltpu.roll(x, shift, 1)
        x = jnp.where((lane % seg) >= shift, op(x, moved), x)
        shift *= 2
    return x


def _gate_kernel(gt_ref, bias_ref, r_ref):
    L = MLSTM_CHUNK
    B, _, S = r_ref.shape
    g = gt_ref[...] + bias_ref[...]
    per_seq = lambda lo: jnp.concatenate(
        [g[lo:lo + N_HEADS, s * S:(s + 1) * S] for s in range(B)], axis=0)
    i_pre = per_seq(0)
    f_pre = per_seq(N_HEADS)
    logf = jnp.minimum(f_pre, 0.0) - jnp.log1p(jnp.exp(-jnp.abs(f_pre)))
    lane = lax.broadcasted_iota(jnp.int32, (B * N_HEADS, S), 1)
    b = _segmented_scan(logf, jnp.add, lane, L)
    a = i_pre - b
    amax = _segmented_scan(a, jnp.maximum, lane, L)

    def put(kind, lanes, value):
        for s in range(B):
            r_ref[s, kind * N_HEADS:(kind + 1) * N_HEADS, lanes] = value[s * N_HEADS:(s + 1) * N_HEADS]

    put(0, slice(0, S), a)
    r_ref[:, 5 * N_HEADS:N_GATE_ROWS, :] = jnp.zeros((B, N_GATE_ROWS - 5 * N_HEADS, S), jnp.float32)
    m_prev = jnp.zeros((B * N_HEADS, 1), jnp.float32)
    for c in range(S // L):
        seg = slice(c * L, (c + 1) * L)
        big_m = jnp.maximum(m_prev, amax[:, seg])
        m_last = big_m[:, L - 1:L]
        put(1, seg, big_m)
        put(2, seg, -(b[:, seg] + big_m))
        put(3, seg, m_prev - big_m)
        put(4, seg, a[:, seg] - m_last)
        m_prev = b[:, (c + 1) * L - 1:(c + 1) * L] + m_last


def _gate_layer(gates_t, bias, B, S, layer):
    return pl.pallas_call(
        _gate_kernel,
        grid=(1,),
        in_specs=[
            pl.BlockSpec((N_GATES, B * S), lambda i: (0, 0)),
            _resident((N_GATES, 1), layer),
        ],
        out_specs=pl.BlockSpec((B, N_GATE_ROWS, S), lambda i: (0, 0, 0)),
        out_shape=jax.ShapeDtypeStruct((B, N_GATE_ROWS, S), jnp.float32),
        compiler_params=_params("arbitrary"),
        name="mlstm_gates",
    )(gates_t, bias)


def _hi_lo_rows(row):
    hi = row.astype(jnp.bfloat16).astype(jnp.float32)
    idx = lax.broadcasted_iota(jnp.int32, (16, row.shape[1]), 0)
    slab = jnp.where(idx == 0, hi, jnp.where(idx == 1, row - hi, 0.0))
    return slab.astype(jnp.bfloat16)


def _mlstm_chunks(qt_ref, k_ref, vt_ref, r_ref, out_ref, c_ref, n_ref, n_chunks):
    L = MLSTM_CHUNK
    H = L // 2

    causal = (lax.broadcasted_iota(jnp.int32, (H, H), 0)
              <= lax.broadcasted_iota(jnp.int32, (H, H), 1))

    tok = lambda c: slice(c * L, (c + 1) * L)
    lanes = lambda ref, rows, c: ref[(c * L) // PROJ_TILE, rows,
                                     (c * L) % PROJ_TILE:(c * L) % PROJ_TILE + L]
    qk = lambda h: slice(h * DQK, (h + 1) * DQK)
    vv = lambda h: slice(h * DV, (h + 1) * DV)
    units = [(c, h) for c in range(n_chunks) for h in range(N_HEADS)]
    rows = [r_ref[0, :, tok(c)] for c in range(n_chunks)]
    a_cols = [r[0:2 * N_HEADS, :].T for r in rows]
    gate_row = lambda c, h, kind: rows[c][kind * N_HEADS + h:kind * N_HEADS + h + 1, :]
    ct = [c_ref[h] for h in range(N_HEADS)]
    n = [n_ref[h] for h in range(N_HEADS)]

    def scores(c, h):
        lhs = jnp.concatenate([k_ref[tok(c), qk(h)], ct[h].astype(jnp.bfloat16), _hi_lo_rows(n[h])], axis=0)
        return _dot(lhs, lanes(qt_ref, qk(h), c))

    def weigh(c, h, both):
        s, from_state = both[0:L, :], both[L:, :]
        a_col = a_cols[c][:, h:h + 1]
        big_m = gate_row(c, h, 1)
        diag = lambda i: jnp.exp(jnp.where(
            causal, a_col[i * H:(i + 1) * H, :] - big_m[:, i * H:(i + 1) * H], -jnp.inf))
        above = jnp.exp(a_col[0:H, :] - big_m[:, H:L])
        top = s[0:H, :] * jnp.concatenate([diag(0), above], axis=1)
        bottom = jnp.concatenate([jnp.zeros((H, H), jnp.float32), s[H:L, H:L] * diag(1)], axis=1)
        st = jnp.concatenate([top, bottom], axis=0)
        den_intra = jnp.sum(st, axis=0, keepdims=True)
        num_intra = _dot(lanes(vt_ref, vv(h), c), st.astype(jnp.bfloat16))
        return den_intra, num_intra, from_state

    def finish(c, h, den_intra, num_intra, from_state):
        inter = jnp.exp(gate_row(c, h, 3))
        w = jnp.exp(gate_row(c, h, 4))
        qn = from_state[DV:DV + 1, :] + from_state[DV + 1:DV + 2, :]
        num = inter * from_state[0:DV, :] + num_intra
        den = inter * qn + den_intra
        r = 1.0 / jnp.maximum(jnp.abs(den), jnp.exp(gate_row(c, h, 2)))
        ss = jnp.sum(num * num, axis=0, keepdims=True)
        scale = r * lax.rsqrt(r * r * ss * (1.0 / DV) + EPS)
        lane0 = (c * L) % FFN_TILE
        out_ref[(c * L) // FFN_TILE, vv(h), lane0:lane0 + L] = (num * scale).astype(out_ref.dtype)

        vw = jnp.concatenate([lanes(vt_ref, vv(h), c) * w.astype(jnp.bfloat16), _hi_lo_rows(w)], axis=0)
        upd = _dot(vw, k_ref[tok(c), qk(h)])
        decay = inter[:, L - 1:L]
        ct[h] = decay * ct[h] + upd[0:DV, :]
        n[h] = decay * n[h] + upd[DV:DV + 1, :] + upd[DV + 1:DV + 2, :]

    scored = {0: scores(*units[0])}
    if len(units) > 1:
        scored[1] = scores(*units[1])
    weighed = {0: weigh(*units[0], scored.pop(0))}
    for i, u in enumerate(units):
        if i + 2 < len(units):
            scored[i + 2] = scores(*units[i + 2])
        if i + 1 < len(units):
            weighed[i + 1] = weigh(*units[i + 1], scored.pop(i + 1))
        finish(*u, *weighed.pop(i))

    for h in range(N_HEADS):
        c_ref[h] = ct[h]
        n_ref[h] = n[h]


def kernel(x, pool_norm, pool_w, pool_scale, mlstm_norm, mlstm_w_in, mlstm_gate_bias,
           mlstm_head_norm, mlstm_w_out, ffn_norm, ffn_w_in, ffn_w_out, final_norm):
    B, S, D = x.shape
    depth = ffn_norm.shape[0]
    bf16 = jnp.bfloat16
    groups = pool_w.shape[:2] + (POOL_GROUP_DIM,)
    pool_w_b = (pool_norm.reshape(groups)[..., :, None] * pool_w
                * pool_scale.reshape(groups)[..., None, :]).astype(bf16)
    w_t_b = jnp.swapaxes(
        jnp.pad(mlstm_w_in * mlstm_norm[:, :, None], ((0, 0), (0, 0), (0, W_T_ROWS - mlstm_w_in.shape[2]))),
        1, 2).astype(bf16)
    head_norm_b = jnp.broadcast_to(mlstm_head_norm[:, :, None], mlstm_head_norm.shape + (LANE,))
    gate_bias = mlstm_gate_bias.reshape(-1, N_GATES, 1)
    w_mix_b = mlstm_w_out.astype(bf16)
    fgain = final_norm.reshape(1, D)
    ffn_gain_b = jnp.broadcast_to(ffn_norm[:, :, None], ffn_norm.shape + (LANE,))

    h2 = x.reshape(B * S, D)
    for i in range(depth):
        j = i // 2
        last = i == depth - 1
        if i % 2 == 0:
            h2 = _ffn_layer(h2, S, ffn_gain_b, ffn_w_in, ffn_w_out, fgain, i, last,
                            pool_w=pool_w_b, mix_layer=j)
        else:
            qt, k, vt, ogt, gates_t = _proj_layer(h2, w_t_b, head_norm_b, j)
            rows = _gate_layer(gates_t, gate_bias, B, S, j)
            h2 = _ffn_layer(h2, S, ffn_gain_b, ffn_w_in, ffn_w_out, fgain, i, last,
                            mlstm=(qt, k, vt, rows), out_gate=ogt, w_mix=w_mix_b, mix_layer=j)
    return h2.reshape(B, S, D)
```

```python
import functools
import math

import jax
import jax.numpy as jnp
from jax import lax
from jax.experimental import pallas as pl
from jax.experimental.pallas import tpu as pltpu

D_MODEL = 1024
POOL_WINDOWS = (2, 4, 8, 16)
POOL_GROUP_DIM = D_MODEL // len(POOL_WINDOWS)
POOL_HALO = 32

N_HEADS = 4
DV = D_MODEL // N_HEADS
DQK = DV // 2
QK_W = N_HEADS * DQK
V_W = N_HEADS * DV
MAIN_W = 2 * QK_W + 2 * V_W
N_GATES = 2 * N_HEADS
W_T_ROWS = MAIN_W + 16

D_FF = int(math.ceil(8 * D_MODEL / 3 / 256) * 256)
FF_CHUNK = 256
LANE = 128

EPS = 1e-6

POOL_BLOCK = 128
FFN_TILE = 512
PROJ_TILE = 1024
MLSTM_CHUNK = 256
N_GATE_ROWS = 24

VMEM_LIMIT = 56 * 1024 * 1024

_NT = (((1,), (1,)), ((), ()))


def _params(*semantics):
    return pltpu.CompilerParams(dimension_semantics=semantics,
                                vmem_limit_bytes=VMEM_LIMIT)


def _resident(shape, layer=None):
    zeros = (0,) * len(shape)
    if layer is None:
        return pl.BlockSpec(shape, lambda *_: zeros, pipeline_mode=pl.Buffered(1))
    return pl.BlockSpec((None,) + shape, lambda *_: (layer,) + zeros,
                        pipeline_mode=pl.Buffered(1))


def _rms_norm(x, gain):
    return x * lax.rsqrt(jnp.mean(x * x, axis=-1, keepdims=True) + EPS) * gain


def _dot(a, b, dims=None):
    if dims is None:
        return jnp.dot(a, b, preferred_element_type=jnp.float32)
    return lax.dot_general(a, b, dims, preferred_element_type=jnp.float32)


def _pool_rows(x_ref, r0, r1, tile_in_seq, ext_ref, stage_refs, pooled_ref):
    gd = POOL_GROUP_DIM
    lo, hi = POOL_HALO + r0, POOL_HALO + r1
    x = x_ref[r0:r1, :]
    xn = x * lax.rsqrt(jnp.mean(x * x, axis=-1, keepdims=True) + EPS)
    ext_ref[lo:hi, :] = xn

    prev = ext_ref
    for j, cur in enumerate(stage_refs, start=1):
        shift = 2 ** (j - 1)
        start = 8 * j if r0 == 0 else lo
        skip = 0 if j == 1 else gd
        cur[start:hi, :] = prev[start:hi, skip:] + prev[start - shift:hi - shift, skip:]
        prev = cur

    ts = x_ref.shape[0]
    pos = (tile_in_seq * ts + r0 + 1
           + lax.broadcasted_iota(jnp.int32, (r1 - r0, 1), 0)).astype(jnp.float32)
    for g, win in enumerate(POOL_WINDOWS):
        cols = slice(g * gd, (g + 1) * gd)
        inv_cnt = 1.0 / jnp.minimum(pos, float(win))
        pooled = stage_refs[g][lo:hi, 0:gd] * inv_cnt - xn[:, cols]
        pooled_ref[r0:r1, cols] = pooled.astype(jnp.bfloat16)


def _pool_finish(x_ref, w_ref, ext_ref, pooled_ref, out_ref):
    ts = x_ref.shape[0]
    gd = POOL_GROUP_DIM
    for g in range(len(POOL_WINDOWS)):
        cols = slice(g * gd, (g + 1) * gd)
        out_ref[:, cols] = x_ref[:, cols] + _dot(pooled_ref[:, cols], w_ref[g])
    ext_ref[0:POOL_HALO, :] = ext_ref[ts:ts + POOL_HALO, :]


def _pool_jobs(x_ref, tile_in_seq, w_ref, ext_ref, stage_refs, pooled_ref, out_ref):
    jobs = [functools.partial(_pool_rows, x_ref, r0, r0 + POOL_BLOCK, tile_in_seq, ext_ref,
                              stage_refs, pooled_ref)
            for r0 in range(0, x_ref.shape[0], POOL_BLOCK)]
    jobs.append(functools.partial(_pool_finish, x_ref, w_ref, ext_ref, pooled_ref, out_ref))
    return jobs


def _ffn_kernel(*refs, layer, mixer_proj, final_norm, tiles_per_seq):
    if mixer_proj:
        x_ref, qt_ref, k_ref, vt_ref, r_ref, og_ref, w_mix_ref, *refs = refs
    else:
        x_first_ref, x_next_ref, pool_w_ref, *refs = refs
    (gain_ref, w_in_hbm, w_out_hbm, fgain_ref, o_ref,
     w_in_ref, w_out_ref, act_ref, stage_in, stage_out, sem, *mixer_scratch) = refs
    n_chunks = D_FF // FF_CHUNK
    step = pl.program_id(0)
    if mixer_proj:
        heads_ref, c_ref, n_ref = mixer_scratch
        tiles_per_proj = PROJ_TILE // FFN_TILE

        @pl.when(step % tiles_per_seq == 0)
        def _():
            c_ref[...] = jnp.zeros(c_ref.shape, jnp.float32)
            n_ref[...] = jnp.zeros(n_ref.shape, jnp.float32)

        @pl.when(step % tiles_per_proj == 0)
        def _():
            _mlstm_chunks(qt_ref, k_ref, vt_ref, r_ref, heads_ref, c_ref, n_ref, PROJ_TILE // MLSTM_CHUNK)
    else:
        mixed_ref, cur_ref, pooled_ref, ext_ref, *stage_refs = mixer_scratch
        next_in_seq = (step + 1) % tiles_per_seq

        @pl.when(next_in_seq == 0)
        def _():
            ext_ref[0:POOL_HALO, :] = jnp.zeros((POOL_HALO, D_MODEL), jnp.float32)

    def chunk_copies(j, slot):
        lo = j * FF_CHUNK
        return (
            pltpu.make_async_copy(w_in_hbm.at[layer, :, lo:lo + FF_CHUNK],
                                  stage_in.at[slot, 0], sem.at[slot, 0]),
            pltpu.make_async_copy(w_in_hbm.at[layer, :, D_FF + lo:D_FF + lo + FF_CHUNK],
                                  stage_in.at[slot, 1], sem.at[slot, 1]),
            pltpu.make_async_copy(w_out_hbm.at[layer, lo:lo + FF_CHUNK, :],
                                  stage_out.at[slot], sem.at[slot, 2]),
        )

    def start_chunk(j, slot):
        for i, cp in enumerate(chunk_copies(j, slot)):
            cp.start(priority=min(i, 1))

    def body(stream_weights):
        if stream_weights:
            for j in range(min(2, n_chunks)):
                start_chunk(j, j)
        side_jobs = []
        if mixer_proj:
            heads = heads_ref[step % tiles_per_proj]
            x = x_ref[...] + _dot(heads * og_ref[...], w_mix_ref[...], (((0,), (0,)), ((), ())))
        else:
            if stream_weights:
                ext_ref[0:POOL_HALO, :] = jnp.zeros((POOL_HALO, D_MODEL), jnp.float32)
                for job in _pool_jobs(x_first_ref, 0, pool_w_ref, ext_ref, stage_refs, pooled_ref,
                                      mixed_ref):
                    job()
            cur_ref[...] = mixed_ref[...]
            x = cur_ref[...]
            side_jobs = _pool_jobs(x_next_ref, next_in_seq, pool_w_ref, ext_ref, stage_refs,
                                   pooled_ref, mixed_ref)
        xb = x.astype(jnp.bfloat16)
        rstd = lax.rsqrt(jnp.mean(x * x, axis=-1, keepdims=True) + EPS)
        for j in range(n_chunks):
            lo = j * FF_CHUNK
            if stream_weights:
                slot = j % 2
                for cp in chunk_copies(j, slot):
                    cp.wait()
                for part, col in ((0, lo), (1, D_FF + lo)):
                    for sub in range(0, FF_CHUNK, LANE):
                        w_in_ref[:, col + sub:col + sub + LANE] = (
                            stage_in[slot, part, :, sub:sub + LANE] * gain_ref[...]).astype(jnp.bfloat16)
                w_out_ref[lo:lo + FF_CHUNK, :] = stage_out[slot].astype(jnp.bfloat16)
                if j + 2 < n_chunks:
                    start_chunk(j + 2, slot)
            gate = _dot(xb, w_in_ref[:, lo:lo + FF_CHUNK]) * rstd
            up = _dot(xb, w_in_ref[:, D_FF + lo:D_FF + lo + FF_CHUNK]) * rstd
            act_ref[:, lo:lo + FF_CHUNK] = (gate * jax.nn.sigmoid(gate) * up).astype(jnp.bfloat16)
            if j < len(side_jobs):
                side_jobs[j]()
        out = x + _dot(act_ref[...], w_out_ref[...])
        if final_norm:
            out = _rms_norm(out, fgain_ref[...])
        o_ref[...] = out

    first = step == 0
    pl.when(first)(functools.partial(body, True))
    pl.when(jnp.logical_not(first))(functools.partial(body, False))


def _ffn_layer(h2, S, gain, w_in, w_out, final_gain, layer, final_norm, mlstm=None, out_gate=None,
               w_mix=None, pool_w=None, mix_layer=None):
    T, D = h2.shape
    tm = FFN_TILE
    nt = T // tm
    assert S // tm > 1 and tm % POOL_BLOCK == 0 and tm // POOL_BLOCK < D_FF // FF_CHUNK
    tok = pl.BlockSpec((tm, D), lambda i: (i, 0))
    hbm = pl.BlockSpec(memory_space=pl.ANY)
    mixer_proj = mlstm is not None
    if mixer_proj:
        tpp = PROJ_TILE // tm
        ppq = S // PROJ_TILE
        assert PROJ_TILE % tm == 0 and S % PROJ_TILE == 0
        proj_tile = lambda i: (i // tpp, 0, 0)
        mix_specs = [
            tok,
            pl.BlockSpec((1, QK_W, PROJ_TILE), proj_tile),
            pl.BlockSpec((PROJ_TILE, QK_W), lambda i: (i // tpp, 0)),
            pl.BlockSpec((1, V_W, PROJ_TILE), proj_tile),
            pl.BlockSpec((1, N_GATE_ROWS, PROJ_TILE), lambda i: (i // tpp // ppq, 0, i // tpp % ppq)),
            pl.BlockSpec((None, D, tm), lambda i: (i, 0, 0)),
            _resident(w_mix.shape[1:], mix_layer),
        ]
        mix_args = (h2, *mlstm, out_gate, w_mix)
        mixer_scratch = [
            pltpu.VMEM((tpp, V_W, tm), jnp.bfloat16),
            pltpu.VMEM((N_HEADS, DV, DQK), jnp.float32),
            pltpu.VMEM((N_HEADS, 1, DQK), jnp.float32),
        ]
    else:
        mix_specs = [pl.BlockSpec((tm, D), lambda i: (0, 0), pipeline_mode=pl.Buffered(1)),
                     pl.BlockSpec((tm, D), lambda i: (jnp.minimum(i + 1, nt - 1), 0)),
                     _resident(pool_w.shape[1:], mix_layer)]
        mix_args = (h2, h2, pool_w)
        mixer_scratch = [
            pltpu.VMEM((tm, D), jnp.float32),
            pltpu.VMEM((tm, D), jnp.float32),
            pltpu.VMEM((tm, D), jnp.bfloat16),
            pltpu.VMEM((POOL_HALO + tm, D), jnp.float32),
        ] + [pltpu.VMEM((POOL_HALO + tm, D - g * POOL_GROUP_DIM), jnp.float32)
             for g in range(len(POOL_WINDOWS))]
    return pl.pallas_call(
        functools.partial(_ffn_kernel, layer=layer, mixer_proj=mixer_proj, final_norm=final_norm,
                          tiles_per_seq=S // tm),
        grid=(nt,),
        in_specs=mix_specs + [_resident((D, LANE), layer), hbm, hbm, _resident((1, D))],
        out_specs=tok,
        out_shape=jax.ShapeDtypeStruct(h2.shape, h2.dtype),
        scratch_shapes=[
            pltpu.VMEM((D, 2 * D_FF), jnp.bfloat16),
            pltpu.VMEM((D_FF, D), jnp.bfloat16),
            pltpu.VMEM((tm, D_FF), jnp.bfloat16),
            pltpu.VMEM((2, 2, D, FF_CHUNK), jnp.float32),
            pltpu.VMEM((2, FF_CHUNK, D), jnp.float32),
            pltpu.SemaphoreType.DMA((2, 3)),
        ] + mixer_scratch,
        compiler_params=_params("arbitrary"),
        name="swiglu" + ("_mix" if mixer_proj else "_pool") + ("_final" if final_norm else ""),
    )(*mix_args, gain, w_in, w_out, final_gain)


def _proj_kernel(x_ref, wt_ref, hn_ref, qt_ref, k_ref, vt_ref, ogt_ref, gt_ref):
    x = x_ref[...]
    xb = x.astype(jnp.bfloat16)
    rstd = lax.rsqrt(jnp.mean(x * x, axis=-1, keepdims=True) + EPS)
    rstd_row = jnp.broadcast_to(rstd, (x.shape[0], LANE)).T[0:1, :]
    og = _dot(wt_ref[2 * QK_W + V_W:, :], xb, _NT) * rstd_row
    gt_ref[...] = og[V_W:V_W + N_GATES, :]
    for lo in range(0, og.shape[1], LANE):
        ogt_ref[lo // FFN_TILE, :, lo % FFN_TILE:lo % FFN_TILE + LANE] = (
            jax.nn.sigmoid(og[0:V_W, lo:lo + LANE]) * hn_ref[...]).astype(ogt_ref.dtype)
    vt_ref[...] = (_dot(wt_ref[2 * QK_W:2 * QK_W + V_W, :], xb, _NT) * rstd_row).astype(vt_ref.dtype)
    qt_ref[...] = (_dot(wt_ref[0:QK_W, :], xb, _NT) * (rstd_row * (DQK ** -0.5))).astype(qt_ref.dtype)
    k_ref[...] = (_dot(xb, wt_ref[QK_W:2 * QK_W, :], _NT) * rstd).astype(k_ref.dtype)


def _proj_layer(h2, w_t, head_norm_b, layer):
    T, D = h2.shape
    tm = PROJ_TILE
    tok = lambda i: (i, 0)
    lanes = lambda i: (0, i)
    tile = lambda i: (i, 0, 0)
    return pl.pallas_call(
        _proj_kernel,
        grid=(T // tm,),
        in_specs=[
            pl.BlockSpec((tm, D), tok),
            _resident(w_t.shape[1:], layer),
            _resident(head_norm_b.shape[1:], layer),
        ],
        out_specs=[
            pl.BlockSpec((None, QK_W, tm), tile),
            pl.BlockSpec((tm, QK_W), tok),
            pl.BlockSpec((None, V_W, tm), tile),
            pl.BlockSpec((tm // FFN_TILE, V_W, FFN_TILE), tile),
            pl.BlockSpec((N_GATES, tm), lanes),
        ],
        out_shape=[
            jax.ShapeDtypeStruct((T // tm, QK_W, tm), jnp.bfloat16),
            jax.ShapeDtypeStruct((T, QK_W), jnp.bfloat16),
            jax.ShapeDtypeStruct((T // tm, V_W, tm), jnp.bfloat16),
            jax.ShapeDtypeStruct((T // FFN_TILE, V_W, FFN_TILE), jnp.bfloat16),
            jax.ShapeDtypeStruct((N_GATES, T), jnp.float32),
        ],
        compiler_params=_params("arbitrary"),
        name="mlstm_proj",
    )(h2, w_t, head_norm_b)


def _segmented_scan(x, op, lane, seg):
    shift = 1
    while shift < seg:
        moved = pltpu.roll(x, shift, 1)
        x = jnp.where((lane % seg) >= shift, op(x, moved), x)
        shift *= 2
    return x


def _gate_kernel(gt_ref, bias_ref, r_ref):
    L = MLSTM_CHUNK
    B, _, S = r_ref.shape
    g = gt_ref[...] + bias_ref[...]
    per_seq = lambda lo: jnp.concatenate(
        [g[lo:lo + N_HEADS, s * S:(s + 1) * S] for s in range(B)], axis=0)
    i_pre = per_seq(0)
    f_pre = per_seq(N_HEADS)
    logf = jnp.minimum(f_pre, 0.0) - jnp.log1p(jnp.exp(-jnp.abs(f_pre)))
    lane = lax.broadcasted_iota(jnp.int32, (B * N_HEADS, S), 1)
    b = _segmented_scan(logf, jnp.add, lane, L)
    a = i_pre - b
    amax = _segmented_scan(a, jnp.maximum, lane, L)

    def put(kind, lanes, value):
        for s in range(B):
            r_ref[s, kind * N_HEADS:(kind + 1) * N_HEADS, lanes] = value[s * N_HEADS:(s + 1) * N_HEADS]

    put(0, slice(0, S), a)
    r_ref[:, 5 * N_HEADS:N_GATE_ROWS, :] = jnp.zeros((B, N_GATE_ROWS - 5 * N_HEADS, S), jnp.float32)
    m_prev = jnp.zeros((B * N_HEADS, 1), jnp.float32)
    for c in range(S // L):
        seg = slice(c * L, (c + 1) * L)
        big_m = jnp.maximum(m_prev, amax[:, seg])
        m_last = big_m[:, L - 1:L]
        put(1, seg, big_m)
        put(2, seg, -(b[:, seg] + big_m))
        put(3, seg, m_prev - big_m)
        put(4, seg, a[:, seg] - m_last)
        m_prev = b[:, (c + 1) * L - 1:(c + 1) * L] + m_last


def _gate_layer(gates_t, bias, B, S, layer):
    return pl.pallas_call(
        _gate_kernel,
        grid=(1,),
        in_specs=[
            pl.BlockSpec((N_GATES, B * S), lambda i: (0, 0)),
            _resident((N_GATES, 1), layer),
        ],
        out_specs=pl.BlockSpec((B, N_GATE_ROWS, S), lambda i: (0, 0, 0)),
        out_shape=jax.ShapeDtypeStruct((B, N_GATE_ROWS, S), jnp.float32),
        compiler_params=_params("arbitrary"),
        name="mlstm_gates",
    )(gates_t, bias)


def _hi_lo_rows(row):
    hi = row.astype(jnp.bfloat16).astype(jnp.float32)
    idx = lax.broadcasted_iota(jnp.int32, (16, row.shape[1]), 0)
    slab = jnp.where(idx == 0, hi, jnp.where(idx == 1, row - hi, 0.0))
    return slab.astype(jnp.bfloat16)


def _mlstm_chunks(qt_ref, k_ref, vt_ref, r_ref, out_ref, c_ref, n_ref, n_chunks):
    L = MLSTM_CHUNK
    H = L // 2

    causal = (lax.broadcasted_iota(jnp.int32, (H, H), 0)
              <= lax.broadcasted_iota(jnp.int32, (H, H), 1))

    tok = lambda c: slice(c * L, (c + 1) * L)
    lanes = lambda ref, rows, c: ref[(c * L) // PROJ_TILE, rows,
                                     (c * L) % PROJ_TILE:(c * L) % PROJ_TILE + L]
    qk = lambda h: slice(h * DQK, (h + 1) * DQK)
    vv = lambda h: slice(h * DV, (h + 1) * DV)
    units = [(c, h) for c in range(n_chunks) for h in range(N_HEADS)]
    rows = [r_ref[0, :, tok(c)] for c in range(n_chunks)]
    a_cols = [r[0:2 * N_HEADS, :].T for r in rows]
    gate_row = lambda c, h, kind: rows[c][kind * N_HEADS + h:kind * N_HEADS + h + 1, :]
    ct = [c_ref[h] for h in range(N_HEADS)]
    n = [n_ref[h] for h in range(N_HEADS)]

    def scores(c, h):
        lhs = jnp.concatenate([k_ref[tok(c), qk(h)], ct[h].astype(jnp.bfloat16), _hi_lo_rows(n[h])], axis=0)
        return _dot(lhs, lanes(qt_ref, qk(h), c))

    def weigh(c, h, both):
        s, from_state = both[0:L, :], both[L:, :]
        a_col = a_cols[c][:, h:h + 1]
        big_m = gate_row(c, h, 1)
        diag = lambda i: jnp.exp(jnp.where(
            causal, a_col[i * H:(i + 1) * H, :] - big_m[:, i * H:(i + 1) * H], -jnp.inf))
        above = jnp.exp(a_col[0:H, :] - big_m[:, H:L])
        top = s[0:H, :] * jnp.concatenate([diag(0), above], axis=1)
        bottom = jnp.concatenate([jnp.zeros((H, H), jnp.float32), s[H:L, H:L] * diag(1)], axis=1)
        st = jnp.concatenate([top, bottom], axis=0)
        den_intra = jnp.sum(st, axis=0, keepdims=True)
        num_intra = _dot(lanes(vt_ref, vv(h), c), st.astype(jnp.bfloat16))
        return den_intra, num_intra, from_state

    def finish(c, h, den_intra, num_intra, from_state):
        inter = jnp.exp(gate_row(c, h, 3))
        w = jnp.exp(gate_row(c, h, 4))
        qn = from_state[DV:DV + 1, :] + from_state[DV + 1:DV + 2, :]
        num = inter * from_state[0:DV, :] + num_intra
        den = inter * qn + den_intra
        r = 1.0 / jnp.maximum(jnp.abs(den), jnp.exp(gate_row(c, h, 2)))
        ss = jnp.sum(num * num, axis=0, keepdims=True)
        scale = r * lax.rsqrt(r * r * ss * (1.0 / DV) + EPS)
        lane0 = (c * L) % FFN_TILE
        out_ref[(c * L) // FFN_TILE, vv(h), lane0:lane0 + L] = (num * scale).astype(out_ref.dtype)

        vw = jnp.concatenate([lanes(vt_ref, vv(h), c) * w.astype(jnp.bfloat16), _hi_lo_rows(w)], axis=0)
        upd = _dot(vw, k_ref[tok(c), qk(h)])
        decay = inter[:, L - 1:L]
        ct[h] = decay * ct[h] + upd[0:DV, :]
        n[h] = decay * n[h] + upd[DV:DV + 1, :] + upd[DV + 1:DV + 2, :]

    scored = {0: scores(*units[0])}
    if len(units) > 1:
        scored[1] = scores(*units[1])
    weighed = {0: weigh(*units[0], scored.pop(0))}
    for i, u in enumerate(units):
        if i + 2 < len(units):
            scored[i + 2] = scores(*units[i + 2])
        if i + 1 < len(units):
            weighed[i + 1] = weigh(*units[i + 1], scored.pop(i + 1))
        finish(*u, *weighed.pop(i))

    for h in range(N_HEADS):
        c_ref[h] = ct[h]
        n_ref[h] = n[h]


def kernel(x, pool_norm, pool_w, pool_scale, mlstm_norm, mlstm_w_in, mlstm_gate_bias,
           mlstm_head_norm, mlstm_w_out, ffn_norm, ffn_w_in, ffn_w_out, final_norm):
    B, S, D = x.shape
    depth = ffn_norm.shape[0]
    bf16 = jnp.bfloat16
    groups = pool_w.shape[:2] + (POOL_GROUP_DIM,)
    pool_w_b = (pool_norm.reshape(groups)[..., :, None] * pool_w
                * pool_scale.reshape(groups)[..., None, :]).astype(bf16)
    w_t_b = jnp.swapaxes(
        jnp.pad(mlstm_w_in * mlstm_norm[:, :, None], ((0, 0), (0, 0), (0, W_T_ROWS - mlstm_w_in.shape[2]))),
        1, 2).astype(bf16)
    head_norm_b = jnp.broadcast_to(mlstm_head_norm[:, :, None], mlstm_head_norm.shape + (LANE,))
    gate_bias = mlstm_gate_bias.reshape(-1, N_GATES, 1)
    w_mix_b = mlstm_w_out.astype(bf16)
    fgain = final_norm.reshape(1, D)
    ffn_gain_b = jnp.broadcast_to(ffn_norm[:, :, None], ffn_norm.shape + (LANE,))

    h2 = x.reshape(B * S, D)
    for i in range(depth):
        j = i // 2
        last = i == depth - 1
        if i % 2 == 0:
            h2 = _ffn_layer(h2, S, ffn_gain_b, ffn_w_in, ffn_w_out, fgain, i, last,
                            pool_w=pool_w_b, mix_layer=j)
        else:
            qt, k, vt, ogt, gates_t = _proj_layer(h2, w_t_b, head_norm_b, j)
            rows = _gate_layer(gates_t, gate_bias, B, S, j)
            h2 = _ffn_layer(h2, S, ffn_gain_b, ffn_w_in, ffn_w_out, fgain, i, last,
                            mlstm=(qt, k, vt, rows), out_gate=ogt, w_mix=w_mix_b, mix_layer=j)
    return h2.reshape(B, S, D)
```

```python
import functools
import math

import jax
import jax.numpy as jnp
from jax import lax
from jax.experimental import pallas as pl
from jax.experimental.pallas import tpu as pltpu

D_MODEL = 1024
POOL_WINDOWS = (2, 4, 8, 16)
POOL_GROUP_DIM = D_MODEL // len(POOL_WINDOWS)
POOL_HALO = 32

N_HEADS = 4
DV = D_MODEL // N_HEADS
DQK = DV // 2
QK_W = N_HEADS * DQK
V_W = N_HEADS * DV
MAIN_W = 2 * QK_W + 2 * V_W
N_GATES = 2 * N_HEADS
W_T_ROWS = MAIN_W + 16

D_FF = int(math.ceil(8 * D_MODEL / 3 / 256) * 256)
FF_CHUNK = 256
LANE = 128

EPS = 1e-6

POOL_BLOCK = 128
FFN_TILE = 512
PROJ_TILE = 1024
MLSTM_CHUNK = 256
N_GATE_ROWS = 24

VMEM_LIMIT = 56 * 1024 * 1024

_NT = (((1,), (1,)), ((), ()))


def _params(*semantics):
    return pltpu.CompilerParams(dimension_semantics=semantics,
                                vmem_limit_bytes=VMEM_LIMIT)


def _resident(shape, layer=None):
    zeros = (0,) * len(shape)
    if layer is None:
        return pl.BlockSpec(shape, lambda *_: zeros, pipeline_mode=pl.Buffered(1))
    return pl.BlockSpec((None,) + shape, lambda *_: (layer,) + zeros,
                        pipeline_mode=pl.Buffered(1))


def _rms_norm(x, gain):
    return x * lax.rsqrt(jnp.mean(x * x, axis=-1, keepdims=True) + EPS) * gain


def _dot(a, b, dims=None):
    if dims is None:
        return jnp.dot(a, b, preferred_element_type=jnp.float32)
    return lax.dot_general(a, b, dims, preferred_element_type=jnp.float32)


def _pool_rows(x_ref, r0, r1, tile_in_seq, ext_ref, stage_refs, pooled_ref):
    gd = POOL_GROUP_DIM
    lo, hi = POOL_HALO + r0, POOL_HALO + r1
    x = x_ref[r0:r1, :]
    xn = x * lax.rsqrt(jnp.mean(x * x, axis=-1, keepdims=True) + EPS)
    ext_ref[lo:hi, :] = xn

    prev = ext_ref
    for j, cur in enumerate(stage_refs, start=1):
        shift = 2 ** (j - 1)
        start = 8 * j if r0 == 0 else lo
        skip = 0 if j == 1 else gd
        cur[start:hi, :] = prev[start:hi, skip:] + prev[start - shift:hi - shift, skip:]
        prev = cur

    ts = x_ref.shape[0]
    pos = (tile_in_seq * ts + r0 + 1
           + lax.broadcasted_iota(jnp.int32, (r1 - r0, 1), 0)).astype(jnp.float32)
    for g, win in enumerate(POOL_WINDOWS):
        cols = slice(g * gd, (g + 1) * gd)
        inv_cnt = 1.0 / jnp.minimum(pos, float(win))
        pooled = stage_refs[g][lo:hi, 0:gd] * inv_cnt - xn[:, cols]
        pooled_ref[r0:r1, cols] = pooled.astype(jnp.bfloat16)


def _pool_finish(x_ref, w_ref, ext_ref, pooled_ref, out_ref):
    ts = x_ref.shape[0]
    gd = POOL_GROUP_DIM
    for g in range(len(POOL_WINDOWS)):
        cols = slice(g * gd, (g + 1) * gd)
        out_ref[:, cols] = x_ref[:, cols] + _dot(pooled_ref[:, cols], w_ref[g])
    ext_ref[0:POOL_HALO, :] = ext_ref[ts:ts + POOL_HALO, :]


def _pool_jobs(x_ref, tile_in_seq, w_ref, ext_ref, stage_refs, pooled_ref, out_ref):
    jobs = [functools.partial(_pool_rows, x_ref, r0, r0 + POOL_BLOCK, tile_in_seq, ext_ref,
                              stage_refs, pooled_ref)
            for r0 in range(0, x_ref.shape[0], POOL_BLOCK)]
    jobs.append(functools.partial(_pool_finish, x_ref, w_ref, ext_ref, pooled_ref, out_ref))
    return jobs


def _ffn_kernel(*refs, layer, mixer_proj, final_norm, tiles_per_seq):
    if mixer_proj:
        x_ref, qt_ref, k_ref, vt_ref, r_ref, og_ref, w_mix_ref, *refs = refs
    else:
        x_first_ref, x_next_ref, pool_w_ref, *refs = refs
    (gain_ref, w_in_hbm, w_out_hbm, fgain_ref, o_ref,
     w_in_ref, w_out_ref, act_ref, stage_in, stage_out, sem, *mixer_scratch) = refs
    n_chunks = D_FF // FF_CHUNK
    step = pl.program_id(0)
    if mixer_proj:
        heads_ref, c_ref, n_ref = mixer_scratch
        tiles_per_proj = PROJ_TILE // FFN_TILE

        @pl.when(step % tiles_per_seq == 0)
        def _():
            c_ref[...] = jnp.zeros(c_ref.shape, jnp.float32)
            n_ref[...] = jnp.zeros(n_ref.shape, jnp.float32)

        @pl.when(step % tiles_per_proj == 0)
        def _():
            _mlstm_chunks(qt_ref, k_ref, vt_ref, r_ref, heads_ref, c_ref, n_ref, PROJ_TILE // MLSTM_CHUNK)
    else:
        mixed_ref, cur_ref, pooled_ref, ext_ref, *stage_refs = mixer_scratch
        next_in_seq = (step + 1) % tiles_per_seq

        @pl.when(next_in_seq == 0)
        def _():
            ext_ref[0:POOL_HALO, :] = jnp.zeros((POOL_HALO, D_MODEL), jnp.float32)

    def chunk_copies(j, slot):
        lo = j * FF_CHUNK
        return (
            pltpu.make_async_copy(w_in_hbm.at[layer, :, lo:lo + FF_CHUNK],
                                  stage_in.at[slot, 0], sem.at[slot, 0]),
            pltpu.make_async_copy(w_in_hbm.at[layer, :, D_FF + lo:D_FF + lo + FF_CHUNK],
                                  stage_in.at[slot, 1], sem.at[slot, 1]),
            pltpu.make_async_copy(w_out_hbm.at[layer, lo:lo + FF_CHUNK, :],
                                  stage_out.at[slot], sem.at[slot, 2]),
        )

    def body(stream_weights):
        if stream_weights:
            for j in range(min(3, n_chunks)):
                for cp in chunk_copies(j, j):
                    cp.start()
        side_jobs = []
        if mixer_proj:
            heads = heads_ref[step % tiles_per_proj]
            x = x_ref[...] + _dot(heads * og_ref[...], w_mix_ref[...], (((0,), (0,)), ((), ())))
        else:
            if stream_weights:
                ext_ref[0:POOL_HALO, :] = jnp.zeros((POOL_HALO, D_MODEL), jnp.float32)
                for job in _pool_jobs(x_first_ref, 0, pool_w_ref, ext_ref, stage_refs, pooled_ref,
                                      mixed_ref):
                    job()
            cur_ref[...] = mixed_ref[...]
            x = cur_ref[...]
            side_jobs = _pool_jobs(x_next_ref, next_in_seq, pool_w_ref, ext_ref, stage_refs,
                                   pooled_ref, mixed_ref)
        xb = x.astype(jnp.bfloat16)
        rstd = lax.rsqrt(jnp.mean(x * x, axis=-1, keepdims=True) + EPS)
        for j in range(n_chunks):
            lo = j * FF_CHUNK
            if stream_weights:
                slot = j % 3
                for cp in chunk_copies(j, slot):
                    cp.wait()
                for part, col in ((0, lo), (1, D_FF + lo)):
                    for sub in range(0, FF_CHUNK, LANE):
                        w_in_ref[:, col + sub:col + sub + LANE] = (
                            stage_in[slot, part, :, sub:sub + LANE] * gain_ref[...]).astype(jnp.bfloat16)
                w_out_ref[lo:lo + FF_CHUNK, :] = stage_out[slot].astype(jnp.bfloat16)
                if j + 3 < n_chunks:
                    for cp in chunk_copies(j + 3, slot):
                        cp.start()
            gate = _dot(xb, w_in_ref[:, lo:lo + FF_CHUNK]) * rstd
            up = _dot(xb, w_in_ref[:, D_FF + lo:D_FF + lo + FF_CHUNK]) * rstd
            act_ref[:, lo:lo + FF_CHUNK] = (gate * jax.nn.sigmoid(gate) * up).astype(jnp.bfloat16)
            if j < len(side_jobs):
                side_jobs[j]()
        out = x + _dot(act_ref[...], w_out_ref[...])
        if final_norm:
            out = _rms_norm(out, fgain_ref[...])
        o_ref[...] = out

    first = step == 0
    pl.when(first)(functools.partial(body, True))
    pl.when(jnp.logical_not(first))(functools.partial(body, False))


def _ffn_layer(h2, S, gain, w_in, w_out, final_gain, layer, final_norm, mlstm=None, out_gate=None,
               w_mix=None, pool_w=None, mix_layer=None):
    T, D = h2.shape
    tm = FFN_TILE
    nt = T // tm
    assert S // tm > 1 and tm % POOL_BLOCK == 0 and tm // POOL_BLOCK < D_FF // FF_CHUNK
    tok = pl.BlockSpec((tm, D), lambda i: (i, 0))
    hbm = pl.BlockSpec(memory_space=pl.ANY)
    mixer_proj = mlstm is not None
    if mixer_proj:
        tpp = PROJ_TILE // tm
        ppq = S // PROJ_TILE
        assert PROJ_TILE % tm == 0 and S % PROJ_TILE == 0
        proj_tile = lambda i: (i // tpp, 0, 0)
        mix_specs = [
            tok,
            pl.BlockSpec((1, QK_W, PROJ_TILE), proj_tile),
            pl.BlockSpec((PROJ_TILE, QK_W), lambda i: (i // tpp, 0)),
            pl.BlockSpec((1, V_W, PROJ_TILE), proj_tile),
            pl.BlockSpec((1, N_GATE_ROWS, PROJ_TILE), lambda i: (i // tpp // ppq, 0, i // tpp % ppq)),
            pl.BlockSpec((None, D, tm), lambda i: (i, 0, 0)),
            _resident(w_mix.shape[1:], mix_layer),
        ]
        mix_args = (h2, *mlstm, out_gate, w_mix)
        mixer_scratch = [
            pltpu.VMEM((tpp, V_W, tm), jnp.bfloat16),
            pltpu.VMEM((N_HEADS, DV, DQK), jnp.float32),
            pltpu.VMEM((N_HEADS, 1, DQK), jnp.float32),
        ]
    else:
        mix_specs = [pl.BlockSpec((tm, D), lambda i: (0, 0), pipeline_mode=pl.Buffered(1)),
                     pl.BlockSpec((tm, D), lambda i: (jnp.minimum(i + 1, nt - 1), 0)),
                     _resident(pool_w.shape[1:], mix_layer)]
        mix_args = (h2, h2, pool_w)
        mixer_scratch = [
            pltpu.VMEM((tm, D), jnp.float32),
            pltpu.VMEM((tm, D), jnp.float32),
            pltpu.VMEM((tm, D), jnp.bfloat16),
            pltpu.VMEM((POOL_HALO + tm, D), jnp.float32),
        ] + [pltpu.VMEM((POOL_HALO + tm, D - g * POOL_GROUP_DIM), jnp.float32)
             for g in range(len(POOL_WINDOWS))]
    return pl.pallas_call(
        functools.partial(_ffn_kernel, layer=layer, mixer_proj=mixer_proj, final_norm=final_norm,
                          tiles_per_seq=S // tm),
        grid=(nt,),
        in_specs=mix_specs + [_resident((D, LANE), layer), hbm, hbm, _resident((1, D))],
        out_specs=tok,
        out_shape=jax.ShapeDtypeStruct(h2.shape, h2.dtype),
        scratch_shapes=[
            pltpu.VMEM((D, 2 * D_FF), jnp.bfloat16),
            pltpu.VMEM((D_FF, D), jnp.bfloat16),
            pltpu.VMEM((tm, D_FF), jnp.bfloat16),
            pltpu.VMEM((3, 2, D, FF_CHUNK), jnp.float32),
            pltpu.VMEM((3, FF_CHUNK, D), jnp.float32),
            pltpu.SemaphoreType.DMA((3, 3)),
        ] + mixer_scratch,
        compiler_params=_params("arbitrary"),
        name="swiglu" + ("_mix" if mixer_proj else "_pool") + ("_final" if final_norm else ""),
    )(*mix_args, gain, w_in, w_out, final_gain)


def _proj_kernel(x_ref, wt_ref, hn_ref, qt_ref, k_ref, vt_ref, ogt_ref, gt_ref):
    x = x_ref[...]
    xb = x.astype(jnp.bfloat16)
    rstd = lax.rsqrt(jnp.mean(x * x, axis=-1, keepdims=True) + EPS)
    rstd_row = jnp.broadcast_to(rstd, (x.shape[0], LANE)).T[0:1, :]
    og = _dot(wt_ref[2 * QK_W + V_W:, :], xb, _NT) * rstd_row
    gt_ref[...] = og[V_W:V_W + N_GATES, :]
    for lo in range(0, og.shape[1], LANE):
        ogt_ref[lo // FFN_TILE, :, lo % FFN_TILE:lo % FFN_TILE + LANE] = (
            jax.nn.sigmoid(og[0:V_W, lo:lo + LANE]) * hn_ref[...]).astype(ogt_ref.dtype)
    vt_ref[...] = (_dot(wt_ref[2 * QK_W:2 * QK_W + V_W, :], xb, _NT) * rstd_row).astype(vt_ref.dtype)
    qt_ref[...] = (_dot(wt_ref[0:QK_W, :], xb, _NT) * (rstd_row * (DQK ** -0.5))).astype(qt_ref.dtype)
    k_ref[...] = (_dot(xb, wt_ref[QK_W:2 * QK_W, :], _NT) * rstd).astype(k_ref.dtype)


def _proj_layer(h2, w_t, head_norm_b, layer):
    T, D = h2.shape
    tm = PROJ_TILE
    tok = lambda i: (i, 0)
    lanes = lambda i: (0, i)
    tile = lambda i: (i, 0, 0)
    return pl.pallas_call(
        _proj_kernel,
        grid=(T // tm,),
        in_specs=[
            pl.BlockSpec((tm, D), tok),
            _resident(w_t.shape[1:], layer),
            _resident(head_norm_b.shape[1:], layer),
        ],
        out_specs=[
            pl.BlockSpec((None, QK_W, tm), tile),
            pl.BlockSpec((tm, QK_W), tok),
            pl.BlockSpec((None, V_W, tm), tile),
            pl.BlockSpec((tm // FFN_TILE, V_W, FFN_TILE), tile),
            pl.BlockSpec((N_GATES, tm), lanes),
        ],
        out_shape=[
            jax.ShapeDtypeStruct((T // tm, QK_W, tm), jnp.bfloat16),
            jax.ShapeDtypeStruct((T, QK_W), jnp.bfloat16),
            jax.ShapeDtypeStruct((T // tm, V_W, tm), jnp.bfloat16),
            jax.ShapeDtypeStruct((T // FFN_TILE, V_W, FFN_TILE), jnp.bfloat16),
            jax.ShapeDtypeStruct((N_GATES, T), jnp.float32),
        ],
        compiler_params=_params("arbitrary"),
        name="mlstm_proj",
    )(h2, w_t, head_norm_b)


def _segmented_scan(x, op, lane, seg):
    shift = 1
    while shift < seg:
        moved = pltpu.roll(x, shift, 1)
        x = jnp.where((lane % seg) >= shift, op(x, moved), x)
        shift *= 2
    return x


def _gate_kernel(gt_ref, bias_ref, r_ref):
    L = MLSTM_CHUNK
    B, _, S = r_ref.shape
    g = gt_ref[...] + bias_ref[...]
    per_seq = lambda lo: jnp.concatenate(
        [g[lo:lo + N_HEADS, s * S:(s + 1) * S] for s in range(B)], axis=0)
    i_pre = per_seq(0)
    f_pre = per_seq(N_HEADS)
    logf = jnp.minimum(f_pre, 0.0) - jnp.log1p(jnp.exp(-jnp.abs(f_pre)))
    lane = lax.broadcasted_iota(jnp.int32, (B * N_HEADS, S), 1)
    b = _segmented_scan(logf, jnp.add, lane, L)
    a = i_pre - b
    amax = _segmented_scan(a, jnp.maximum, lane, L)

    def put(kind, lanes, value):
        for s in range(B):
            r_ref[s, kind * N_HEADS:(kind + 1) * N_HEADS, lanes] = value[s * N_HEADS:(s + 1) * N_HEADS]

    put(0, slice(0, S), a)
    r_ref[:, 5 * N_HEADS:N_GATE_ROWS, :] = jnp.zeros((B, N_GATE_ROWS - 5 * N_HEADS, S), jnp.float32)
    m_prev = jnp.zeros((B * N_HEADS, 1), jnp.float32)
    for c in range(S // L):
        seg = slice(c * L, (c + 1) * L)
        big_m = jnp.maximum(m_prev, amax[:, seg])
        m_last = big_m[:, L - 1:L]
        put(1, seg, big_m)
        put(2, seg, -(b[:, seg] + big_m))
        put(3, seg, m_prev - big_m)
        put(4, seg, a[:, seg] - m_last)
        m_prev = b[:, (c + 1) * L - 1:(c + 1) * L] + m_last


def _gate_layer(gates_t, bias, B, S, layer):
    return pl.pallas_call(
        _gate_kernel,
        grid=(1,),
        in_specs=[
            pl.BlockSpec((N_GATES, B * S), lambda i: (0, 0)),
            _resident((N_GATES, 1), layer),
        ],
        out_specs=pl.BlockSpec((B, N_GATE_ROWS, S), lambda i: (0, 0, 0)),
        out_shape=jax.ShapeDtypeStruct((B, N_GATE_ROWS, S), jnp.float32),
        compiler_params=_params("arbitrary"),
        name="mlstm_gates",
    )(gates_t, bias)


def _hi_lo_rows(row):
    hi = row.astype(jnp.bfloat16).astype(jnp.float32)
    idx = lax.broadcasted_iota(jnp.int32, (16, row.shape[1]), 0)
    slab = jnp.where(idx == 0, hi, jnp.where(idx == 1, row - hi, 0.0))
    return slab.astype(jnp.bfloat16)


def _mlstm_chunks(qt_ref, k_ref, vt_ref, r_ref, out_ref, c_ref, n_ref, n_chunks):
    L = MLSTM_CHUNK
    H = L // 2

    causal = (lax.broadcasted_iota(jnp.int32, (H, H), 0)
              <= lax.broadcasted_iota(jnp.int32, (H, H), 1))

    tok = lambda c: slice(c * L, (c + 1) * L)
    lanes = lambda ref, rows, c: ref[(c * L) // PROJ_TILE, rows,
                                     (c * L) % PROJ_TILE:(c * L) % PROJ_TILE + L]
    qk = lambda h: slice(h * DQK, (h + 1) * DQK)
    vv = lambda h: slice(h * DV, (h + 1) * DV)
    units = [(c, h) for c in range(n_chunks) for h in range(N_HEADS)]
    rows = [r_ref[0, :, tok(c)] for c in range(n_chunks)]
    a_cols = [r[0:2 * N_HEADS, :].T for r in rows]
    gate_row = lambda c, h, kind: rows[c][kind * N_HEADS + h:kind * N_HEADS + h + 1, :]
    ct = [c_ref[h] for h in range(N_HEADS)]
    n = [n_ref[h] for h in range(N_HEADS)]

    def scores(c, h):
        lhs = jnp.concatenate([k_ref[tok(c), qk(h)], ct[h].astype(jnp.bfloat16), _hi_lo_rows(n[h])], axis=0)
        return _dot(lhs, lanes(qt_ref, qk(h), c))

    def weigh(c, h, both):
        s, from_state = both[0:L, :], both[L:, :]
        a_col = a_cols[c][:, h:h + 1]
        big_m = gate_row(c, h, 1)
        diag = lambda i: jnp.exp(jnp.where(
            causal, a_col[i * H:(i + 1) * H, :] - big_m[:, i * H:(i + 1) * H], -jnp.inf))
        above = jnp.exp(a_col[0:H, :] - big_m[:, H:L])
        top = s[0:H, :] * jnp.concatenate([diag(0), above], axis=1)
        bottom = jnp.concatenate([jnp.zeros((H, H), jnp.float32), s[H:L, H:L] * diag(1)], axis=1)
        st = jnp.concatenate([top, bottom], axis=0)
        den_intra = jnp.sum(st, axis=0, keepdims=True)
        num_intra = _dot(lanes(vt_ref, vv(h), c), st.astype(jnp.bfloat16))
        return den_intra, num_intra, from_state

    def finish(c, h, den_intra, num_intra, from_state):
        inter = jnp.exp(gate_row(c, h, 3))
        w = jnp.exp(gate_row(c, h, 4))
        qn = from_state[DV:DV + 1, :] + from_state[DV + 1:DV + 2, :]
        num = inter * from_state[0:DV, :] + num_intra
        den = inter * qn + den_intra
        r = 1.0 / jnp.maximum(jnp.abs(den), jnp.exp(gate_row(c, h, 2)))
        ss = jnp.sum(num * num, axis=0, keepdims=True)
        scale = r * lax.rsqrt(r * r * ss * (1.0 / DV) + EPS)
        lane0 = (c * L) % FFN_TILE
        out_ref[(c * L) // FFN_TILE, vv(h), lane0:lane0 + L] = (num * scale).astype(out_ref.dtype)

        vw = jnp.concatenate([lanes(vt_ref, vv(h), c) * w.astype(jnp.bfloat16), _hi_lo_rows(w)], axis=0)
        upd = _dot(vw, k_ref[tok(c), qk(h)])
        decay = inter[:, L - 1:L]
        ct[h] = decay * ct[h] + upd[0:DV, :]
        n[h] = decay * n[h] + upd[DV:DV + 1, :] + upd[DV + 1:DV + 2, :]

    scored = {0: scores(*units[0])}
    if len(units) > 1:
        scored[1] = scores(*units[1])
    weighed = {0: weigh(*units[0], scored.pop(0))}
    for i, u in enumerate(units):
        if i + 2 < len(units):
            scored[i + 2] = scores(*units[i + 2])
        if i + 1 < len(units):
            weighed[i + 1] = weigh(*units[i + 1], scored.pop(i + 1))
        finish(*u, *weighed.pop(i))

    for h in range(N_HEADS):
        c_ref[h] = ct[h]
        n_ref[h] = n[h]


def kernel(x, pool_norm, pool_w, pool_scale, mlstm_norm, mlstm_w_in, mlstm_gate_bias,
           mlstm_head_norm, mlstm_w_out, ffn_norm, ffn_w_in, ffn_w_out, final_norm):
    B, S, D = x.shape
    depth = ffn_norm.shape[0]
    bf16 = jnp.bfloat16
    groups = pool_w.shape[:2] + (POOL_GROUP_DIM,)
    pool_w_b = (pool_norm.reshape(groups)[..., :, None] * pool_w
                * pool_scale.reshape(groups)[..., None, :]).astype(bf16)
    w_t_b = jnp.swapaxes(
        jnp.pad(mlstm_w_in * mlstm_norm[:, :, None], ((0, 0), (0, 0), (0, W_T_ROWS - mlstm_w_in.shape[2]))),
        1, 2).astype(bf16)
    head_norm_b = jnp.broadcast_to(mlstm_head_norm[:, :, None], mlstm_head_norm.shape + (LANE,))
    gate_bias = mlstm_gate_bias.reshape(-1, N_GATES, 1)
    w_mix_b = mlstm_w_out.astype(bf16)
    fgain = final_norm.reshape(1, D)
    ffn_gain_b = jnp.broadcast_to(ffn_norm[:, :, None], ffn_norm.shape + (LANE,))

    h2 = x.reshape(B * S, D)
    for i in range(depth):
        j = i // 2
        last = i == depth - 1
        if i % 2 == 0:
            h2 = _ffn_layer(h2, S, ffn_gain_b, ffn_w_in, ffn_w_out, fgain, i, last,
                            pool_w=pool_w_b, mix_layer=j)
        else:
            qt, k, vt, ogt, gates_t = _proj_layer(h2, w_t_b, head_norm_b, j)
            rows = _gate_layer(gates_t, gate_bias, B, S, j)
            h2 = _ffn_layer(h2, S, ffn_gain_b, ffn_w_in, ffn_w_out, fgain, i, last,
                            mlstm=(qt, k, vt, rows), out_gate=ogt, w_mix=w_mix_b, mix_layer=j)
    return h2.reshape(B, S, D)
```
